```python
import math
import jax, jax.numpy as jnp
from jax import lax
import numpy as np

D_MODEL = 2048
BATCH = 4
SEQ = 2048
DEPTH = 2
DEC_BATCH = 128
DEC_SEQ = 8
PAST_LEN = 16384
PAGE_SIZE = 128

MIX_W = D_MODEL // 2
HG_DK = 128
HG_HEADS = MIX_W // HG_DK
HG_DV = MIX_W // HG_HEADS
HG_CHUNK = 16
S5_CH = 16
S5_GROUPS = MIX_W // S5_CH
S5_STATE = 64
S5_DT_MIN = 1e-3
S5_DT_MAX = 1e-1
ML_HEADS = 4
ML_DH = MIX_W // ML_HEADS
ML_CHUNK = 64
CONV_W = 4
N_BRANCH = 3
D_FF = 256 * ((8 * D_MODEL // 3 + 255) // 256)
IN_WIDTHS = (MIX_W, MIX_W, MIX_W, MIX_W, MIX_W, MIX_W, MIX_W, N_BRANCH * D_MODEL)
N_IN = sum(IN_WIDTHS)
EPS = 1e-6
NEG_BIG = -1e30

kernel_name = 'hybrid_hgrn2_s5_mlstm_macaron_step'


def rmsnorm(x, g):
    x32 = x.astype(jnp.float32)
    y = x32 * lax.rsqrt(jnp.mean(x32 * x32, axis=-1, keepdims=True) + EPS)
    return (y * g.astype(jnp.float32)).astype(x.dtype)


def head_rms(x):
    return x * lax.rsqrt(jnp.mean(x * x, axis=-1, keepdims=True) + EPS)


def swiglu(h, w_up, w_down):
    a, b = jnp.split(jnp.einsum('bld,df->blf', h, w_up), 2, axis=-1)
    return jnp.einsum('blf,fd->bld', jax.nn.silu(a) * b, w_down)


def gla_chunked(q, k, v, logf, s0):
    bsz, seq, nh, _ = q.shape
    dv = v.shape[-1]
    pad = (-seq) % HG_CHUNK
    widths = ((0, 0), (0, pad), (0, 0), (0, 0))
    q, k, v, logf = (jnp.pad(a, widths) for a in (q, k, v, logf))
    nc = (seq + pad) // HG_CHUNK

    def blocks(a):
        return a.reshape(bsz, nc, HG_CHUNK, nh, a.shape[-1]).transpose(1, 0, 3, 2, 4)

    causal = jnp.tril(jnp.ones((HG_CHUNK, HG_CHUNK), dtype=bool))[:, :, None]

    def step(state, blk):
        qc, kc, vc, gc = blk
        b = jnp.cumsum(gc, axis=2)
        diff = b[:, :, :, None, :] - b[:, :, None, :, :]
        decay = jnp.where(causal, jnp.exp(jnp.where(causal, diff, 0.0)), 0.0)
        scores = jnp.einsum('bhtk,bhsk,bhtsk->bhts', qc, kc, decay)
        out = (jnp.einsum('bhts,bhsv->bhtv', scores, vc)
               + jnp.einsum('bhtk,bhkv->bhtv', qc * jnp.exp(b), state))
        b_end = b[:, :, -1:, :]
        state = (state * jnp.exp(b[:, :, -1, :])[..., None]
                 + jnp.einsum('bhsk,bhsv->bhkv', kc * jnp.exp(b_end - b), vc))
        return state, out

    s_final, out = lax.scan(step, s0, tuple(blocks(a) for a in (q, k, v, logf)))
    out = out.transpose(1, 0, 3, 2, 4).reshape(bsz, nc * HG_CHUNK, nh, dv)[:, :seq]
    return out, s_final


def mlstm_chunked(q, k, v, ig, logf, c0, n0, m0):
    bsz, nh, seq, dh = q.shape
    pad = (-seq) % ML_CHUNK
    pad4 = ((0, 0), (0, 0), (0, pad), (0, 0))
    pad3 = ((0, 0), (0, 0), (0, pad))
    q, k, v = (jnp.pad(a, pad4) for a in (q, k, v))
    logf = jnp.pad(logf, pad3)
    ig = jnp.pad(ig, pad3, constant_values=NEG_BIG)
    nc = (seq + pad) // ML_CHUNK

    def blocks(a):
        return jnp.moveaxis(a.reshape(a.shape[:2] + (nc, ML_CHUNK) + a.shape[3:]), 2, 0)

    causal = jnp.tril(jnp.ones((ML_CHUNK, ML_CHUNK), dtype=bool))

    def step(carry, blk):
        c, n, m = carry
        qc, kc, vc, ic, fc = blk
        b = jnp.cumsum(fc, axis=-1)
        logw = jnp.where(causal, b[..., :, None] - b[..., None, :] + ic[..., None, :], NEG_BIG)
        m_t = jnp.maximum(b + m[..., None], jnp.max(logw, axis=-1))
        w_prev = jnp.exp(b + m[..., None] - m_t)
        w = jnp.exp(logw - m_t[..., None])
        s = jnp.einsum('bhtd,bhsd->bhts', qc, kc) * w
        num = (w_prev[..., None] * jnp.einsum('bhtd,bhdv->bhtv', qc, c)
               + jnp.einsum('bhts,bhsv->bhtv', s, vc))
        den = w_prev * jnp.einsum('bhtd,bhd->bht', qc, n) + jnp.sum(s, axis=-1)
        h = num / jnp.maximum(jnp.abs(den), jnp.exp(-m_t))[..., None]
        m_end = m_t[..., -1]
        g_prev = jnp.exp(b[..., -1] + m - m_end)
        w_in = jnp.exp(b[..., -1:] - b + ic - m_end[..., None])
        c = g_prev[..., None, None] * c + jnp.einsum('bhs,bhsd,bhsv->bhdv', w_in, kc, vc)
        n = g_prev[..., None] * n + jnp.einsum('bhs,bhsd->bhd', w_in, kc)
        return (c, n, m_end), h

    (c, n, m), h = lax.scan(step, (c0, n0, m0), tuple(blocks(a) for a in (q, k, v, ig, logf)))
    h = jnp.moveaxis(h, 0, 2).reshape(bsz, nh, nc * ML_CHUNK, dh)[:, :, :seq]
    return h, c, n, m


def s5_scan(u, x0_re, x0_im, a_re, a_im, log_dt, b_re, b_im, c_re, c_im, d):
    f32 = jnp.float32
    a_re, a_im, b_re, b_im, c_re, c_im, d = (t.astype(f32) for t in (a_re, a_im, b_re, b_im, c_re, c_im, d))
    dt = jnp.exp(log_dt.astype(f32))[:, None]
    lam_re = jnp.minimum(a_re, -1e-4)
    lam_im = a_im
    mag = jnp.exp(lam_re * dt)
    ab_re = mag * jnp.cos(lam_im * dt)
    ab_im = mag * jnp.sin(lam_im * dt)
    inv = 1.0 / (lam_re * lam_re + lam_im * lam_im)
    f_re = ((ab_re - 1.0) * lam_re + ab_im * lam_im) * inv
    f_im = (ab_im * lam_re - (ab_re - 1.0) * lam_im) * inv
    bb_re = f_re[..., None] * b_re - f_im[..., None] * b_im
    bb_im = f_re[..., None] * b_im + f_im[..., None] * b_re
    bu_re = jnp.einsum('blgc,gpc->blgp', u, bb_re)
    bu_im = jnp.einsum('blgc,gpc->blgp', u, bb_im)
    bu_re = bu_re.at[:, 0].add(ab_re * x0_re - ab_im * x0_im)
    bu_im = bu_im.at[:, 0].add(ab_re * x0_im + ab_im * x0_re)
    a_rb = jnp.broadcast_to(ab_re, bu_re.shape)
    a_ib = jnp.broadcast_to(ab_im, bu_im.shape)

    def combine(e1, e2):
        a1r, a1i, b1r, b1i = e1
        a2r, a2i, b2r, b2i = e2
        return (a1r * a2r - a1i * a2i, a1r * a2i + a1i * a2r,
                a2r * b1r - a2i * b1i + b2r, a2r * b1i + a2i * b1r + b2i)

    _, _, xr, xi = lax.associative_scan(combine, (a_rb, a_ib, bu_re, bu_im), axis=1)
    y = (jnp.einsum('blgp,gcp->blgc', xr, c_re) - jnp.einsum('blgp,gcp->blgc', xi, c_im)
         + d * u)
    return y, xr[:, -1], xi[:, -1]


def causal_conv(x, buf, w, bias):
    seq = x.shape[1]
    xx = jnp.concatenate([buf, x], axis=1)
    y = bias
    for j in range(CONV_W):
        y = y + xx[:, j:j + seq] * w[j]
    return y, xx[:, seq:]


def mixer(h, l, st, p):
    f32 = jnp.float32
    s_hg, s_re, s_im, s_c, s_n, s_m, s_conv = (s.astype(f32) for s in st)
    bsz, seq, _ = h.shape
    cols = jnp.einsum('bld,de->ble', h, p['w_in'][l]).astype(f32)
    splits = np.cumsum(IN_WIDTHS)[:-1].tolist()
    hq, hf, hi, hg, su, mx, mo, gz = jnp.split(cols, splits, axis=-1)

    def heads(a, n):
        return a.reshape(bsz, seq, n, -1)

    lbs = jax.nn.softmax(p['hgrn_lower_bounds'].astype(f32), axis=0)
    lb = (jnp.cumsum(lbs, axis=0) - lbs[0])[l]
    forget = lb + (1.0 - lb) * jax.nn.sigmoid(hf)
    logf = jnp.log(forget)
    k_in = (1.0 - lb) * jax.nn.sigmoid(-hf)
    o, s_hg_new = gla_chunked(heads(jax.nn.silu(hq), HG_HEADS), heads(k_in, HG_HEADS),
                              heads(hi, HG_HEADS), heads(logf, HG_HEADS), s_hg)
    o = head_rms(o).reshape(bsz, seq, MIX_W) * p['hgrn_norm'][l] * jax.nn.silu(hg)
    br_a = jnp.einsum('blc,cd->bld', o, p['w_hgrn_out'][l])

    y, x_re, x_im = s5_scan(su.reshape(bsz, seq, S5_GROUPS, S5_CH), s_re, s_im,
                            p['s5_a_re'][l], p['s5_a_im'][l], p['s5_log_dt'][l],
                            p['s5_b_re'][l], p['s5_b_im'][l], p['s5_c_re'][l], p['s5_c_im'][l],
                            p['s5_d'][l])
    y = jax.nn.gelu(y.reshape(bsz, seq, MIX_W))
    br_b = (jnp.einsum('blc,cd->bld', y, p['w_s5_glu_a'][l])
            * jax.nn.sigmoid(jnp.einsum('blc,cd->bld', y, p['w_s5_glu_b'][l])))

    xc, conv_new = causal_conv(mx, s_conv, p['mlstm_conv_w'][l], p['mlstm_conv_b'][l])
    xc = jax.nn.silu(xc)
    q = jnp.einsum('blhd,hde->blhe', heads(xc, ML_HEADS), p['mlstm_wq'][l])
    k = jnp.einsum('blhd,hde->blhe', heads(xc, ML_HEADS), p['mlstm_wk'][l])
    v = jnp.einsum('blhd,hde->blhe', heads(mx, ML_HEADS), p['mlstm_wv'][l])
    gin = jnp.concatenate([q.reshape(bsz, seq, MIX_W), k.reshape(bsz, seq, MIX_W),
                           v.reshape(bsz, seq, MIX_W)], axis=-1)
    gates = (jnp.einsum('blc,cg->blg', gin, p['mlstm_w_gates'][l]) + p['mlstm_b_gates'][l]).astype(f32)
    ig = jnp.moveaxis(gates[..., :ML_HEADS], 2, 1)
    logf_m = jnp.moveaxis(jax.nn.log_sigmoid(gates[..., ML_HEADS:]), 2, 1)
    hm, c_new, n_new, m_new = mlstm_chunked(q.transpose(0, 2, 1, 3),
                                            k.transpose(0, 2, 1, 3) * ML_DH ** -0.5,
                                            v.transpose(0, 2, 1, 3), ig, logf_m, s_c, s_n, s_m)
    hm = (head_rms(hm).transpose(0, 2, 1, 3).reshape(bsz, seq, MIX_W)
          * p['mlstm_norm'][l] * jax.nn.sigmoid(mo))
    br_c = jnp.einsum('blc,cd->bld', hm, p['w_mlstm_out'][l])

    g = jax.nn.sigmoid(gz.reshape(bsz, seq, N_BRANCH, D_MODEL))
    merged = g[:, :, 0] * br_a + g[:, :, 1] * br_b + g[:, :, 2] * br_c
    out = jnp.einsum('bld,de->ble', merged, p['w_out'][l])
    return out, (s_hg_new, x_re, x_im, c_new, n_new, m_new, conv_new)


def layer(x, l, st, p):
    g = p['norm_gains'][l]
    ff1 = swiglu(rmsnorm(x, g[0]), p['w_ffn1_up'][l], p['w_ffn1_down'][l])
    x = x + (0.5 * rmsnorm(ff1, g[1])).astype(x.dtype)
    mix, new_st = mixer(rmsnorm(x, g[2]), l, st, p)
    x = x + rmsnorm(mix, g[3]).astype(x.dtype)
    ff2 = swiglu(rmsnorm(x, g[4]), p['w_ffn2_up'][l], p['w_ffn2_down'][l])
    x = x + (0.5 * rmsnorm(ff2, g[5])).astype(x.dtype)
    return x, new_st


def run_trunk(x, states, p):
    new = [[] for _ in states]
    for l in range(DEPTH):
        x, st = layer(x, l, tuple(s[l] for s in states), p)
        for lst, s in zip(new, st):
            lst.append(s)
    return x, tuple(jnp.stack(lst) for lst in new)


def zero_states(n):
    z = jnp.zeros
    f32 = jnp.float32
    return (z((DEPTH, n, HG_HEADS, HG_DK, HG_DV), f32),
            z((DEPTH, n, S5_GROUPS, S5_STATE), f32),
            z((DEPTH, n, S5_GROUPS, S5_STATE), f32),
            z((DEPTH, n, ML_HEADS, ML_DH, ML_DH), f32),
            z((DEPTH, n, ML_HEADS, ML_DH), f32),
            z((DEPTH, n, ML_HEADS), f32),
            z((DEPTH, n, CONV_W - 1, MIX_W), f32))


def setup_inputs(seed: int = 0) -> dict:
    key = jax.random.key(seed)
    ks = iter(jax.random.split(key, 64))
    f32 = jnp.float32

    def nrm(shape, scale):
        return scale * jax.random.normal(next(ks), shape, f32)

    L = DEPTH
    return {
        'x_prompt': nrm((BATCH, SEQ, D_MODEL), 1.0),
        'x_sample': nrm((DEC_BATCH, DEC_SEQ, D_MODEL), 1.0),
        'state_hgrn': nrm((L, DEC_BATCH, HG_HEADS, HG_DK, HG_DV), 0.5),
        'state_s5_re': nrm((L, DEC_BATCH, S5_GROUPS, S5_STATE), 0.1),
        'state_s5_im': nrm((L, DEC_BATCH, S5_GROUPS, S5_STATE), 0.1),
        'state_mlstm_c': nrm((L, DEC_BATCH, ML_HEADS, ML_DH, ML_DH), 0.05),
        'state_mlstm_n': nrm((L, DEC_BATCH, ML_HEADS, ML_DH), 0.1),
        'state_mlstm_m': nrm((L, DEC_BATCH, ML_HEADS), 1.0),
        'state_mlstm_conv': nrm((L, DEC_BATCH, CONV_W - 1, MIX_W), 1.0),
        'norm_gains': 1.0 + nrm((L, 6, D_MODEL), 0.01),
        'w_ffn1_up': nrm((L, D_MODEL, 2 * D_FF), D_MODEL ** -0.5),
        'w_ffn1_down': nrm((L, D_FF, D_MODEL), D_FF ** -0.5),
        'w_in': nrm((L, D_MODEL, N_IN), D_MODEL ** -0.5),
        'hgrn_lower_bounds': nrm((L, MIX_W), 0.1),
        'hgrn_norm': 1.0 + nrm((L, MIX_W), 0.01),
        'w_hgrn_out': nrm((L, MIX_W, D_MODEL), MIX_W ** -0.5),
        's5_a_re': -0.5 + nrm((L, S5_GROUPS, S5_STATE), 0.01),
        's5_a_im': jnp.broadcast_to(math.pi * jnp.arange(S5_STATE, dtype=f32), (L, S5_GROUPS, S5_STATE))
                   + nrm((L, S5_GROUPS, S5_STATE), 0.01),
        's5_log_dt': jax.random.uniform(next(ks), (L, S5_GROUPS), f32,
                                        math.log(S5_DT_MIN), math.log(S5_DT_MAX)),
        's5_b_re': nrm((L, S5_GROUPS, S5_STATE, S5_CH), (2 * S5_CH) ** -0.5),
        's5_b_im': nrm((L, S5_GROUPS, S5_STATE, S5_CH), (2 * S5_CH) ** -0.5),
        's5_c_re': nrm((L, S5_GROUPS, S5_CH, S5_STATE), (2 * S5_STATE) ** -0.5),
        's5_c_im': nrm((L, S5_GROUPS, S5_CH, S5_STATE), (2 * S5_STATE) ** -0.5),
        's5_d': nrm((L, S5_GROUPS, S5_CH), 1.0),
        'w_s5_glu_a': nrm((L, MIX_W, D_MODEL), MIX_W ** -0.5),
        'w_s5_glu_b': nrm((L, MIX_W, D_MODEL), MIX_W ** -0.5),
        'mlstm_conv_w': nrm((L, CONV_W, MIX_W), CONV_W ** -0.5),
        'mlstm_conv_b': nrm((L, MIX_W), 0.01),
        'mlstm_wq': nrm((L, ML_HEADS, ML_DH, ML_DH), ML_DH ** -0.5),
        'mlstm_wk': nrm((L, ML_HEADS, ML_DH, ML_DH), ML_DH ** -0.5),
        'mlstm_wv': nrm((L, ML_HEADS, ML_DH, ML_DH), ML_DH ** -0.5),
        'mlstm_w_gates': nrm((L, 3 * MIX_W, 2 * ML_HEADS), 0.1 * (3 * MIX_W) ** -0.5),
        'mlstm_b_gates': jnp.concatenate(
            [nrm((L, ML_HEADS), 0.1),
             jnp.linspace(3.0, 6.0, ML_HEADS, dtype=f32)[None] + nrm((L, ML_HEADS), 0.01)], axis=-1),
        'mlstm_norm': 1.0 + nrm((L, MIX_W), 0.01),
        'w_mlstm_out': nrm((L, MIX_W, D_MODEL), MIX_W ** -0.5),
        'w_out': nrm((L, D_MODEL, D_MODEL), D_MODEL ** -0.5),
        'w_ffn2_up': nrm((L, D_MODEL, 2 * D_FF), D_MODEL ** -0.5),
        'w_ffn2_down': nrm((L, D_FF, D_MODEL), D_FF ** -0.5),
    }


def reference(x_prompt, x_sample, state_hgrn, state_s5_re, state_s5_im, state_mlstm_c,
              state_mlstm_n, state_mlstm_m, state_mlstm_conv, norm_gains, w_ffn1_up, w_ffn1_down,
              w_in, hgrn_lower_bounds, hgrn_norm, w_hgrn_out, s5_a_re, s5_a_im, s5_log_dt,
              s5_b_re, s5_b_im, s5_c_re, s5_c_im, s5_d, w_s5_glu_a, w_s5_glu_b, mlstm_conv_w,
              mlstm_conv_b, mlstm_wq, mlstm_wk, mlstm_wv, mlstm_w_gates, mlstm_b_gates, mlstm_norm,
              w_mlstm_out, w_out, w_ffn2_up, w_ffn2_down):
    p = dict(norm_gains=norm_gains, w_ffn1_up=w_ffn1_up, w_ffn1_down=w_ffn1_down, w_in=w_in,
             hgrn_lower_bounds=hgrn_lower_bounds, hgrn_norm=hgrn_norm, w_hgrn_out=w_hgrn_out,
             s5_a_re=s5_a_re, s5_a_im=s5_a_im, s5_log_dt=s5_log_dt, s5_b_re=s5_b_re,
             s5_b_im=s5_b_im, s5_c_re=s5_c_re, s5_c_im=s5_c_im, s5_d=s5_d,
             w_s5_glu_a=w_s5_glu_a, w_s5_glu_b=w_s5_glu_b, mlstm_conv_w=mlstm_conv_w,
             mlstm_conv_b=mlstm_conv_b, mlstm_wq=mlstm_wq, mlstm_wk=mlstm_wk, mlstm_wv=mlstm_wv,
             mlstm_w_gates=mlstm_w_gates, mlstm_b_gates=mlstm_b_gates, mlstm_norm=mlstm_norm,
             w_mlstm_out=w_mlstm_out, w_out=w_out, w_ffn2_up=w_ffn2_up, w_ffn2_down=w_ffn2_down)
    y_prompt, (p_hg, p_re, p_im, p_c, p_n, p_m, p_conv) = run_trunk(x_prompt, zero_states(BATCH), p)
    sample_states = (state_hgrn, state_s5_re, state_s5_im, state_mlstm_c, state_mlstm_n,
                     state_mlstm_m, state_mlstm_conv)
    y_sample, (s_hg, s_re, s_im, s_c, s_n, s_m, s_conv) = run_trunk(x_sample, sample_states, p)
    return (y_prompt, y_sample, p_hg, p_re, p_im, p_c, p_n, p_m, p_conv,
            s_hg, s_re, s_im, s_c, s_n, s_m, s_conv)
```

```python
import functools
import math

import jax
import jax.numpy as jnp
from jax import lax
from jax.experimental import pallas as pl
from jax.experimental.pallas import tpu as pltpu

F32 = jnp.float32
BF16 = jnp.bfloat16

EPS = 1e-6
NEG_BIG = -1e30
S5_DT_MIN_CLAMP = -1e-4

HG_DK = 128
S5_CH = 16
S5_SUPER = 8
ML_HEADS = 4
CONV_W = 4
N_BRANCH = 3

LANES = 128
SUBLANES = 8
VMEM_LIMIT = 56 * 1024 * 1024


def _cparams(sem):
    return pltpu.CompilerParams(dimension_semantics=sem, vmem_limit_bytes=VMEM_LIMIT)


def _rms_scale(y):
    return lax.rsqrt(jnp.mean(y * y, axis=-1, keepdims=True) + EPS)


def _silu(x):
    return x * jax.nn.sigmoid(x)


def _ffn_body(x_ref, gpre_ref, gpost_ref, wa_ref, wb_ref, wd_ref, o_ref, h_ref, acc_ref):
    j = pl.program_id(1)

    @pl.when(j == 0)
    def _():
        x = x_ref[...]
        h_ref[...] = (x * _rms_scale(x) * gpre_ref[...]).astype(BF16)
        acc_ref[...] = jnp.zeros_like(acc_ref)

    h = h_ref[...]
    a = jnp.dot(h, wa_ref[...], preferred_element_type=F32)
    b = jnp.dot(h, wb_ref[...], preferred_element_type=F32)
    act = (_silu(a) * b).astype(BF16)
    acc_ref[...] += jnp.dot(act, wd_ref[...], preferred_element_type=F32)

    @pl.when(j == pl.num_programs(1) - 1)
    def _():
        y = acc_ref[...]
        o_ref[...] = x_ref[...] + 0.5 * (y * _rms_scale(y) * gpost_ref[...])


def _ffn(x, g_pre, g_post, w_up, w_down, *, tm=512, tf=512):
    m, d = x.shape
    d_ff = w_down.shape[0]
    nf = d_ff // tf
    return pl.pallas_call(
        _ffn_body,
        grid=(m // tm, nf),
        in_specs=[
            pl.BlockSpec((tm, d), lambda i, j: (i, 0)),
            pl.BlockSpec((1, d), lambda i, j: (0, 0)),
            pl.BlockSpec((1, d), lambda i, j: (0, 0)),
            pl.BlockSpec((d, tf), lambda i, j: (0, j)),
            pl.BlockSpec((d, tf), lambda i, j: (0, j + nf)),
            pl.BlockSpec((tf, d), lambda i, j: (j, 0)),
        ],
        out_specs=pl.BlockSpec((tm, d), lambda i, j: (i, 0)),
        out_shape=jax.ShapeDtypeStruct((m, d), F32),
        scratch_shapes=[pltpu.VMEM((tm, d), BF16), pltpu.VMEM((tm, d), F32)],
        compiler_params=_cparams(("parallel", "arbitrary")),
        name="ffn",
    )(x, g_pre.reshape(1, d), g_post.reshape(1, d), w_up, w_up, w_down)


def _inproj_body(x_ref, g_ref, w_ref, o_ref, h_ref):
    @pl.when(pl.program_id(1) == 0)
    def _():
        x = x_ref[...]
        h_ref[...] = (x * _rms_scale(x) * g_ref[...]).astype(BF16)

    o_ref[...] = jnp.dot(h_ref[...], w_ref[...], preferred_element_type=F32)


def _inproj(x, g, w, *, tm=512, tn=1024):
    m, d = x.shape
    n = w.shape[1]
    return pl.pallas_call(
        _inproj_body,
        grid=(m // tm, n // tn),
        in_specs=[
            pl.BlockSpec((tm, d), lambda i, j: (i, 0)),
            pl.BlockSpec((1, d), lambda i, j: (0, 0)),
            pl.BlockSpec((d, tn), lambda i, j: (0, j)),
        ],
        out_specs=pl.BlockSpec((tm, tn), lambda i, j: (i, j)),
        out_shape=jax.ShapeDtypeStruct((m, n), F32),
        scratch_shapes=[pltpu.VMEM((tm, d), BF16)],
        compiler_params=_cparams(("parallel", "arbitrary")),
        name="inproj",
    )(x, g.reshape(1, d), w)


def _prep_body(lbraw_ref, are_ref, aim_ref, ldt_ref, bre_ref, bim_ref,
               lb_ref, abr_ref, abi_ref, bbr_ref, bbi_ref):
    depth = lbraw_ref.shape[0]
    raw = lbraw_ref[...]
    e = jnp.exp(raw - jnp.max(raw, axis=0, keepdims=True))
    lbs = e / jnp.sum(e, axis=0, keepdims=True)
    run = jnp.zeros_like(lbs[0:1])
    for l in range(depth):
        run = run + lbs[l:l + 1]
        lb_ref[l:l + 1, :] = run - lbs[0:1]

    for l in range(depth):
        dt = jnp.exp(ldt_ref[l])
        lam_re = jnp.minimum(are_ref[l], S5_DT_MIN_CLAMP)
        lam_im = aim_ref[l]
        mag = jnp.exp(lam_re * dt)
        ab_re = mag * jnp.cos(lam_im * dt)
        ab_im = mag * jnp.sin(lam_im * dt)
        inv = 1.0 / (lam_re * lam_re + lam_im * lam_im)
        f_re = ((ab_re - 1.0) * lam_re + ab_im * lam_im) * inv
        f_im = (ab_im * lam_re - (ab_re - 1.0) * lam_im) * inv
        abr_ref[l] = ab_re
        abi_ref[l] = ab_im
        b_re = bre_ref[l]
        b_im = bim_ref[l]
        bbr_ref[l] = f_re * b_re - f_im * b_im
        bbi_ref[l] = f_re * b_im + f_im * b_re


def _prep(lb_raw, a_re, a_im, log_dt, b_re, b_im):
    depth, groups, state = a_re.shape
    gp = groups * state
    ch = b_re.shape[-1]
    row = lambda a: a.reshape(depth, 1, gp)
    ldt = jnp.broadcast_to(log_dt[:, :, None], (depth, groups, state))
    chan_major = lambda b: b.transpose(0, 3, 1, 2).reshape(depth, ch, gp)
    out_shape = (
        jax.ShapeDtypeStruct(lb_raw.shape, F32),
        jax.ShapeDtypeStruct((depth, 1, gp), F32),
        jax.ShapeDtypeStruct((depth, 1, gp), F32),
        jax.ShapeDtypeStruct((depth, ch, gp), F32),
        jax.ShapeDtypeStruct((depth, ch, gp), F32),
    )
    return pl.pallas_call(_prep_body, out_shape=out_shape, name="prep")(
        lb_raw, row(a_re), row(a_im), row(ldt), chan_major(b_re), chan_major(b_im))


def _hgrn_body(q_ref, f_ref, i_ref, g_ref, lb_ref, nrm_ref, s0_ref, o_ref, s_ref, st_ref,
               *, nb, tl, c):
    t = pl.program_id(2)

    @pl.when(t == 0)
    def _():
        for n in range(nb):
            st_ref[n] = s0_ref[n, 0].T

    lb = lb_ref[...]
    nrm = nrm_ref[...]
    row = lax.broadcasted_iota(jnp.int32, (c, HG_DK), 0)
    rowc = lax.broadcasted_iota(jnp.int32, (c, 1), 0)
    chunks_per_seq = tl // c

    def chunk(idx, carry):
        n = idx // chunks_per_seq
        r0 = pl.multiple_of(idx * c, c)
        rows = pl.ds(r0, c)
        fpre = f_ref[rows, :]
        q = _silu(q_ref[rows, :])
        v = i_ref[rows, :]
        g = g_ref[rows, :]
        forget = lb + (1.0 - lb) * jax.nn.sigmoid(fpre)
        kin = (1.0 - lb) * jax.nn.sigmoid(-fpre)

        cp = forget
        sh = 1
        while sh < c:
            cp = cp * jnp.where(row >= sh, pltpu.roll(cp, sh, 0), 1.0)
            sh *= 2
        sp = jnp.where(row < c - 1, pltpu.roll(forget, c - 1, 0), 1.0)
        sh = 1
        while sh < c:
            sp = sp * jnp.where(row < c - sh, pltpu.roll(sp, c - sh, 0), 1.0)
            sh *= 2

        st = st_ref[n]
        out = lax.dot_general((q * cp).astype(BF16), st.astype(BF16),
                              (((1,), (1,)), ((), ())), preferred_element_type=F32)
        decay = jnp.ones_like(forget)
        for d in range(c):
            if d > 0:
                decay = decay * (forget if d == 1 else pltpu.roll(forget, d - 1, 0))
            kd = kin if d == 0 else pltpu.roll(kin, d, 0)
            vd = v if d == 0 else pltpu.roll(v, d, 0)
            r = jnp.sum(q * kd * decay, axis=-1, keepdims=True)
            r = jnp.where(rowc >= d, r, 0.0)
            out = out + r * vd

        upd = lax.dot_general(v.astype(BF16), (kin * sp).astype(BF16),
                              (((0,), (0,)), ((), ())), preferred_element_type=F32)
        st_ref[n] = st * cp[c - 1:c, :] + upd
        o_ref[rows, :] = out * _rms_scale(out) * nrm * _silu(g)
        return carry

    lax.fori_loop(0, nb * chunks_per_seq, chunk, 0)

    @pl.when(t == pl.num_programs(2) - 1)
    def _():
        for n in range(nb):
            s_ref[n, 0] = st_ref[n].T


def _hgrn(cols, lb, nrm, s0, *, row0, seq, nb, tl, c):
    nseq, heads, dk, dv = s0.shape
    nt = seq // tl
    blk = nb * tl
    rb0 = row0 // blk
    mix_w = heads * dk
    hb = mix_w // dk

    def colspec(group):
        return pl.BlockSpec((blk, dk), lambda b, h, t: (rb0 + b * nt + t, group * hb + h))

    vec = pl.BlockSpec((1, dk), lambda b, h, t: (0, h))
    sspec = pl.BlockSpec((nb, 1, dk, dv), lambda b, h, t: (b, h, 0, 0))
    return pl.pallas_call(
        functools.partial(_hgrn_body, nb=nb, tl=tl, c=c),
        grid=(nseq // nb, heads, nt),
        in_specs=[colspec(0), colspec(1), colspec(2), colspec(3), vec, vec, sspec],
        out_specs=(pl.BlockSpec((blk, dv), lambda b, h, t: (b * nt + t, h)), sspec),
        out_shape=(jax.ShapeDtypeStruct((nseq * seq, mix_w), F32),
                   jax.ShapeDtypeStruct(s0.shape, F32)),
        scratch_shapes=[pltpu.VMEM((nb, dv, dk), F32)],
        compiler_params=_cparams(("parallel", "parallel", "arbitrary")),
        name="hgrn",
    )(cols, cols, cols, cols, lb.reshape(1, mix_w), nrm.reshape(1, mix_w), s0)


def _gelu_tanh(x):
    return 0.5 * x * (1.0 + jnp.tanh(math.sqrt(2.0 / math.pi) * (x + 0.044715 * (x * x * x))))


def _s5_body(u_ref, wb_ref, wc_ref, abr_ref, abi_ref, d_ref, x0r_ref, x0i_ref,
             y_ref, xr_ref, xi_ref, sr_ref, si_ref, *, nb, tl):
    rows = nb * tl
    pitch = rows + SUBLANES
    nsb = wb_ref.shape[0]
    cw = wb_ref.shape[1]
    half = wb_ref.shape[2] // 2
    per = half // LANES
    slabs = nsb * per
    t = pl.program_id(1)

    @pl.when(t == 0)
    def _():
        xr_ref[...] = x0r_ref[...]
        xi_ref[...] = x0i_ref[...]

    for sb in range(nsb):
        ub = u_ref[:, sb * cw:(sb + 1) * cw].astype(BF16)
        res = jnp.dot(ub, wb_ref[sb], preferred_element_type=F32)
        for j in range(per):
            base = (sb * per + j) * pitch
            sr_ref[pl.ds(base, rows), :] = res[:, j * LANES:(j + 1) * LANES]
            si_ref[pl.ds(base, rows), :] = res[:, half + j * LANES:half + (j + 1) * LANES]

    ar = abr_ref[...]
    ai = abi_ref[...]

    def seq_scan(n, carry):
        def step(tt, x):
            xr, xi = x
            at = pl.ds(n * tl + tt, slabs, stride=pitch)
            nxr = ar * xr - ai * xi + sr_ref[at, :]
            nxi = ar * xi + ai * xr + si_ref[at, :]
            sr_ref[at, :] = nxr
            si_ref[at, :] = nxi
            return nxr, nxi

        xr, xi = lax.fori_loop(0, tl, step, (xr_ref[n], xi_ref[n]))
        xr_ref[n] = xr
        xi_ref[n] = xi
        return carry

    lax.fori_loop(0, nb, seq_scan, 0)

    for sb in range(nsb):
        parts = [sr_ref[pl.ds((sb * per + j) * pitch, rows), :] for j in range(per)]
        parts += [si_ref[pl.ds((sb * per + j) * pitch, rows), :] for j in range(per)]
        xs = jnp.concatenate(parts, axis=1).astype(BF16)
        cs = slice(sb * cw, (sb + 1) * cw)
        y = jnp.dot(xs, wc_ref[sb], preferred_element_type=F32) + d_ref[:, cs] * u_ref[:, cs]
        y_ref[:, cs] = _gelu_tanh(y)


def _s5(cols, wb, wc, ab_re, ab_im, d, x0_re, x0_im, *, row0, col0, seq, nb, tl):
    nseq, slabs, _ = x0_re.shape
    nt = seq // tl
    blk = nb * tl
    rb0 = row0 // blk
    mix_w = d.shape[-1]
    cb = col0 // mix_w
    const3 = lambda a: pl.BlockSpec(a.shape, lambda b, t: (0, 0, 0))
    const2 = lambda a: pl.BlockSpec(a.shape, lambda b, t: (0, 0))
    xspec = pl.BlockSpec((nb, slabs, LANES), lambda b, t: (b, 0, 0))
    pitch = blk + SUBLANES
    return pl.pallas_call(
        functools.partial(_s5_body, nb=nb, tl=tl),
        grid=(nseq // nb, nt),
        in_specs=[pl.BlockSpec((blk, mix_w), lambda b, t: (rb0 + b * nt + t, cb)),
                  const3(wb), const3(wc), const2(ab_re), const2(ab_im), const2(d), xspec, xspec],
        out_specs=(pl.BlockSpec((blk, mix_w), lambda b, t: (b * nt + t, 0)), xspec, xspec),
        out_shape=(jax.ShapeDtypeStruct((nseq * seq, mix_w), F32),
                   jax.ShapeDtypeStruct(x0_re.shape, F32),
                   jax.ShapeDtypeStruct(x0_im.shape, F32)),
        scratch_shapes=[pltpu.VMEM((slabs * pitch, LANES), F32),
                        pltpu.VMEM((slabs * pitch, LANES), F32)],
        compiler_params=_cparams(("parallel", "arbitrary")),
        name="s5",
    )(cols, wb, wc, ab_re, ab_im, d, x0_re, x0_im)


def _s5_weights(bb_re, bb_im, c_re, c_im):
    ch, gp = bb_re.shape
    groups = c_re.shape[0]
    state = gp // groups
    nsb = groups // S5_SUPER
    eye = jnp.eye(S5_SUPER, dtype=F32)

    def in_w(bb):
        b4 = bb.reshape(ch, nsb, S5_SUPER, state)
        return jnp.einsum('csgp,gh->sgchp', b4, eye).reshape(nsb, S5_SUPER * ch, S5_SUPER * state)

    def out_w(c):
        c4 = c.reshape(nsb, S5_SUPER, ch, state)
        return jnp.einsum('sgcp,gh->sgphc', c4, eye).reshape(nsb, S5_SUPER * state, S5_SUPER * ch)

    wb = jnp.concatenate([in_w(bb_re), in_w(bb_im)], axis=2).astype(BF16)
    wc = jnp.concatenate([out_w(c_re), -out_w(c_im)], axis=1).astype(BF16)
    return wb, wc


def _mlproj_body(x_ref, c0_ref, cw_ref, cb_ref, wq_ref, wk_ref, wv_ref, wg_ref, bg_ref,
                 q_ref, k_ref, v_ref, g_ref, cn_ref, xx_ref, *, nb, tl):
    t = pl.program_id(1)
    width = x_ref.shape[1]
    halo = CONV_W - 1
    lo = SUBLANES - halo

    @pl.when(t == 0)
    def _():
        xx_ref[:, lo:SUBLANES, :] = c0_ref[...]

    @pl.when(t > 0)
    def _():
        xx_ref[:, lo:SUBLANES, :] = xx_ref[:, tl + lo:tl + SUBLANES, :]

    x = x_ref[...]
    xx_ref[:, SUBLANES:, :] = x.reshape(nb, tl, width)
    xc = jnp.zeros((nb, tl, width), F32) + cb_ref[...][None]
    for j in range(CONV_W):
        xc = xc + xx_ref[:, lo + j:lo + j + tl, :] * cw_ref[j:j + 1, :][None]
    cn_ref[...] = xx_ref[:, tl + lo:tl + SUBLANES, :]
    xc = _silu(xc).reshape(nb * tl, width).astype(BF16)
    xb = x.astype(BF16)

    dh = width // ML_HEADS
    gates = jnp.zeros((nb * tl, 2 * ML_HEADS), F32) + bg_ref[...]
    for h in range(ML_HEADS):
        hs = slice(h * dh, (h + 1) * dh)
        q = jnp.dot(xc[:, hs], wq_ref[h], preferred_element_type=F32)
        k = jnp.dot(xc[:, hs], wk_ref[h], preferred_element_type=F32)
        v = jnp.dot(xb[:, hs], wv_ref[h], preferred_element_type=F32)
        q_ref[:, hs] = q
        k_ref[:, hs] = k
        v_ref[:, hs] = v
        for part, val in enumerate((q, k, v)):
            w = wg_ref[part * width + h * dh:part * width + (h + 1) * dh, :]
            gates = gates + jnp.dot(val.astype(BF16), w, preferred_element_type=F32)
    g_ref[...] = gates


def _mlproj(cols, conv0, conv_w, conv_b, wq, wk, wv, wg, bg, *, row0, col0, seq, nb, tl):
    nseq, halo, width = conv0.shape
    nt = seq // tl
    blk = nb * tl
    rb0 = row0 // blk
    cb = col0 // width
    ng = wg.shape[1]
    full = lambda a: pl.BlockSpec(a.shape, lambda b, t: (0,) * a.ndim)
    rowspec = lambda w: pl.BlockSpec((blk, w), lambda b, t: (b * nt + t, 0))
    cspec = pl.BlockSpec((nb, halo, width), lambda b, t: (b, 0, 0))
    conv_b = conv_b.reshape(1, width)
    bg = bg.reshape(1, ng)
    return pl.pallas_call(
        functools.partial(_mlproj_body, nb=nb, tl=tl),
        grid=(nseq // nb, nt),
        in_specs=[pl.BlockSpec((blk, width), lambda b, t: (rb0 + b * nt + t, cb)), cspec,
                  full(conv_w), full(conv_b), full(wq), full(wk), full(wv), full(wg), full(bg)],
        out_specs=(rowspec(width), rowspec(width), rowspec(width), rowspec(ng), cspec),
        out_shape=(jax.ShapeDtypeStruct((nseq * seq, width), F32),) * 3
                  + (jax.ShapeDtypeStruct((nseq * seq, ng), F32),
                     jax.ShapeDtypeStruct(conv0.shape, F32)),
        scratch_shapes=[pltpu.VMEM((nb, tl + SUBLANES, width), F32)],
        compiler_params=_cparams(("parallel", "arbitrary")),
        name="mlproj",
    )(cols, conv0, conv_w, conv_b, wq, wk, wv, wg, bg)


def _mlrec_body(q_ref, k_ref, v_ref, g_ref, o_ref, nrm_ref, c0_ref, n0_ref, m0_ref,
                h_ref, c_ref, n_ref, m_ref, *, nb, tl, c):
    head = pl.program_id(1)
    t = pl.program_id(2)
    dh = q_ref.shape[1]

    @pl.when(t == 0)
    def _():
        c_ref[...] = c0_ref[...]
        n_ref[...] = n0_ref[...]
        m_ref[...] = m0_ref[...]

    nrm = nrm_ref[...]
    glane = lax.broadcasted_iota(jnp.int32, (c, 2 * ML_HEADS), 1)
    r = lax.broadcasted_iota(jnp.int32, (c, c), 0)
    s = lax.broadcasted_iota(jnp.int32, (c, c), 1)
    causal = r >= s
    chunks_per_seq = tl // c
    k_scale = dh ** -0.5

    def chunk(idx, carry):
        n = idx // chunks_per_seq
        rows = pl.ds(pl.multiple_of(idx * c, c), c)
        q = q_ref[rows, :]
        k = k_ref[rows, :] * k_scale
        v = v_ref[rows, :]
        gt = g_ref[rows, :]
        ig_col = jnp.sum(jnp.where(glane == head, gt, 0.0), axis=1, keepdims=True)
        fp_col = jnp.sum(jnp.where(glane == head + ML_HEADS, gt, 0.0), axis=1, keepdims=True)
        lf_col = jnp.minimum(fp_col, 0.0) - jnp.log1p(jnp.exp(-jnp.abs(fp_col)))
        lf_row = jnp.sum(jnp.where(r == s, lf_col, 0.0), axis=0, keepdims=True)
        ig_row = jnp.sum(jnp.where(r == s, ig_col, 0.0), axis=0, keepdims=True)
        b_col = jnp.sum(jnp.where(causal, lf_row, 0.0), axis=1, keepdims=True)
        b_row = jnp.sum(jnp.where(r <= s, lf_col, 0.0), axis=0, keepdims=True)

        m_prev = m_ref[n, 0][:, 0:1]
        n_prev = n_ref[n, 0]
        c_prev = c_ref[n, 0]

        logw = jnp.where(causal, b_col - b_row + ig_row, NEG_BIG)
        m_t = jnp.maximum(b_col + m_prev, jnp.max(logw, axis=1, keepdims=True))
        w_prev = jnp.exp(b_col + m_prev - m_t)
        w = jnp.exp(logw - m_t)
        qb = q.astype(BF16)
        sc = lax.dot_general(qb, k.astype(BF16), (((1,), (1,)), ((), ())),
                             preferred_element_type=F32) * w
        num = (w_prev * jnp.dot(qb, c_prev.astype(BF16), preferred_element_type=F32)
               + jnp.dot(sc.astype(BF16), v.astype(BF16), preferred_element_type=F32))
        den = (w_prev * jnp.sum(q * n_prev, axis=1, keepdims=True)
               + jnp.sum(sc, axis=1, keepdims=True))
        hval = num / jnp.maximum(jnp.abs(den), jnp.exp(-m_t))

        m_end = m_t[c - 1:c, :]
        b_end = b_col[c - 1:c, :]
        g_prev = jnp.exp(b_end + m_prev - m_end)
        w_in = jnp.exp(b_end - b_col + ig_col - m_end)
        kw = k * w_in
        c_ref[n, 0] = g_prev * c_prev + lax.dot_general(
            kw.astype(BF16), v.astype(BF16), (((0,), (0,)), ((), ())), preferred_element_type=F32)
        n_ref[n, 0] = g_prev * n_prev + jnp.sum(kw, axis=0, keepdims=True)
        m_ref[n, 0] = jnp.broadcast_to(m_end, (1, LANES))
        h_ref[rows, :] = hval * _rms_scale(hval) * nrm * jax.nn.sigmoid(o_ref[rows, :])
        return carry

    lax.fori_loop(0, nb * chunks_per_seq, chunk, 0)


def _mlrec(q, k, v, gates, cols, nrm, c0, n0, m0, *, row0, col0, nb, tl, c):
    nseq, heads, dh, _ = c0.shape
    seq = q.shape[0] // nseq
    nt = seq // tl
    blk = nb * tl
    rb0 = row0 // blk
    cb = col0 // dh
    ng = gates.shape[1]
    width = heads * dh
    qspec = pl.BlockSpec((blk, dh), lambda b, h, t: (b * nt + t, h))
    cspec = pl.BlockSpec((nb, 1, dh, dh), lambda b, h, t: (b, h, 0, 0))
    nspec = pl.BlockSpec((nb, 1, 1, dh), lambda b, h, t: (b, h, 0, 0))
    mspec = pl.BlockSpec((nb, 1, 1, LANES), lambda b, h, t: (b, h, 0, 0))
    n0 = n0.reshape(nseq, heads, 1, dh)
    m0 = jnp.broadcast_to(m0[:, :, None, None], (nseq, heads, 1, LANES))
    hm, c_new, n_new, m_new = pl.pallas_call(
        functools.partial(_mlrec_body, nb=nb, tl=tl, c=c),
        grid=(nseq // nb, heads, nt),
        in_specs=[qspec, qspec, qspec,
                  pl.BlockSpec((blk, ng), lambda b, h, t: (b * nt + t, 0)),
                  pl.BlockSpec((blk, dh), lambda b, h, t: (rb0 + b * nt + t, cb + h)),
                  pl.BlockSpec((1, dh), lambda b, h, t: (0, h)),
                  cspec, nspec, mspec],
        out_specs=(qspec, cspec, nspec, mspec),
        out_shape=(jax.ShapeDtypeStruct((nseq * seq, width), F32),
                   jax.ShapeDtypeStruct(c0.shape, F32),
                   jax.ShapeDtypeStruct(n0.shape, F32),
                   jax.ShapeDtypeStruct(m0.shape, F32)),
        compiler_params=_cparams(("parallel", "parallel", "arbitrary")),
        name="mlrec",
    )(q, k, v, gates, cols, nrm.reshape(1, width), c0, n0, m0)
    return hm, c_new, n_new.reshape(nseq, heads, dh), m_new[:, :, 0, 0]


def _merge_body(x_ref, oa_ref, ob_ref, oc_ref, ga_ref, gb_ref, gc_ref, wa_ref, wba_ref, wbb_ref,
                wc_ref, wo_ref, g_ref, o_ref, a_scr, b_scr, c_scr, acc_ref):
    j = pl.program_id(1)

    @pl.when(j == 0)
    def _():
        a_scr[...] = oa_ref[...].astype(BF16)
        b_scr[...] = ob_ref[...].astype(BF16)
        c_scr[...] = oc_ref[...].astype(BF16)
        acc_ref[...] = jnp.zeros_like(acc_ref)

    yb = b_scr[...]
    br_a = jnp.dot(a_scr[...], wa_ref[...], preferred_element_type=F32)
    br_b = (jnp.dot(yb, wba_ref[...], preferred_element_type=F32)
            * jax.nn.sigmoid(jnp.dot(yb, wbb_ref[...], preferred_element_type=F32)))
    br_c = jnp.dot(c_scr[...], wc_ref[...], preferred_element_type=F32)
    merged = (jax.nn.sigmoid(ga_ref[...]) * br_a + jax.nn.sigmoid(gb_ref[...]) * br_b
              + jax.nn.sigmoid(gc_ref[...]) * br_c)
    acc_ref[...] += jnp.dot(merged.astype(BF16), wo_ref[...], preferred_element_type=F32)

    @pl.when(j == pl.num_programs(1) - 1)
    def _():
        y = acc_ref[...]
        o_ref[...] = x_ref[...] + y * _rms_scale(y) * g_ref[...]


def _merge(x, oa, ob, oc, cols, w_a, w_ba, w_bb, w_c, w_out, g, *, col0, tm=256, tn=512):
    m, d = x.shape
    mix_w = oa.shape[1]
    nj = d // tn
    gb0 = col0 // tn
    act = pl.BlockSpec((tm, mix_w), lambda i, j: (i, 0))
    wspec = pl.BlockSpec((mix_w, tn), lambda i, j: (0, j))

    def gate(branch):
        return pl.BlockSpec((tm, tn), lambda i, j: (i, gb0 + branch * nj + j))

    return pl.pallas_call(
        _merge_body,
        grid=(m // tm, nj),
        in_specs=[pl.BlockSpec((tm, d), lambda i, j: (i, 0)), act, act, act,
                  gate(0), gate(1), gate(2), wspec, wspec, wspec, wspec,
                  pl.BlockSpec((tn, d), lambda i, j: (j, 0)),
                  pl.BlockSpec((1, d), lambda i, j: (0, 0))],
        out_specs=pl.BlockSpec((tm, d), lambda i, j: (i, 0)),
        out_shape=jax.ShapeDtypeStruct((m, d), F32),
        scratch_shapes=[pltpu.VMEM((tm, mix_w), BF16)] * 3 + [pltpu.VMEM((tm, d), F32)],
        compiler_params=_cparams(("parallel", "arbitrary")),
        name="merge",
    )(x, oa, ob, oc, cols, cols, cols, w_a, w_ba, w_bb, w_c, w_out, g.reshape(1, d))


def kernel(x_prompt, x_sample, state_hgrn, state_s5_re, state_s5_im, state_mlstm_c, state_mlstm_n, state_mlstm_m, state_mlstm_conv, norm_gains, w_ffn1_up, w_ffn1_down, w_in, hgrn_lower_bounds, hgrn_norm, w_hgrn_out, s5_a_re, s5_a_im, s5_log_dt, s5_b_re, s5_b_im, s5_c_re, s5_c_im, s5_d, w_s5_glu_a, w_s5_glu_b, mlstm_conv_w, mlstm_conv_b, mlstm_wq, mlstm_wk, mlstm_wv, mlstm_w_gates, mlstm_b_gates, mlstm_norm, w_mlstm_out, w_out, w_ffn2_up, w_ffn2_down):
    batch, seq, d_model = x_prompt.shape
    dec_batch, dec_seq, _ = x_sample.shape
    depth = w_in.shape[0]
    mix_w = w_hgrn_out.shape[1]
    n_prompt = batch * seq
    heads_hg = state_hgrn.shape[2]
    groups, state = s5_a_re.shape[1:]
    slabs = groups * state // LANES

    x = jnp.concatenate([x_prompt.reshape(n_prompt, d_model),
                         x_sample.reshape(dec_batch * dec_seq, d_model)], axis=0)

    lb_all, ab_re, ab_im, bb_re, bb_im = _prep(hgrn_lower_bounds, s5_a_re, s5_a_im, s5_log_dt,
                                               s5_b_re, s5_b_im)

    zeros = lambda *shape: jnp.zeros(shape, F32)
    col_su, col_mx, col_mo, col_gz = 4 * mix_w, 5 * mix_w, 6 * mix_w, 7 * mix_w

    new_prompt = [[] for _ in range(7)]
    new_sample = [[] for _ in range(7)]
    for l in range(depth):
        g = norm_gains[l]
        x = _ffn(x, g[0], g[1], w_ffn1_up[l].astype(BF16), w_ffn1_down[l].astype(BF16))
        cols = _inproj(x, g[2], w_in[l].astype(BF16))

        wb, wc = _s5_weights(bb_re[l], bb_im[l], s5_c_re[l], s5_c_im[l])
        abr = ab_re[l].reshape(slabs, LANES)
        abi = ab_im[l].reshape(slabs, LANES)
        s5d = s5_d[l].reshape(1, mix_w)
        wq, wk, wv = (w[l].astype(BF16) for w in (mlstm_wq, mlstm_wk, mlstm_wv))
        wg = mlstm_w_gates[l].astype(BF16)

        groups_cfg = (
            (0, (zeros(batch, heads_hg, HG_DK, mix_w // heads_hg),
                 zeros(batch, slabs, LANES), zeros(batch, slabs, LANES),
                 zeros(batch, ML_HEADS, mix_w // ML_HEADS, mix_w // ML_HEADS),
                 zeros(batch, ML_HEADS, mix_w // ML_HEADS), zeros(batch, ML_HEADS),
                 zeros(batch, CONV_W - 1, mix_w)),
             dict(seq=seq, hg=dict(nb=1, tl=512, c=16), s5=dict(nb=1, tl=128), mp=dict(nb=1, tl=256),
                  mr=dict(nb=1, tl=256, c=64))),
            (n_prompt, (state_hgrn[l], state_s5_re[l].reshape(dec_batch, slabs, LANES),
                        state_s5_im[l].reshape(dec_batch, slabs, LANES), state_mlstm_c[l],
                        state_mlstm_n[l], state_mlstm_m[l], state_mlstm_conv[l]),
             dict(seq=dec_seq, hg=dict(nb=16, tl=dec_seq, c=dec_seq), s5=dict(nb=16, tl=dec_seq),
                  mp=dict(nb=16, tl=dec_seq), mr=dict(nb=8, tl=dec_seq, c=dec_seq))),
        )
        oa_parts, ob_parts, oc_parts = [], [], []
        for (row0, st, cfg), new in zip(groups_cfg, (new_prompt, new_sample)):
            s_hg, s_re, s_im, s_c, s_n, s_m, s_conv = st
            nseq = s_hg.shape[0]
            oa, hg_new = _hgrn(cols, lb_all[l], hgrn_norm[l], s_hg, row0=row0, seq=cfg['seq'], **cfg['hg'])
            ob, re_new, im_new = _s5(cols, wb, wc, abr, abi, s5d, s_re, s_im,
                                     row0=row0, col0=col_su, seq=cfg['seq'], **cfg['s5'])
            q, k, v, gates, conv_new = _mlproj(cols, s_conv, mlstm_conv_w[l], mlstm_conv_b[l],
                                               wq, wk, wv, wg, mlstm_b_gates[l],
                                               row0=row0, col0=col_mx, seq=cfg['seq'], **cfg['mp'])
            oc, c_new, n_new, m_new = _mlrec(q, k, v, gates, cols, mlstm_norm[l], s_c, s_n, s_m,
                                             row0=row0, col0=col_mo, **cfg['mr'])
            oa_parts.append(oa)
            ob_parts.append(ob)
            oc_parts.append(oc)
            for lst, val in zip(new, (hg_new, re_new.reshape(nseq, groups, state),
                                      im_new.reshape(nseq, groups, state), c_new, n_new, m_new,
                                      conv_new)):
                lst.append(val)

        x = _merge(x, jnp.concatenate(oa_parts), jnp.concatenate(ob_parts),
                   jnp.concatenate(oc_parts), cols,
                   w_hgrn_out[l].astype(BF16), w_s5_glu_a[l].astype(BF16),
                   w_s5_glu_b[l].astype(BF16), w_mlstm_out[l].astype(BF16),
                   w_out[l].astype(BF16), g[3], col0=col_gz)
        x = _ffn(x, g[4], g[5], w_ffn2_up[l].astype(BF16), w_ffn2_down[l].astype(BF16))

    y_prompt = x[:n_prompt].reshape(batch, seq, d_model)
    y_sample = x[n_prompt:].reshape(dec_batch, dec_seq, d_model)
    stack = lambda lists: tuple(jnp.stack(v) for v in lists)
    return (y_prompt, y_sample) + stack(new_prompt) + stack(new_sample)
```

```python
import functools
import math

import jax
import jax.numpy as jnp
from jax import lax
from jax.experimental import pallas as pl
from jax.experimental.pallas import tpu as pltpu

F32 = jnp.float32
BF16 = jnp.bfloat16

EPS = 1e-6
NEG_BIG = -1e30
S5_DT_MIN_CLAMP = -1e-4

HG_DK = 128
S5_CH = 16
S5_SUPER = 8
ML_HEADS = 4
CONV_W = 4
N_BRANCH = 3

LANES = 128
SUBLANES = 8
VMEM_LIMIT = 56 * 1024 * 1024


def _cparams(sem):
    return pltpu.CompilerParams(dimension_semantics=sem, vmem_limit_bytes=VMEM_LIMIT)


def _rms_scale(y):
    return lax.rsqrt(jnp.mean(y * y, axis=-1, keepdims=True) + EPS)


def _silu(x):
    return x * jax.nn.sigmoid(x)


def _ffn_body(x_ref, gpre_ref, gpost_ref, wa_ref, wb_ref, wd_ref, o_ref, h_ref, acc_ref):
    j = pl.program_id(1)

    @pl.when(j == 0)
    def _():
        x = x_ref[...]
        h_ref[...] = (x * _rms_scale(x) * gpre_ref[...]).astype(BF16)
        acc_ref[...] = jnp.zeros_like(acc_ref)

    h = h_ref[...]
    a = jnp.dot(h, wa_ref[...], preferred_element_type=F32)
    b = jnp.dot(h, wb_ref[...], preferred_element_type=F32)
    act = (_silu(a) * b).astype(BF16)
    acc_ref[...] += jnp.dot(act, wd_ref[...], preferred_element_type=F32)

    @pl.when(j == pl.num_programs(1) - 1)
    def _():
        y = acc_ref[...]
        o_ref[...] = x_ref[...] + 0.5 * (y * _rms_scale(y) * gpost_ref[...])


def _ffn(x, g_pre, g_post, w_up, w_down, *, tm=512, tf=512):
    m, d = x.shape
    d_ff = w_down.shape[0]
    nf = d_ff // tf
    return pl.pallas_call(
        _ffn_body,
        grid=(m // tm, nf),
        in_specs=[
            pl.BlockSpec((tm, d), lambda i, j: (i, 0)),
            pl.BlockSpec((1, d), lambda i, j: (0, 0)),
            pl.BlockSpec((1, d), lambda i, j: (0, 0)),
            pl.BlockSpec((d, tf), lambda i, j: (0, j)),
            pl.BlockSpec((d, tf), lambda i, j: (0, j + nf)),
            pl.BlockSpec((tf, d), lambda i, j: (j, 0)),
        ],
        out_specs=pl.BlockSpec((tm, d), lambda i, j: (i, 0)),
        out_shape=jax.ShapeDtypeStruct((m, d), F32),
        scratch_shapes=[pltpu.VMEM((tm, d), BF16), pltpu.VMEM((tm, d), F32)],
        compiler_params=_cparams(("parallel", "arbitrary")),
        name="ffn",
    )(x, g_pre.reshape(1, d), g_post.reshape(1, d), w_up, w_up, w_down)


def _inproj_body(x_ref, g_ref, w_ref, o_ref, h_ref):
    @pl.when(pl.program_id(1) == 0)
    def _():
        x = x_ref[...]
        h_ref[...] = (x * _rms_scale(x) * g_ref[...]).astype(BF16)

    o_ref[...] = jnp.dot(h_ref[...], w_ref[...], preferred_element_type=F32)


def _inproj(x, g, w, *, tm=512, tn=1024):
    m, d = x.shape
    n = w.shape[1]
    return pl.pallas_call(
        _inproj_body,
        grid=(m // tm, n // tn),
        in_specs=[
            pl.BlockSpec((tm, d), lambda i, j: (i, 0)),
            pl.BlockSpec((1, d), lambda i, j: (0, 0)),
            pl.BlockSpec((d, tn), lambda i, j: (0, j)),
        ],
        out_specs=pl.BlockSpec((tm, tn), lambda i, j: (i, j)),
        out_shape=jax.ShapeDtypeStruct((m, n), F32),
        scratch_shapes=[pltpu.VMEM((tm, d), BF16)],
        compiler_params=_cparams(("parallel", "arbitrary")),
        name="inproj",
    )(x, g.reshape(1, d), w)


def _prep_body(lbraw_ref, are_ref, aim_ref, ldt_ref, bre_ref, bim_ref,
               lb_ref, abr_ref, abi_ref, bbr_ref, bbi_ref):
    depth = lbraw_ref.shape[0]
    raw = lbraw_ref[...]
    e = jnp.exp(raw - jnp.max(raw, axis=0, keepdims=True))
    lbs = e / jnp.sum(e, axis=0, keepdims=True)
    run = jnp.zeros_like(lbs[0:1])
    for l in range(depth):
        run = run + lbs[l:l + 1]
        lb_ref[l:l + 1, :] = run - lbs[0:1]

    for l in range(depth):
        dt = jnp.exp(ldt_ref[l])
        lam_re = jnp.minimum(are_ref[l], S5_DT_MIN_CLAMP)
        lam_im = aim_ref[l]
        mag = jnp.exp(lam_re * dt)
        ab_re = mag * jnp.cos(lam_im * dt)
        ab_im = mag * jnp.sin(lam_im * dt)
        inv = 1.0 / (lam_re * lam_re + lam_im * lam_im)
        f_re = ((ab_re - 1.0) * lam_re + ab_im * lam_im) * inv
        f_im = (ab_im * lam_re - (ab_re - 1.0) * lam_im) * inv
        abr_ref[l] = ab_re
        abi_ref[l] = ab_im
        b_re = bre_ref[l]
        b_im = bim_ref[l]
        bbr_ref[l] = f_re * b_re - f_im * b_im
        bbi_ref[l] = f_re * b_im + f_im * b_re


def _prep(lb_raw, a_re, a_im, log_dt, b_re, b_im):
    depth, groups, state = a_re.shape
    gp = groups * state
    ch = b_re.shape[-1]
    row = lambda a: a.reshape(depth, 1, gp)
    ldt = jnp.broadcast_to(log_dt[:, :, None], (depth, groups, state))
    chan_major = lambda b: b.transpose(0, 3, 1, 2).reshape(depth, ch, gp)
    out_shape = (
        jax.ShapeDtypeStruct(lb_raw.shape, F32),
        jax.ShapeDtypeStruct((depth, 1, gp), F32),
        jax.ShapeDtypeStruct((depth, 1, gp), F32),
        jax.ShapeDtypeStruct((depth, ch, gp), F32),
        jax.ShapeDtypeStruct((depth, ch, gp), F32),
    )
    return pl.pallas_call(_prep_body, out_shape=out_shape, name="prep")(
        lb_raw, row(a_re), row(a_im), row(ldt), chan_major(b_re), chan_major(b_im))


def _hgrn_body(*refs, nb, tl, c, heads, has_init, n_prev):
    q_ref, f_ref, i_ref, g_ref, lb_ref, nrm_ref = refs[:6]
    s0_ref = refs[6] if has_init else None
    o_ref, s_ref, st_ref = refs[6 + has_init + n_prev:]
    t = pl.program_id(1)

    @pl.when(t == 0)
    def _():
        def init(n, carry):
            for h in range(heads):
                if has_init:
                    st_ref[n, h] = s0_ref[n, h].T
                else:
                    st_ref[n, h] = jnp.zeros(st_ref.shape[2:], F32)
            return carry

        lax.fori_loop(0, nb, init, 0)

    row = lax.broadcasted_iota(jnp.int32, (c, HG_DK), 0)
    lag = row - lax.broadcasted_iota(jnp.int32, (c, HG_DK), 1)
    ones = jnp.ones((HG_DK, LANES), BF16)
    chunks_per_seq = tl // c

    def chunk(idx, carry):
        n = idx // chunks_per_seq
        rows = pl.ds(pl.multiple_of(idx * c, c), c)
        for h in range(heads):
            hs = slice(h * HG_DK, (h + 1) * HG_DK)
            lb = lb_ref[:, hs]
            fpre = f_ref[rows, hs]
            q = _silu(q_ref[rows, hs])
            v = i_ref[rows, hs]
            vb = v.astype(BF16)
            forget = lb + (1.0 - lb) * jax.nn.sigmoid(fpre)
            kin = (1.0 - lb) * jax.nn.sigmoid(-fpre)

            cp = forget
            sh = 1
            while sh < c:
                cp = cp * jnp.where(row >= sh, pltpu.roll(cp, sh, 0), 1.0)
                sh *= 2
            sp = jnp.where(row < c - 1, pltpu.roll(forget, c - 1, 0), 1.0)
            sh = 1
            while sh < c:
                sp = sp * jnp.where(row < c - sh, pltpu.roll(sp, c - sh, 0), 1.0)
                sh *= 2

            st = st_ref[n, h]
            out = lax.dot_general((q * cp).astype(BF16), st.astype(BF16),
                                  (((1,), (1,)), ((), ())), preferred_element_type=F32)
            kd = kin
            prods = [q * kd]
            for d in range(1, c):
                kd = pltpu.roll(kd, 1, 0) * forget
                prods.append(q * kd)
            sums = jnp.dot(jnp.concatenate(prods, axis=0).astype(BF16), ones,
                           preferred_element_type=F32)
            scores = jnp.zeros((c, HG_DK), F32)
            for d in range(c):
                scores = jnp.where(lag == d, sums[d * c:(d + 1) * c], scores)
            out = out + jnp.dot(scores[:, :c].astype(BF16), vb, preferred_element_type=F32)

            upd = lax.dot_general(vb, (kin * sp).astype(BF16),
                                  (((0,), (0,)), ((), ())), preferred_element_type=F32)
            st_ref[n, h] = st * cp[c - 1:c, :] + upd
            o_ref[rows, hs] = out * _rms_scale(out) * nrm_ref[:, hs] * _silu(g_ref[rows, hs])
        return carry

    lax.fori_loop(0, nb * chunks_per_seq, chunk, 0)

    @pl.when(t == pl.num_programs(1) - 1)
    def _():
        def fin(n, carry):
            for h in range(heads):
                s_ref[n, h] = st_ref[n, h].T
            return carry

        lax.fori_loop(0, nb, fin, 0)


def _layer_state_spec(shape, nb, layer):
    tail = (0,) * (len(shape) - 2)
    return pl.BlockSpec((None, nb) + tuple(shape[2:]), lambda b, t: (layer, b) + tail)


def _hgrn(cols, lb, nrm, s_init, prevs, *, layer, nseq, heads, row0, seq, nb, tl, c):
    depth = lb.shape[0]
    dk = HG_DK
    mix_w = heads * dk
    dv = mix_w // heads
    nt = seq // tl
    blk = nb * tl
    rb0 = row0 // blk
    sshape = (depth, nseq, heads, dk, dv)

    def colspec(group):
        return pl.BlockSpec((blk, mix_w), lambda b, t: (rb0 + b * nt + t, group))

    vec = pl.BlockSpec((None, 1, mix_w), lambda b, t: (layer, 0, 0))
    sspec = _layer_state_spec(sshape, nb, layer)
    args = [cols, cols, cols, cols, lb.reshape(depth, 1, mix_w), nrm.reshape(depth, 1, mix_w)]
    st_specs, st_args, aliases = _state_args(
        [sspec], None if s_init is None else (s_init,), prevs, len(args))
    return pl.pallas_call(
        functools.partial(_hgrn_body, nb=nb, tl=tl, c=c, heads=heads,
                          has_init=s_init is not None, n_prev=len(aliases)),
        grid=(nseq // nb, nt),
        in_specs=[colspec(0), colspec(1), colspec(2), colspec(3), vec, vec] + st_specs,
        out_specs=(pl.BlockSpec((blk, mix_w), lambda b, t: (rb0 + b * nt + t, 0)), sspec),
        out_shape=(jax.ShapeDtypeStruct((cols.shape[0], mix_w), F32),
                   jax.ShapeDtypeStruct(sshape, F32)),
        scratch_shapes=[pltpu.VMEM((nb, heads, dv, dk), F32)],
        input_output_aliases=aliases,
        compiler_params=_cparams(("parallel", "arbitrary")),
        name="hgrn",
    )(*args, *st_args)


def _gelu_tanh(x):
    return 0.5 * x * (1.0 + jnp.tanh(math.sqrt(2.0 / math.pi) * (x + 0.044715 * (x * x * x))))


def _state_args(specs, inits, prevs, n_args):
    in_specs, args, aliases = [], [], {}
    if inits is not None:
        in_specs += list(specs)
        args += list(inits)
    for out_idx, prev in enumerate(prevs):
        if prev is not None:
            aliases[n_args + len(args)] = out_idx
            in_specs.append(pl.BlockSpec(memory_space=pl.ANY))
            args.append(prev)
    return in_specs, args, aliases


def _s5_body(*refs, nb, tl, has_init, n_prev):
    u_ref, wb_ref, wc_ref, abr_ref, abi_ref, d_ref = refs[:6]
    y_ref, xr_ref, xi_ref, sr_ref, si_ref = refs[6 + 2 * has_init + n_prev:]
    rows = nb * tl
    pitch = rows + SUBLANES
    nsb = wb_ref.shape[0]
    cw = wb_ref.shape[1]
    half = wb_ref.shape[2] // 2
    per = half // LANES
    slabs = nsb * per
    t = pl.program_id(1)

    @pl.when(t == 0)
    def _():
        if has_init:
            xr_ref[...] = refs[6][...]
            xi_ref[...] = refs[7][...]
        else:
            xr_ref[...] = jnp.zeros_like(xr_ref)
            xi_ref[...] = jnp.zeros_like(xi_ref)

    for sb in range(nsb):
        ub = u_ref[:, sb * cw:(sb + 1) * cw].astype(BF16)
        res = jnp.dot(ub, wb_ref[sb], preferred_element_type=F32)
        for j in range(per):
            base = (sb * per + j) * pitch
            sr_ref[pl.ds(base, rows), :] = res[:, j * LANES:(j + 1) * LANES]
            si_ref[pl.ds(base, rows), :] = res[:, half + j * LANES:half + (j + 1) * LANES]

    ar = abr_ref[...]
    ai = abi_ref[...]

    def seq_scan(n, carry):
        def step(tt, x):
            xr, xi = x
            at = pl.ds(n * tl + tt, slabs, stride=pitch)
            nxr = ar * xr - ai * xi + sr_ref[at, :]
            nxi = ar * xi + ai * xr + si_ref[at, :]
            sr_ref[at, :] = nxr
            si_ref[at, :] = nxi
            return nxr, nxi

        xr, xi = lax.fori_loop(0, tl, step, (xr_ref[n], xi_ref[n]))
        xr_ref[n] = xr
        xi_ref[n] = xi
        return carry

    lax.fori_loop(0, nb, seq_scan, 0)

    for sb in range(nsb):
        parts = [sr_ref[pl.ds((sb * per + j) * pitch, rows), :] for j in range(per)]
        parts += [si_ref[pl.ds((sb * per + j) * pitch, rows), :] for j in range(per)]
        xs = jnp.concatenate(parts, axis=1).astype(BF16)
        cs = slice(sb * cw, (sb + 1) * cw)
        y = jnp.dot(xs, wc_ref[sb], preferred_element_type=F32) + d_ref[:, cs] * u_ref[:, cs]
        y_ref[:, cs] = _gelu_tanh(y)


def _s5(cols, wb, wc, ab_re, ab_im, d, inits, prevs, *, layer, depth, nseq, row0, col0, seq, nb, tl):
    slabs = ab_re.shape[0]
    nt = seq // tl
    blk = nb * tl
    rb0 = row0 // blk
    mix_w = d.shape[-1]
    cb = col0 // mix_w
    const3 = lambda a: pl.BlockSpec(a.shape, lambda b, t: (0, 0, 0))
    const2 = lambda a: pl.BlockSpec(a.shape, lambda b, t: (0, 0))
    sshape = (depth, nseq, slabs, LANES)
    xspec = _layer_state_spec(sshape, nb, layer)
    pitch = blk + SUBLANES
    args = [cols, wb, wc, ab_re, ab_im, d]
    st_specs, st_args, aliases = _state_args([xspec, xspec], inits, prevs, len(args))
    return pl.pallas_call(
        functools.partial(_s5_body, nb=nb, tl=tl, has_init=inits is not None, n_prev=len(aliases)),
        grid=(nseq // nb, nt),
        in_specs=[pl.BlockSpec((blk, mix_w), lambda b, t: (rb0 + b * nt + t, cb)),
                  const3(wb), const3(wc), const2(ab_re), const2(ab_im), const2(d)] + st_specs,
        out_specs=(pl.BlockSpec((blk, mix_w), lambda b, t: (rb0 + b * nt + t, 0)), xspec, xspec),
        out_shape=(jax.ShapeDtypeStruct((cols.shape[0], mix_w), F32),
                   jax.ShapeDtypeStruct(sshape, F32),
                   jax.ShapeDtypeStruct(sshape, F32)),
        scratch_shapes=[pltpu.VMEM((slabs * pitch, LANES), F32),
                        pltpu.VMEM((slabs * pitch, LANES), F32)],
        input_output_aliases=aliases,
        compiler_params=_cparams(("parallel", "arbitrary")),
        name="s5",
    )(*args, *st_args)


def _s5_weights(bb_re, bb_im, c_re, c_im):
    ch, gp = bb_re.shape
    groups = c_re.shape[0]
    state = gp // groups
    nsb = groups // S5_SUPER
    eye = jnp.eye(S5_SUPER, dtype=F32)

    def in_w(bb):
        b4 = bb.reshape(ch, nsb, S5_SUPER, state)
        return jnp.einsum('csgp,gh->sgchp', b4, eye).reshape(nsb, S5_SUPER * ch, S5_SUPER * state)

    def out_w(c):
        c4 = c.reshape(nsb, S5_SUPER, ch, state)
        return jnp.einsum('sgcp,gh->sgphc', c4, eye).reshape(nsb, S5_SUPER * state, S5_SUPER * ch)

    wb = jnp.concatenate([in_w(bb_re), in_w(bb_im)], axis=2).astype(BF16)
    wc = jnp.concatenate([out_w(c_re), -out_w(c_im)], axis=1).astype(BF16)
    return wb, wc


def _mlproj_body(*refs, nb, tl, has_init, n_prev):
    x_ref, cw_ref, cb_ref, wq_ref, wk_ref, wv_ref, wg_ref, bg_ref = refs[:8]
    q_ref, k_ref, v_ref, g_ref, cn_ref, xx_ref = refs[8 + has_init + n_prev:]
    t = pl.program_id(1)
    width = x_ref.shape[1]
    halo = CONV_W - 1
    lo = SUBLANES - halo

    @pl.when(t == 0)
    def _():
        if has_init:
            xx_ref[:, lo:SUBLANES, :] = refs[8][...]
        else:
            xx_ref[:, lo:SUBLANES, :] = jnp.zeros((nb, halo, width), F32)

    @pl.when(t > 0)
    def _():
        xx_ref[:, lo:SUBLANES, :] = xx_ref[:, tl + lo:tl + SUBLANES, :]

    x = x_ref[...]
    xx_ref[:, SUBLANES:, :] = x.reshape(nb, tl, width)
    xc = jnp.zeros((nb, tl, width), F32) + cb_ref[...][None]
    for j in range(CONV_W):
        xc = xc + xx_ref[:, lo + j:lo + j + tl, :] * cw_ref[j:j + 1, :][None]
    cn_ref[...] = xx_ref[:, tl + lo:tl + SUBLANES, :]
    xc = _silu(xc).reshape(nb * tl, width).astype(BF16)
    xb = x.astype(BF16)

    dh = width // ML_HEADS
    gates = jnp.zeros((nb * tl, 2 * ML_HEADS), F32) + bg_ref[...]
    for h in range(ML_HEADS):
        hs = slice(h * dh, (h + 1) * dh)
        q = jnp.dot(xc[:, hs], wq_ref[h], preferred_element_type=F32)
        k = jnp.dot(xc[:, hs], wk_ref[h], preferred_element_type=F32)
        v = jnp.dot(xb[:, hs], wv_ref[h], preferred_element_type=F32)
        q_ref[:, hs] = q
        k_ref[:, hs] = k
        v_ref[:, hs] = v
        for part, val in enumerate((q, k, v)):
            w = wg_ref[part * width + h * dh:part * width + (h + 1) * dh, :]
            gates = gates + jnp.dot(val.astype(BF16), w, preferred_element_type=F32)
    g_ref[...] = gates


def _mlproj(cols, conv_w, conv_b, wq, wk, wv, wg, bg, init, prev, *, layer, depth, nseq, row0, col0,
            seq, nb, tl):
    width = conv_w.shape[1]
    halo = CONV_W - 1
    nt = seq // tl
    blk = nb * tl
    rb0 = row0 // blk
    cb = col0 // width
    ng = wg.shape[1]
    full = lambda a: pl.BlockSpec(a.shape, lambda b, t: (0,) * a.ndim)
    rowspec = lambda w: pl.BlockSpec((blk, w), lambda b, t: (b * nt + t, 0))
    sshape = (depth, nseq, halo, width)
    cspec = _layer_state_spec(sshape, nb, layer)
    args = [cols, conv_w, conv_b.reshape(1, width), wq, wk, wv, wg, bg.reshape(1, ng)]
    st_specs, st_args, aliases = _state_args(
        [cspec], None if init is None else (init,), (None, None, None, None, prev), len(args))
    return pl.pallas_call(
        functools.partial(_mlproj_body, nb=nb, tl=tl, has_init=init is not None,
                          n_prev=len(aliases)),
        grid=(nseq // nb, nt),
        in_specs=[pl.BlockSpec((blk, width), lambda b, t: (rb0 + b * nt + t, cb))]
                 + [full(a) for a in args[1:]] + st_specs,
        out_specs=(rowspec(width), rowspec(width), rowspec(width), rowspec(ng), cspec),
        out_shape=(jax.ShapeDtypeStruct((nseq * seq, width), F32),) * 3
                  + (jax.ShapeDtypeStruct((nseq * seq, ng), F32),
                     jax.ShapeDtypeStruct(sshape, F32)),
        scratch_shapes=[pltpu.VMEM((nb, tl + SUBLANES, width), F32)],
        input_output_aliases=aliases,
        compiler_params=_cparams(("parallel", "arbitrary")),
        name="mlproj",
    )(*args, *st_args)


def _mlrec_body(*refs, nb, tl, c, heads, has_init, n_prev):
    q_ref, k_ref, v_ref, g_ref, o_ref, nrm_ref = refs[:6]
    h_ref, c_ref, n_ref, m_ref = refs[6 + 3 * has_init + n_prev:]
    t = pl.program_id(1)
    dh = q_ref.shape[1] // heads

    @pl.when(t == 0)
    def _():
        if has_init:
            c_ref[...] = refs[6][...]
            n_ref[...] = refs[7][...]
            m_ref[...] = refs[8][...]
        else:
            c_ref[...] = jnp.zeros_like(c_ref)
            n_ref[...] = jnp.zeros_like(n_ref)
            m_ref[...] = jnp.zeros_like(m_ref)

    glane = lax.broadcasted_iota(jnp.int32, (c, 2 * heads), 1)
    r = lax.broadcasted_iota(jnp.int32, (c, c), 0)
    s = lax.broadcasted_iota(jnp.int32, (c, c), 1)
    causal = r >= s
    chunks_per_seq = tl // c
    k_scale = dh ** -0.5

    def chunk(idx, carry):
        n = idx // chunks_per_seq
        rows = pl.ds(pl.multiple_of(idx * c, c), c)
        gt = g_ref[rows, :]
        for h in range(heads):
            hs = slice(h * dh, (h + 1) * dh)
            q = q_ref[rows, hs]
            k = k_ref[rows, hs] * k_scale
            v = v_ref[rows, hs]
            ig_col = jnp.sum(jnp.where(glane == h, gt, 0.0), axis=1, keepdims=True)
            fp_col = jnp.sum(jnp.where(glane == h + heads, gt, 0.0), axis=1, keepdims=True)
            lf_col = jnp.minimum(fp_col, 0.0) - jnp.log1p(jnp.exp(-jnp.abs(fp_col)))
            lf_row = jnp.sum(jnp.where(r == s, lf_col, 0.0), axis=0, keepdims=True)
            ig_row = jnp.sum(jnp.where(r == s, ig_col, 0.0), axis=0, keepdims=True)
            b_col = jnp.sum(jnp.where(causal, lf_row, 0.0), axis=1, keepdims=True)
            b_row = jnp.sum(jnp.where(r <= s, lf_col, 0.0), axis=0, keepdims=True)

            m_prev = m_ref[n, h][:, 0:1]
            n_prev = n_ref[n, h]
            c_prev = c_ref[n, h]

            logw = jnp.where(causal, b_col - b_row + ig_row, NEG_BIG)
            m_t = jnp.maximum(b_col + m_prev, jnp.max(logw, axis=1, keepdims=True))
            w_prev = jnp.exp(b_col + m_prev - m_t)
            w = jnp.exp(logw - m_t)
            qb = q.astype(BF16)
            vb = v.astype(BF16)
            sc = lax.dot_general(qb, k.astype(BF16), (((1,), (1,)), ((), ())),
                                 preferred_element_type=F32) * w
            num = (w_prev * jnp.dot(qb, c_prev.astype(BF16), preferred_element_type=F32)
                   + jnp.dot(sc.astype(BF16), vb, preferred_element_type=F32))
            den = (w_prev * jnp.sum(q * n_prev, axis=1, keepdims=True)
                   + jnp.sum(sc, axis=1, keepdims=True))
            hval = num / jnp.maximum(jnp.abs(den), jnp.exp(-m_t))

            m_end = m_t[c - 1:c, :]
            b_end = b_col[c - 1:c, :]
            g_prev = jnp.exp(b_end + m_prev - m_end)
            w_in = jnp.exp(b_end - b_col + ig_col - m_end)
            kw = k * w_in
            c_ref[n, h] = g_prev * c_prev + lax.dot_general(
                kw.astype(BF16), vb, (((0,), (0,)), ((), ())), preferred_element_type=F32)
            n_ref[n, h] = g_prev * n_prev + jnp.sum(kw, axis=0, keepdims=True)
            m_ref[n, h] = jnp.broadcast_to(m_end, (1, LANES))
            h_ref[rows, hs] = (hval * _rms_scale(hval) * nrm_ref[:, hs]
                               * jax.nn.sigmoid(o_ref[rows, hs]))
        return carry

    lax.fori_loop(0, nb * chunks_per_seq, chunk, 0)


def _mlrec(q, k, v, gates, cols, nrm, inits, prevs, *, layer, depth, nseq, heads, row0, col0,
           nb, tl, c):
    width = q.shape[1]
    dh = width // heads
    seq = q.shape[0] // nseq
    nt = seq // tl
    blk = nb * tl
    rb0 = row0 // blk
    cb = col0 // width
    ng = gates.shape[1]
    qspec = pl.BlockSpec((blk, width), lambda b, t: (b * nt + t, 0))
    ospec = pl.BlockSpec((blk, width), lambda b, t: (rb0 + b * nt + t, 0))
    shapes = ((depth, nseq, heads, dh, dh), (depth, nseq, heads, 1, dh), (depth, nseq, heads, 1, LANES))
    sspecs = [_layer_state_spec(sh, nb, layer) for sh in shapes]
    args = [q, k, v, gates, cols, nrm.reshape(1, width)]
    st_specs, st_args, aliases = _state_args(sspecs, inits, prevs, len(args))
    return pl.pallas_call(
        functools.partial(_mlrec_body, nb=nb, tl=tl, c=c, heads=heads, has_init=inits is not None,
                          n_prev=len(aliases)),
        grid=(nseq // nb, nt),
        in_specs=[qspec, qspec, qspec,
                  pl.BlockSpec((blk, ng), lambda b, t: (b * nt + t, 0)),
                  pl.BlockSpec((blk, width), lambda b, t: (rb0 + b * nt + t, cb)),
                  pl.BlockSpec((1, width), lambda b, t: (0, 0))] + st_specs,
        out_specs=(ospec, *sspecs),
        out_shape=(jax.ShapeDtypeStruct((cols.shape[0], width), F32),)
                  + tuple(jax.ShapeDtypeStruct(sh, F32) for sh in shapes),
        input_output_aliases=aliases,
        compiler_params=_cparams(("parallel", "arbitrary")),
        name="mlrec",
    )(*args, *st_args)


def _merge_body(x_ref, oa_ref, ob_ref, oc_ref, ga_ref, gb_ref, gc_ref, wa_ref, wba_ref, wbb_ref,
                wc_ref, wo_ref, g_ref, o_ref, a_scr, b_scr, c_scr, acc_ref):
    j = pl.program_id(1)

    @pl.when(j == 0)
    def _():
        a_scr[...] = oa_ref[...].astype(BF16)
        b_scr[...] = ob_ref[...].astype(BF16)
        c_scr[...] = oc_ref[...].astype(BF16)
        acc_ref[...] = jnp.zeros_like(acc_ref)

    yb = b_scr[...]
    br_a = jnp.dot(a_scr[...], wa_ref[...], preferred_element_type=F32)
    br_b = (jnp.dot(yb, wba_ref[...], preferred_element_type=F32)
            * jax.nn.sigmoid(jnp.dot(yb, wbb_ref[...], preferred_element_type=F32)))
    br_c = jnp.dot(c_scr[...], wc_ref[...], preferred_element_type=F32)
    merged = (jax.nn.sigmoid(ga_ref[...]) * br_a + jax.nn.sigmoid(gb_ref[...]) * br_b
              + jax.nn.sigmoid(gc_ref[...]) * br_c)
    acc_ref[...] += jnp.dot(merged.astype(BF16), wo_ref[...], preferred_element_type=F32)

    @pl.when(j == pl.num_programs(1) - 1)
    def _():
        y = acc_ref[...]
        o_ref[...] = x_ref[...] + y * _rms_scale(y) * g_ref[...]


def _merge(x, oa, ob, oc, cols, w_a, w_ba, w_bb, w_c, w_out, g, *, col0, tm=256, tn=512):
    m, d = x.shape
    mix_w = oa.shape[1]
    nj = d // tn
    gb0 = col0 // tn
    act = pl.BlockSpec((tm, mix_w), lambda i, j: (i, 0))
    wspec = pl.BlockSpec((mix_w, tn), lambda i, j: (0, j))

    def gate(branch):
        return pl.BlockSpec((tm, tn), lambda i, j: (i, gb0 + branch * nj + j))

    return pl.pallas_call(
        _merge_body,
        grid=(m // tm, nj),
        in_specs=[pl.BlockSpec((tm, d), lambda i, j: (i, 0)), act, act, act,
                  gate(0), gate(1), gate(2), wspec, wspec, wspec, wspec,
                  pl.BlockSpec((tn, d), lambda i, j: (j, 0)),
                  pl.BlockSpec((1, d), lambda i, j: (0, 0))],
        out_specs=pl.BlockSpec((tm, d), lambda i, j: (i, 0)),
        out_shape=jax.ShapeDtypeStruct((m, d), F32),
        scratch_shapes=[pltpu.VMEM((tm, mix_w), BF16)] * 3 + [pltpu.VMEM((tm, d), F32)],
        compiler_params=_cparams(("parallel", "arbitrary")),
        name="merge",
    )(x, oa, ob, oc, cols, cols, cols, w_a, w_ba, w_bb, w_c, w_out, g.reshape(1, d))


def kernel(x_prompt, x_sample, state_hgrn, state_s5_re, state_s5_im, state_mlstm_c, state_mlstm_n, state_mlstm_m, state_mlstm_conv, norm_gains, w_ffn1_up, w_ffn1_down, w_in, hgrn_lower_bounds, hgrn_norm, w_hgrn_out, s5_a_re, s5_a_im, s5_log_dt, s5_b_re, s5_b_im, s5_c_re, s5_c_im, s5_d, w_s5_glu_a, w_s5_glu_b, mlstm_conv_w, mlstm_conv_b, mlstm_wq, mlstm_wk, mlstm_wv, mlstm_w_gates, mlstm_b_gates, mlstm_norm, w_mlstm_out, w_out, w_ffn2_up, w_ffn2_down):
    batch, seq, d_model = x_prompt.shape
    dec_batch, dec_seq, _ = x_sample.shape
    depth = w_in.shape[0]
    mix_w = w_hgrn_out.shape[1]
    n_prompt = batch * seq
    heads_hg = state_hgrn.shape[2]
    groups, state = s5_a_re.shape[1:]
    slabs = groups * state // LANES

    x = jnp.concatenate([x_prompt.reshape(n_prompt, d_model),
                         x_sample.reshape(dec_batch * dec_seq, d_model)], axis=0)

    lb_all, ab_re, ab_im, bb_re, bb_im = _prep(hgrn_lower_bounds, s5_a_re, s5_a_im, s5_log_dt,
                                               s5_b_re, s5_b_im)

    col_su, col_mx, col_mo, col_gz = 4 * mix_w, 5 * mix_w, 6 * mix_w, 7 * mix_w
    dh = mix_w // ML_HEADS

    sample_init = dict(
        hg=state_hgrn,
        s5=(state_s5_re.reshape(depth, dec_batch, slabs, LANES),
            state_s5_im.reshape(depth, dec_batch, slabs, LANES)),
        conv=state_mlstm_conv,
        ml=(state_mlstm_c, state_mlstm_n.reshape(depth, dec_batch, ML_HEADS, 1, dh),
            jnp.broadcast_to(state_mlstm_m[..., None, None], (depth, dec_batch, ML_HEADS, 1, LANES))))
    groups_cfg = (
        dict(row0=0, nseq=batch, seq=seq, init=dict(hg=None, s5=None, conv=None, ml=None),
             hg=dict(nb=1, tl=256, c=16), s5=dict(nb=1, tl=128), mp=dict(nb=1, tl=256),
             mr=dict(nb=1, tl=256, c=64)),
        dict(row0=n_prompt, nseq=dec_batch, seq=dec_seq, init=sample_init,
             hg=dict(nb=8, tl=dec_seq, c=dec_seq), s5=dict(nb=16, tl=dec_seq),
             mp=dict(nb=16, tl=dec_seq), mr=dict(nb=4, tl=dec_seq, c=dec_seq)),
    )
    new = [dict(hg=None, s5=(None, None), conv=None, ml=(None, None, None)) for _ in groups_cfg]

    for l in range(depth):
        g = norm_gains[l]
        x = _ffn(x, g[0], g[1], w_ffn1_up[l].astype(BF16), w_ffn1_down[l].astype(BF16))
        cols = _inproj(x, g[2], w_in[l].astype(BF16))

        wb, wc = _s5_weights(bb_re[l], bb_im[l], s5_c_re[l], s5_c_im[l])
        abr = ab_re[l].reshape(slabs, LANES)
        abi = ab_im[l].reshape(slabs, LANES)
        s5d = s5_d[l].reshape(1, mix_w)
        wq, wk, wv = (w[l].astype(BF16) for w in (mlstm_wq, mlstm_wk, mlstm_wv))
        wg = mlstm_w_gates[l].astype(BF16)

        oa = ob = oc = None
        for cfg, st in zip(groups_cfg, new):
            common = dict(layer=l, nseq=cfg['nseq'], row0=cfg['row0'])
            init = cfg['init']
            oa, st['hg'] = _hgrn(cols, lb_all, hgrn_norm, init['hg'], (oa, st['hg']),
                                 heads=heads_hg, seq=cfg['seq'], **common, **cfg['hg'])
            ob, *st['s5'] = _s5(cols, wb, wc, abr, abi, s5d, init['s5'], (ob, *st['s5']),
                                depth=depth, col0=col_su, seq=cfg['seq'], **common, **cfg['s5'])
            q, k, v, gates, st['conv'] = _mlproj(
                cols, mlstm_conv_w[l], mlstm_conv_b[l], wq, wk, wv, wg, mlstm_b_gates[l],
                init['conv'], st['conv'], depth=depth, col0=col_mx, seq=cfg['seq'], **common,
                **cfg['mp'])
            oc, *st['ml'] = _mlrec(q, k, v, gates, cols, mlstm_norm[l], init['ml'], (oc, *st['ml']),
                                   depth=depth, heads=ML_HEADS, col0=col_mo, **common, **cfg['mr'])

        x = _merge(x, oa, ob, oc, cols,
                   w_hgrn_out[l].astype(BF16), w_s5_glu_a[l].astype(BF16),
                   w_s5_glu_b[l].astype(BF16), w_mlstm_out[l].astype(BF16),
                   w_out[l].astype(BF16), g[3], col0=col_gz)
        x = _ffn(x, g[4], g[5], w_ffn2_up[l].astype(BF16), w_ffn2_down[l].astype(BF16))

    y_prompt = x[:n_prompt].reshape(batch, seq, d_model)
    y_sample = x[n_prompt:].reshape(dec_batch, dec_seq, d_model)
    states = ()
    for cfg, st in zip(groups_cfg, new):
        nseq = cfg['nseq']
        c_new, n_new, m_new = st['ml']
        states += (st['hg'],
                   st['s5'][0].reshape(depth, nseq, groups, state),
                   st['s5'][1].reshape(depth, nseq, groups, state),
                   c_new, n_new.reshape(depth, nseq, ML_HEADS, dh), m_new[:, :, :, 0, 0], st['conv'])
    return (y_prompt, y_sample) + states
```

```python
import functools
import math

import jax
import jax.numpy as jnp
from jax import lax
from jax.experimental import pallas as pl
from jax.experimental.pallas import tpu as pltpu

F32 = jnp.float32
BF16 = jnp.bfloat16

EPS = 1e-6
NEG_BIG = -1e30
S5_DT_MIN_CLAMP = -1e-4

HG_DK = 128
S5_CH = 16
S5_SUPER = 8
ML_HEADS = 4
CONV_W = 4
N_GAINS = 6

LANES = 128
SUBLANES = 8
VMEM_LIMIT = 56 * 1024 * 1024


def _cparams(sem):
    return pltpu.CompilerParams(dimension_semantics=sem, vmem_limit_bytes=VMEM_LIMIT)


def _rms_scale(y):
    return lax.rsqrt(jnp.mean(y * y, axis=-1, keepdims=True) + EPS)


def _silu(x):
    return x * jax.nn.sigmoid(x)


def _ffn_body(x_ref, gpre_ref, gpost_ref, wa_ref, wb_ref, wd_ref, o_ref, h_ref, acc_ref):
    j = pl.program_id(1)

    @pl.when(j == 0)
    def _():
        x = x_ref[...]
        h_ref[...] = (x * _rms_scale(x) * gpre_ref[...]).astype(BF16)
        acc_ref[...] = jnp.zeros_like(acc_ref)

    h = h_ref[...]
    a = jnp.dot(h, wa_ref[...], preferred_element_type=F32)
    b = jnp.dot(h, wb_ref[...], preferred_element_type=F32)
    act = (_silu(a) * b).astype(BF16)
    acc_ref[...] += jnp.dot(act, wd_ref[...], preferred_element_type=F32)

    @pl.when(j == pl.num_programs(1) - 1)
    def _():
        y = acc_ref[...]
        o_ref[...] = x_ref[...] + 0.5 * (y * _rms_scale(y) * gpost_ref[...])


def _gain_spec(gains, layer, k):
    return pl.BlockSpec((None, 1, gains.shape[-1]), lambda i, j: (layer * N_GAINS + k, 0, 0))


def _ffn(x, gains, k_pre, k_post, w_up, w_down, *, layer, tm=512, tf=512):
    m, d = x.shape
    d_ff = w_down.shape[1]
    nf = d_ff // tf
    return pl.pallas_call(
        _ffn_body,
        grid=(m // tm, nf),
        in_specs=[
            pl.BlockSpec((tm, d), lambda i, j: (i, 0)),
            _gain_spec(gains, layer, k_pre),
            _gain_spec(gains, layer, k_post),
            pl.BlockSpec((None, d, tf), lambda i, j: (layer, 0, j)),
            pl.BlockSpec((None, d, tf), lambda i, j: (layer, 0, j + nf)),
            pl.BlockSpec((None, tf, d), lambda i, j: (layer, j, 0)),
        ],
        out_specs=pl.BlockSpec((tm, d), lambda i, j: (i, 0)),
        out_shape=jax.ShapeDtypeStruct((m, d), F32),
        scratch_shapes=[pltpu.VMEM((tm, d), BF16), pltpu.VMEM((tm, d), F32)],
        compiler_params=_cparams(("parallel", "arbitrary")),
        name="ffn",
    )(x, gains, gains, w_up, w_up, w_down)


def _inproj_body(x_ref, g_ref, w_ref, o_ref, h_ref):
    @pl.when(pl.program_id(1) == 0)
    def _():
        x = x_ref[...]
        h_ref[...] = (x * _rms_scale(x) * g_ref[...]).astype(BF16)

    o_ref[...] = jnp.dot(h_ref[...], w_ref[...], preferred_element_type=F32)


def _inproj(x, gains, k_gain, w, *, layer, tm=1024, tn=1024):
    m, d = x.shape
    n = w.shape[2]
    return pl.pallas_call(
        _inproj_body,
        grid=(m // tm, n // tn),
        in_specs=[
            pl.BlockSpec((tm, d), lambda i, j: (i, 0)),
            _gain_spec(gains, layer, k_gain),
            pl.BlockSpec((None, d, tn), lambda i, j: (layer, 0, j)),
        ],
        out_specs=pl.BlockSpec((tm, tn), lambda i, j: (i, j)),
        out_shape=jax.ShapeDtypeStruct((m, n), F32),
        scratch_shapes=[pltpu.VMEM((tm, d), BF16)],
        compiler_params=_cparams(("parallel", "arbitrary")),
        name="inproj",
    )(x, gains, w)


def _prep_body(lbraw_ref, are_ref, aim_ref, ldt_ref, bre_ref, bim_ref,
               lb_ref, abr_ref, abi_ref, bbr_ref, bbi_ref):
    depth = lbraw_ref.shape[0]
    raw = lbraw_ref[...]
    e = jnp.exp(raw - jnp.max(raw, axis=0, keepdims=True))
    lbs = e / jnp.sum(e, axis=0, keepdims=True)
    run = jnp.zeros_like(lbs[0:1])
    for l in range(depth):
        run = run + lbs[l:l + 1]
        lb_ref[l:l + 1, :] = run - lbs[0:1]

    for l in range(depth):
        dt = jnp.exp(ldt_ref[l])
        lam_re = jnp.minimum(are_ref[l], S5_DT_MIN_CLAMP)
        lam_im = aim_ref[l]
        mag = jnp.exp(lam_re * dt)
        ab_re = mag * jnp.cos(lam_im * dt)
        ab_im = mag * jnp.sin(lam_im * dt)
        inv = 1.0 / (lam_re * lam_re + lam_im * lam_im)
        f_re = ((ab_re - 1.0) * lam_re + ab_im * lam_im) * inv
        f_im = (ab_im * lam_re - (ab_re - 1.0) * lam_im) * inv
        abr_ref[l] = ab_re
        abi_ref[l] = ab_im
        b_re = bre_ref[l]
        b_im = bim_ref[l]
        bbr_ref[l] = f_re * b_re - f_im * b_im
        bbi_ref[l] = f_re * b_im + f_im * b_re


def _prep(lb_raw, a_re, a_im, log_dt, b_re, b_im):
    depth, groups, state = a_re.shape
    gp = groups * state
    ch = b_re.shape[-1]
    row = lambda a: a.reshape(depth, 1, gp)
    ldt = jnp.broadcast_to(log_dt[:, :, None], (depth, groups, state))
    chan_major = lambda b: b.transpose(0, 3, 1, 2).reshape(depth, ch, gp)
    out_shape = (
        jax.ShapeDtypeStruct(lb_raw.shape, F32),
        jax.ShapeDtypeStruct((depth, 1, gp), F32),
        jax.ShapeDtypeStruct((depth, 1, gp), F32),
        jax.ShapeDtypeStruct((depth, ch, gp), F32),
        jax.ShapeDtypeStruct((depth, ch, gp), F32),
    )
    return pl.pallas_call(_prep_body, out_shape=out_shape, name="prep")(
        lb_raw, row(a_re), row(a_im), row(ldt), chan_major(b_re), chan_major(b_im))


def _hgrn_body(*refs, nb, tl, c, heads, has_init, n_prev):
    q_ref, f_ref, i_ref, g_ref, lb_ref, nrm_ref = refs[:6]
    s0_ref = refs[6] if has_init else None
    (o_ref, s_ref, st_ref, of_ref, prod_ref, sums_ref, qc_ref, ke_ref, vb_ref, ce_ref, gs_ref
     ) = refs[6 + has_init + n_prev:]
    t = pl.program_id(1)

    @pl.when(t == 0)
    def _():
        def init(n, carry):
            for h in range(heads):
                if has_init:
                    st_ref[n, h] = s0_ref[n, h].T
                else:
                    st_ref[n, h] = jnp.zeros(st_ref.shape[2:], F32)
            return carry

        lax.fori_loop(0, nb, init, 0)

    row = lax.broadcasted_iota(jnp.int32, (c, HG_DK), 0)
    lag = row - lax.broadcasted_iota(jnp.int32, (c, HG_DK), 1)
    ones = jnp.ones((HG_DK, LANES), BF16)
    chunks_per_seq = tl // c
    n_steps = nb * chunks_per_seq
    head_cols = [slice(h * HG_DK, (h + 1) * HG_DK) for h in range(heads)]

    def chunk_rows(idx):
        return pl.ds(pl.multiple_of(idx * c, c), c)

    def prepare(idx):
        rows = chunk_rows(idx)
        for h, hs in enumerate(head_cols):
            lb = lb_ref[:, hs]
            fpre = f_ref[rows, hs]
            q = _silu(q_ref[rows, hs])
            forget = lb + (1.0 - lb) * jax.nn.sigmoid(fpre)
            kin = (1.0 - lb) * jax.nn.sigmoid(-fpre)

            cp = forget
            sh = 1
            while sh < c:
                cp = cp * jnp.where(row >= sh, pltpu.roll(cp, sh, 0), 1.0)
                sh *= 2
            sp = jnp.where(row < c - 1, pltpu.roll(forget, c - 1, 0), 1.0)
            sh = 1
            while sh < c:
                sp = sp * jnp.where(row < c - sh, pltpu.roll(sp, c - sh, 0), 1.0)
                sh *= 2

            kd = kin
            prods = [q * kd]
            for d in range(1, c):
                kd = pltpu.roll(kd, 1, 0) * forget
                prods.append(q * kd)
            prod_ref[h] = jnp.concatenate(prods, axis=0).astype(BF16)
            qc_ref[h] = (q * cp).astype(BF16)
            ke_ref[h] = (kin * sp).astype(BF16)
            vb_ref[h] = i_ref[rows, hs].astype(BF16)
            ce_ref[h] = cp[c - 1:c, :]
            gs_ref[h] = nrm_ref[:, hs] * _silu(g_ref[rows, hs])

    def finish(idx):
        n = idx // chunks_per_seq
        rows = chunk_rows(idx)
        sums_ref[...] = jnp.dot(prod_ref[...].reshape(heads * c * c, HG_DK), ones,
                                preferred_element_type=F32)
        inter = []
        for h in range(heads):
            st = st_ref[n, h]
            vb = vb_ref[h]
            inter.append(lax.dot_general(qc_ref[h], st.astype(BF16), (((1,), (1,)), ((), ())),
                                         preferred_element_type=F32))
            upd = lax.dot_general(vb, ke_ref[h], (((0,), (0,)), ((), ())),
                                  preferred_element_type=F32)
            st_ref[n, h] = st * ce_ref[h] + upd
        outs = []
        for h in range(heads):
            scores = jnp.zeros((c, HG_DK), F32)
            for d in range(c):
                scores = jnp.where(lag == d, sums_ref[pl.ds((h * c + d) * c, c), :], scores)
            outs.append(inter[h] + jnp.dot(scores[:, :c].astype(BF16), vb_ref[h],
                                           preferred_element_type=F32))
        for hs, h in zip(head_cols, range(heads)):
            of_ref[rows, hs] = outs[h] * _rms_scale(outs[h]) * gs_ref[h]

    def step(idx, carry):
        finish(idx - 1)
        prepare(idx)
        return carry

    prepare(0)
    lax.fori_loop(1, n_steps, step, 0)
    finish(n_steps - 1)
    o_ref[...] = of_ref[...].astype(BF16)

    @pl.when(t == pl.num_programs(1) - 1)
    def _():
        def fin(n, carry):
            for h in range(heads):
                s_ref[n, h] = st_ref[n, h].T
            return carry

        lax.fori_loop(0, nb, fin, 0)


def _layer_state_spec(shape, nb, layer):
    tail = (0,) * (len(shape) - 2)
    return pl.BlockSpec((None, nb) + tuple(shape[2:]), lambda b, t: (layer, b) + tail)


def _hgrn(cols, lb, nrm, s_init, prevs, *, layer, nseq, heads, row0, seq, nb, tl, c):
    depth = lb.shape[0]
    dk = HG_DK
    mix_w = heads * dk
    dv = mix_w // heads
    nt = seq // tl
    blk = nb * tl
    rb0 = row0 // blk
    sshape = (depth, nseq, heads, dk, dv)

    def colspec(group):
        return pl.BlockSpec((blk, mix_w), lambda b, t: (rb0 + b * nt + t, group))

    vec = pl.BlockSpec((None, 1, mix_w), lambda b, t: (layer, 0, 0))
    sspec = _layer_state_spec(sshape, nb, layer)
    args = [cols, cols, cols, cols, lb.reshape(depth, 1, mix_w), nrm.reshape(depth, 1, mix_w)]
    st_specs, st_args, aliases = _state_args(
        [sspec], None if s_init is None else (s_init,), prevs, len(args))
    return pl.pallas_call(
        functools.partial(_hgrn_body, nb=nb, tl=tl, c=c, heads=heads,
                          has_init=s_init is not None, n_prev=len(aliases)),
        grid=(nseq // nb, nt),
        in_specs=[colspec(0), colspec(1), colspec(2), colspec(3), vec, vec] + st_specs,
        out_specs=(pl.BlockSpec((blk, mix_w), lambda b, t: (rb0 + b * nt + t, 0)), sspec),
        out_shape=(jax.ShapeDtypeStruct((cols.shape[0], mix_w), BF16),
                   jax.ShapeDtypeStruct(sshape, F32)),
        scratch_shapes=[pltpu.VMEM((nb, heads, dv, dk), F32), pltpu.VMEM((blk, mix_w), F32),
                        pltpu.VMEM((heads, c * c, dk), BF16), pltpu.VMEM((heads * c * c, LANES), F32),
                        pltpu.VMEM((heads, c, dk), BF16),
                        pltpu.VMEM((heads, c, dk), BF16), pltpu.VMEM((heads, c, dv), BF16),
                        pltpu.VMEM((heads, 1, dk), F32), pltpu.VMEM((heads, c, dv), F32)],
        input_output_aliases=aliases,
        compiler_params=_cparams(("parallel", "arbitrary")),
        name="hgrn",
    )(*args, *st_args)


def _gelu_tanh(x):
    return 0.5 * x * (1.0 + jnp.tanh(math.sqrt(2.0 / math.pi) * (x + 0.044715 * (x * x * x))))


def _state_args(specs, inits, prevs, n_args):
    in_specs, args, aliases = [], [], {}
    if inits is not None:
        in_specs += list(specs)
        args += list(inits)
    for out_idx, prev in enumerate(prevs):
        if prev is not None:
            aliases[n_args + len(args)] = out_idx
            in_specs.append(pl.BlockSpec(memory_space=pl.ANY))
            args.append(prev)
    return in_specs, args, aliases


def _s5_body(*refs, nb, tl, has_init, n_prev):
    u_ref, wb_ref, wc_ref, abr_ref, abi_ref, d_ref = refs[:6]
    y_ref, xr_ref, xi_ref, sr_ref, si_ref = refs[6 + 2 * has_init + n_prev:]
    rows = nb * tl
    pitch = rows + SUBLANES
    nsb = wb_ref.shape[0]
    cw = wb_ref.shape[1]
    half = wb_ref.shape[2] // 2
    per = half // LANES
    slabs = nsb * per
    t = pl.program_id(1)

    @pl.when(t == 0)
    def _():
        if has_init:
            xr_ref[...] = refs[6][...]
            xi_ref[...] = refs[7][...]
        else:
            xr_ref[...] = jnp.zeros_like(xr_ref)
            xi_ref[...] = jnp.zeros_like(xi_ref)

    for sb in range(nsb):
        ub = u_ref[:, sb * cw:(sb + 1) * cw].astype(BF16)
        res = jnp.dot(ub, wb_ref[sb], preferred_element_type=F32)
        for j in range(per):
            base = (sb * per + j) * pitch
            sr_ref[pl.ds(base, rows), :] = res[:, j * LANES:(j + 1) * LANES]
            si_ref[pl.ds(base, rows), :] = res[:, half + j * LANES:half + (j + 1) * LANES]

    ar = abr_ref[...]
    ai = abi_ref[...]

    def seq_scan(n, carry):
        def step(tt, x):
            xr, xi = x
            at = pl.ds(n * tl + tt, slabs, stride=pitch)
            nxr = ar * xr - ai * xi + sr_ref[at, :]
            nxi = ar * xi + ai * xr + si_ref[at, :]
            sr_ref[at, :] = nxr
            si_ref[at, :] = nxi
            return nxr, nxi

        xr, xi = lax.fori_loop(0, tl, step, (xr_ref[n], xi_ref[n]))
        xr_ref[n] = xr
        xi_ref[n] = xi
        return carry

    lax.fori_loop(0, nb, seq_scan, 0)

    for sb in range(nsb):
        parts = [sr_ref[pl.ds((sb * per + j) * pitch, rows), :] for j in range(per)]
        parts += [si_ref[pl.ds((sb * per + j) * pitch, rows), :] for j in range(per)]
        xs = jnp.concatenate(parts, axis=1).astype(BF16)
        cs = slice(sb * cw, (sb + 1) * cw)
        y = jnp.dot(xs, wc_ref[sb], preferred_element_type=F32) + d_ref[:, cs] * u_ref[:, cs]
        y_ref[:, cs] = _gelu_tanh(y).astype(BF16)


def _s5(cols, wb, wc, ab_re, ab_im, d, inits, prevs, *, layer, depth, nseq, row0, col0, seq, nb, tl):
    slabs = ab_re.shape[0]
    nt = seq // tl
    blk = nb * tl
    rb0 = row0 // blk
    mix_w = d.shape[-1]
    cb = col0 // mix_w
    const3 = lambda a: pl.BlockSpec(a.shape, lambda b, t: (0, 0, 0))
    const2 = lambda a: pl.BlockSpec(a.shape, lambda b, t: (0, 0))
    sshape = (depth, nseq, slabs, LANES)
    xspec = _layer_state_spec(sshape, nb, layer)
    pitch = blk + SUBLANES
    args = [cols, wb, wc, ab_re, ab_im, d]
    st_specs, st_args, aliases = _state_args([xspec, xspec], inits, prevs, len(args))
    return pl.pallas_call(
        functools.partial(_s5_body, nb=nb, tl=tl, has_init=inits is not None, n_prev=len(aliases)),
        grid=(nseq // nb, nt),
        in_specs=[pl.BlockSpec((blk, mix_w), lambda b, t: (rb0 + b * nt + t, cb)),
                  const3(wb), const3(wc), const2(ab_re), const2(ab_im), const2(d)] + st_specs,
        out_specs=(pl.BlockSpec((blk, mix_w), lambda b, t: (rb0 + b * nt + t, 0)), xspec, xspec),
        out_shape=(jax.ShapeDtypeStruct((cols.shape[0], mix_w), BF16),
                   jax.ShapeDtypeStruct(sshape, F32),
                   jax.ShapeDtypeStruct(sshape, F32)),
        scratch_shapes=[pltpu.VMEM((slabs * pitch, LANES), F32),
                        pltpu.VMEM((slabs * pitch, LANES), F32)],
        input_output_aliases=aliases,
        compiler_params=_cparams(("parallel", "arbitrary")),
        name="s5",
    )(*args, *st_args)


def _s5_weights(bb_re, bb_im, c_re, c_im):
    ch, gp = bb_re.shape
    groups = c_re.shape[0]
    state = gp // groups
    nsb = groups // S5_SUPER
    eye = jnp.eye(S5_SUPER, dtype=F32)

    def in_w(bb):
        b4 = bb.reshape(ch, nsb, S5_SUPER, state)
        return jnp.einsum('csgp,gh->sgchp', b4, eye).reshape(nsb, S5_SUPER * ch, S5_SUPER * state)

    def out_w(c):
        c4 = c.reshape(nsb, S5_SUPER, ch, state)
        return jnp.einsum('sgcp,gh->sgphc', c4, eye).reshape(nsb, S5_SUPER * state, S5_SUPER * ch)

    wb = jnp.concatenate([in_w(bb_re), in_w(bb_im)], axis=2).astype(BF16)
    wc = jnp.concatenate([out_w(c_re), -out_w(c_im)], axis=1).astype(BF16)
    return wb, wc


def _mlproj_body(*refs, nb, tl, has_init, n_prev):
    x_ref, cw_ref, cb_ref, wq_ref, wk_ref, wv_ref, wg_ref, bg_ref = refs[:8]
    q_ref, k_ref, v_ref, g_ref, cn_ref, xx_ref = refs[8 + has_init + n_prev:]
    t = pl.program_id(1)
    width = x_ref.shape[1]
    halo = CONV_W - 1
    lo = SUBLANES - halo

    @pl.when(t == 0)
    def _():
        if has_init:
            xx_ref[:, lo:SUBLANES, :] = refs[8][...]
        else:
            xx_ref[:, lo:SUBLANES, :] = jnp.zeros((nb, halo, width), F32)

    @pl.when(t > 0)
    def _():
        xx_ref[:, lo:SUBLANES, :] = xx_ref[:, tl + lo:tl + SUBLANES, :]

    x = x_ref[...]
    xx_ref[:, SUBLANES:, :] = x.reshape(nb, tl, width)
    xc = jnp.zeros((nb, tl, width), F32) + cb_ref[...][None]
    for j in range(CONV_W):
        xc = xc + xx_ref[:, lo + j:lo + j + tl, :] * cw_ref[j:j + 1, :][None]
    cn_ref[...] = xx_ref[:, tl + lo:tl + SUBLANES, :]
    xc = _silu(xc).reshape(nb * tl, width).astype(BF16)
    xb = x.astype(BF16)

    dh = width // ML_HEADS
    gates = jnp.zeros((nb * tl, 2 * ML_HEADS), F32) + bg_ref[...]
    for h in range(ML_HEADS):
        hs = slice(h * dh, (h + 1) * dh)
        q = jnp.dot(xc[:, hs], wq_ref[h], preferred_element_type=F32)
        k = jnp.dot(xc[:, hs], wk_ref[h], preferred_element_type=F32)
        v = jnp.dot(xb[:, hs], wv_ref[h], preferred_element_type=F32)
        q_ref[:, hs] = q
        k_ref[:, hs] = k
        v_ref[:, hs] = v
        for part, val in enumerate((q, k, v)):
            w = wg_ref[part * width + h * dh:part * width + (h + 1) * dh, :]
            gates = gates + jnp.dot(val.astype(BF16), w, preferred_element_type=F32)
    g_ref[...] = gates


def _mlproj(cols, conv_w, conv_b, wq, wk, wv, wg, bg, init, prev, *, layer, depth, nseq, row0, col0,
            seq, nb, tl):
    width = conv_w.shape[1]
    halo = CONV_W - 1
    nt = seq // tl
    blk = nb * tl
    rb0 = row0 // blk
    cb = col0 // width
    ng = wg.shape[1]
    full = lambda a: pl.BlockSpec(a.shape, lambda b, t: (0,) * a.ndim)
    rowspec = lambda w: pl.BlockSpec((blk, w), lambda b, t: (b * nt + t, 0))
    sshape = (depth, nseq, halo, width)
    cspec = _layer_state_spec(sshape, nb, layer)
    args = [cols, conv_w, conv_b.reshape(1, width), wq, wk, wv, wg, bg.reshape(1, ng)]
    st_specs, st_args, aliases = _state_args(
        [cspec], None if init is None else (init,), (None, None, None, None, prev), len(args))
    return pl.pallas_call(
        functools.partial(_mlproj_body, nb=nb, tl=tl, has_init=init is not None,
                          n_prev=len(aliases)),
        grid=(nseq // nb, nt),
        in_specs=[pl.BlockSpec((blk, width), lambda b, t: (rb0 + b * nt + t, cb))]
                 + [full(a) for a in args[1:]] + st_specs,
        out_specs=(rowspec(width), rowspec(width), rowspec(width), rowspec(ng), cspec),
        out_shape=(jax.ShapeDtypeStruct((nseq * seq, width), F32),) * 3
                  + (jax.ShapeDtypeStruct((nseq * seq, ng), F32),
                     jax.ShapeDtypeStruct(sshape, F32)),
        scratch_shapes=[pltpu.VMEM((nb, tl + SUBLANES, width), F32)],
        input_output_aliases=aliases,
        compiler_params=_cparams(("parallel", "arbitrary")),
        name="mlproj",
    )(*args, *st_args)


def _mlrec_body(*refs, nb, tl, c, heads, unroll, has_init, n_prev):
    q_ref, k_ref, v_ref, g_ref, o_ref, nrm_ref = refs[:6]
    h_ref, c_ref, n_ref, m_ref, hf_ref = refs[6 + 3 * has_init + n_prev:]
    t = pl.program_id(1)
    dh = q_ref.shape[1] // heads

    @pl.when(t == 0)
    def _():
        if has_init:
            c_ref[...] = refs[6][...]
            n_ref[...] = refs[7][...]
            m_ref[...] = refs[8][...]
        else:
            c_ref[...] = jnp.zeros_like(c_ref)
            n_ref[...] = jnp.zeros_like(n_ref)
            m_ref[...] = jnp.zeros_like(m_ref)

    glane = lax.broadcasted_iota(jnp.int32, (c, 2 * heads), 1)
    r = lax.broadcasted_iota(jnp.int32, (c, c), 0)
    s = lax.broadcasted_iota(jnp.int32, (c, c), 1)
    causal = r >= s
    chunks_per_seq = tl // c
    k_scale = dh ** -0.5

    def chunk(idx, carry):
        n = idx // chunks_per_seq
        rows = pl.ds(pl.multiple_of(idx * c, c), c)
        gt = g_ref[rows, :]
        for h in range(heads):
            hs = slice(h * dh, (h + 1) * dh)
            q = q_ref[rows, hs]
            k = k_ref[rows, hs] * k_scale
            v = v_ref[rows, hs]
            ig_col = jnp.sum(jnp.where(glane == h, gt, 0.0), axis=1, keepdims=True)
            fp_col = jnp.sum(jnp.where(glane == h + heads, gt, 0.0), axis=1, keepdims=True)
            lf_col = jnp.minimum(fp_col, 0.0) - jnp.log1p(jnp.exp(-jnp.abs(fp_col)))
            lf_row = jnp.sum(jnp.where(r == s, lf_col, 0.0), axis=0, keepdims=True)
            ig_row = jnp.sum(jnp.where(r == s, ig_col, 0.0), axis=0, keepdims=True)
            b_col = jnp.sum(jnp.where(causal, lf_row, 0.0), axis=1, keepdims=True)
            b_row = jnp.sum(jnp.where(r <= s, lf_col, 0.0), axis=0, keepdims=True)

            m_prev = m_ref[n, h][:, 0:1]
            n_prev = n_ref[n, h]
            c_prev = c_ref[n, h]

            logw = jnp.where(causal, b_col - b_row + ig_row, NEG_BIG)
            m_t = jnp.maximum(b_col + m_prev, jnp.max(logw, axis=1, keepdims=True))
            w_prev = jnp.exp(b_col + m_prev - m_t)
            w = jnp.exp(logw - m_t)
            qb = q.astype(BF16)
            vb = v.astype(BF16)
            sc = lax.dot_general(qb, k.astype(BF16), (((1,), (1,)), ((), ())),
                                 preferred_element_type=F32) * w
            num = (w_prev * jnp.dot(qb, c_prev.astype(BF16), preferred_element_type=F32)
                   + jnp.dot(sc.astype(BF16), vb, preferred_element_type=F32))
            den = (w_prev * jnp.sum(q * n_prev, axis=1, keepdims=True)
                   + jnp.sum(sc, axis=1, keepdims=True))
            hval = num / jnp.maximum(jnp.abs(den), jnp.exp(-m_t))

            m_end = m_t[c - 1:c, :]
            b_end = b_col[c - 1:c, :]
            g_prev = jnp.exp(b_end + m_prev - m_end)
            w_in = jnp.exp(b_end - b_col + ig_col - m_end)
            kw = k * w_in
            c_ref[n, h] = g_prev * c_prev + lax.dot_general(
                kw.astype(BF16), vb, (((0,), (0,)), ((), ())), preferred_element_type=F32)
            n_ref[n, h] = g_prev * n_prev + jnp.sum(kw, axis=0, keepdims=True)
            m_ref[n, h] = jnp.broadcast_to(m_end, (1, LANES))
            hf_ref[rows, hs] = (hval * _rms_scale(hval) * nrm_ref[:, hs]
                                * jax.nn.sigmoid(o_ref[rows, hs]))
        return carry

    lax.fori_loop(0, nb * chunks_per_seq, chunk, 0, unroll=unroll)
    h_ref[...] = hf_ref[...].astype(BF16)


def _mlrec(q, k, v, gates, cols, nrm, inits, prevs, *, layer, depth, nseq, heads, row0, col0,
           nb, tl, c, unroll):
    width = q.shape[1]
    dh = width // heads
    seq = q.shape[0] // nseq
    nt = seq // tl
    blk = nb * tl
    rb0 = row0 // blk
    cb = col0 // width
    ng = gates.shape[1]
    qspec = pl.BlockSpec((blk, width), lambda b, t: (b * nt + t, 0))
    ospec = pl.BlockSpec((blk, width), lambda b, t: (rb0 + b * nt + t, 0))
    shapes = ((depth, nseq, heads, dh, dh), (depth, nseq, heads, 1, dh), (depth, nseq, heads, 1, LANES))
    sspecs = [_layer_state_spec(sh, nb, layer) for sh in shapes]
    args = [q, k, v, gates, cols, nrm.reshape(1, width)]
    st_specs, st_args, aliases = _state_args(sspecs, inits, prevs, len(args))
    return pl.pallas_call(
        functools.partial(_mlrec_body, nb=nb, tl=tl, c=c, heads=heads, unroll=unroll,
                          has_init=inits is not None, n_prev=len(aliases)),
        grid=(nseq // nb, nt),
        in_specs=[qspec, qspec, qspec,
                  pl.BlockSpec((blk, ng), lambda b, t: (b * nt + t, 0)),
                  pl.BlockSpec((blk, width), lambda b, t: (rb0 + b * nt + t, cb)),
                  pl.BlockSpec((1, width), lambda b, t: (0, 0))] + st_specs,
        out_specs=(ospec, *sspecs),
        out_shape=(jax.ShapeDtypeStruct((cols.shape[0], width), BF16),)
                  + tuple(jax.ShapeDtypeStruct(sh, F32) for sh in shapes),
        scratch_shapes=[pltpu.VMEM((blk, width), F32)],
        input_output_aliases=aliases,
        compiler_params=_cparams(("parallel", "arbitrary")),
        name="mlrec",
    )(*args, *st_args)


def _merge_body(x_ref, oa_ref, ob_ref, oc_ref, ga_ref, gb_ref, gc_ref, wa_ref, wba_ref, wbb_ref,
                wc_ref, wo_ref, g_ref, o_ref, acc_ref):
    j = pl.program_id(1)

    @pl.when(j == 0)
    def _():
        acc_ref[...] = jnp.zeros_like(acc_ref)

    yb = ob_ref[...]
    br_a = jnp.dot(oa_ref[...], wa_ref[...], preferred_element_type=F32)
    br_b = (jnp.dot(yb, wba_ref[...], preferred_element_type=F32)
            * jax.nn.sigmoid(jnp.dot(yb, wbb_ref[...], preferred_element_type=F32)))
    br_c = jnp.dot(oc_ref[...], wc_ref[...], preferred_element_type=F32)
    merged = (jax.nn.sigmoid(ga_ref[...]) * br_a + jax.nn.sigmoid(gb_ref[...]) * br_b
              + jax.nn.sigmoid(gc_ref[...]) * br_c)
    acc_ref[...] += jnp.dot(merged.astype(BF16), wo_ref[...], preferred_element_type=F32)

    @pl.when(j == pl.num_programs(1) - 1)
    def _():
        y = acc_ref[...]
        o_ref[...] = x_ref[...] + y * _rms_scale(y) * g_ref[...]


def _merge(x, oa, ob, oc, cols, w_a, w_ba, w_bb, w_c, w_out, gains, k_gain, *, layer, col0,
           tm=512, tn=512):
    m, d = x.shape
    mix_w = oa.shape[1]
    nj = d // tn
    gb0 = col0 // tn
    act = pl.BlockSpec((tm, mix_w), lambda i, j: (i, 0))
    wspec = pl.BlockSpec((None, mix_w, tn), lambda i, j: (layer, 0, j))

    def gate(branch):
        return pl.BlockSpec((tm, tn), lambda i, j: (i, gb0 + branch * nj + j))

    return pl.pallas_call(
        _merge_body,
        grid=(m // tm, nj),
        in_specs=[pl.BlockSpec((tm, d), lambda i, j: (i, 0)), act, act, act,
                  gate(0), gate(1), gate(2), wspec, wspec, wspec, wspec,
                  pl.BlockSpec((None, tn, d), lambda i, j: (layer, j, 0)),
                  _gain_spec(gains, layer, k_gain)],
        out_specs=pl.BlockSpec((tm, d), lambda i, j: (i, 0)),
        out_shape=jax.ShapeDtypeStruct((m, d), F32),
        scratch_shapes=[pltpu.VMEM((tm, d), F32)],
        compiler_params=_cparams(("parallel", "arbitrary")),
        name="merge",
    )(x, oa, ob, oc, cols, cols, cols, w_a, w_ba, w_bb, w_c, w_out, gains)


def kernel(x_prompt, x_sample, state_hgrn, state_s5_re, state_s5_im, state_mlstm_c, state_mlstm_n, state_mlstm_m, state_mlstm_conv, norm_gains, w_ffn1_up, w_ffn1_down, w_in, hgrn_lower_bounds, hgrn_norm, w_hgrn_out, s5_a_re, s5_a_im, s5_log_dt, s5_b_re, s5_b_im, s5_c_re, s5_c_im, s5_d, w_s5_glu_a, w_s5_glu_b, mlstm_conv_w, mlstm_conv_b, mlstm_wq, mlstm_wk, mlstm_wv, mlstm_w_gates, mlstm_b_gates, mlstm_norm, w_mlstm_out, w_out, w_ffn2_up, w_ffn2_down):
    batch, seq, d_model = x_prompt.shape
    dec_batch, dec_seq, _ = x_sample.shape
    depth = w_in.shape[0]
    mix_w = w_hgrn_out.shape[1]
    n_prompt = batch * seq
    heads_hg = state_hgrn.shape[2]
    groups, state = s5_a_re.shape[1:]
    slabs = groups * state // LANES

    x = jnp.concatenate([x_prompt.reshape(n_prompt, d_model),
                         x_sample.reshape(dec_batch * dec_seq, d_model)], axis=0)

    lb_all, ab_re, ab_im, bb_re, bb_im = _prep(hgrn_lower_bounds, s5_a_re, s5_a_im, s5_log_dt,
                                               s5_b_re, s5_b_im)

    col_su, col_mx, col_mo, col_gz = 4 * mix_w, 5 * mix_w, 6 * mix_w, 7 * mix_w
    dh = mix_w // ML_HEADS

    sample_init = dict(
        hg=state_hgrn,
        s5=(state_s5_re.reshape(depth, dec_batch, slabs, LANES),
            state_s5_im.reshape(depth, dec_batch, slabs, LANES)),
        conv=state_mlstm_conv,
        ml=(state_mlstm_c, state_mlstm_n.reshape(depth, dec_batch, ML_HEADS, 1, dh),
            jnp.broadcast_to(state_mlstm_m[..., None, None], (depth, dec_batch, ML_HEADS, 1, LANES))))
    groups_cfg = (
        dict(row0=0, nseq=batch, seq=seq, init=dict(hg=None, s5=None, conv=None, ml=None),
             hg=dict(nb=1, tl=512, c=16), s5=dict(nb=1, tl=128), mp=dict(nb=1, tl=256),
             mr=dict(nb=1, tl=256, c=64, unroll=2)),
        dict(row0=n_prompt, nseq=dec_batch, seq=dec_seq, init=sample_init,
             hg=dict(nb=8, tl=dec_seq, c=dec_seq), s5=dict(nb=16, tl=dec_seq),
             mp=dict(nb=16, tl=dec_seq), mr=dict(nb=4, tl=dec_seq, c=dec_seq, unroll=2)),
    )
    new = [dict(hg=None, s5=(None, None), conv=None, ml=(None, None, None)) for _ in groups_cfg]

    gains = norm_gains.reshape(depth * N_GAINS, 1, d_model)
    bf = lambda w: w.astype(BF16)
    w_ffn1_up, w_ffn1_down, w_ffn2_up, w_ffn2_down = map(bf, (w_ffn1_up, w_ffn1_down, w_ffn2_up, w_ffn2_down))
    w_in = bf(w_in)
    w_hgrn_out, w_s5_glu_a, w_s5_glu_b, w_mlstm_out, w_out = map(
        bf, (w_hgrn_out, w_s5_glu_a, w_s5_glu_b, w_mlstm_out, w_out))

    for l in range(depth):
        x = _ffn(x, gains, 0, 1, w_ffn1_up, w_ffn1_down, layer=l)
        cols = _inproj(x, gains, 2, w_in, layer=l)

        wb, wc = _s5_weights(bb_re[l], bb_im[l], s5_c_re[l], s5_c_im[l])
        abr = ab_re[l].reshape(slabs, LANES)
        abi = ab_im[l].reshape(slabs, LANES)
        s5d = s5_d[l].reshape(1, mix_w)
        wq, wk, wv = (w[l].astype(BF16) for w in (mlstm_wq, mlstm_wk, mlstm_wv))
        wg = mlstm_w_gates[l].astype(BF16)

        oa = ob = oc = None
        for cfg, st in zip(groups_cfg, new):
            common = dict(layer=l, nseq=cfg['nseq'], row0=cfg['row0'])
            init = cfg['init']
            oa, st['hg'] = _hgrn(cols, lb_all, hgrn_norm, init['hg'], (oa, st['hg']),
                                 heads=heads_hg, seq=cfg['seq'], **common, **cfg['hg'])
            ob, *st['s5'] = _s5(cols, wb, wc, abr, abi, s5d, init['s5'], (ob, *st['s5']),
                                depth=depth, col0=col_su, seq=cfg['seq'], **common, **cfg['s5'])
            q, k, v, gates, st['conv'] = _mlproj(
                cols, mlstm_conv_w[l], mlstm_conv_b[l], wq, wk, wv, wg, mlstm_b_gates[l],
                init['conv'], st['conv'], depth=depth, col0=col_mx, seq=cfg['seq'], **common,
                **cfg['mp'])
            oc, *st['ml'] = _mlrec(q, k, v, gates, cols, mlstm_norm[l], init['ml'], (oc, *st['ml']),
                                   depth=depth, heads=ML_HEADS, col0=col_mo, **common, **cfg['mr'])

        x = _merge(x, oa, ob, oc, cols, w_hgrn_out, w_s5_glu_a, w_s5_glu_b, w_mlstm_out, w_out,
                   gains, 3, layer=l, col0=col_gz)
        x = _ffn(x, gains, 4, 5, w_ffn2_up, w_ffn2_down, layer=l)

    y_prompt = x[:n_prompt].reshape(batch, seq, d_model)
    y_sample = x[n_prompt:].reshape(dec_batch, dec_seq, d_model)
    states = ()
    for cfg, st in zip(groups_cfg, new):
        nseq = cfg['nseq']
        c_new, n_new, m_new = st['ml']
        states += (st['hg'],
                   st['s5'][0].reshape(depth, nseq, groups, state),
                   st['s5'][1].reshape(depth, nseq, groups, state),
                   c_new, n_new.reshape(depth, nseq, ML_HEADS, dh), m_new[:, :, :, 0, 0], st['conv'])
    return (y_prompt, y_sample) + states
```

```python
import functools
import math

import jax
import jax.numpy as jnp
from jax import lax
from jax.experimental import pallas as pl
from jax.experimental.pallas import tpu as pltpu

F32 = jnp.float32
BF16 = jnp.bfloat16

EPS = 1e-6
NEG_BIG = -1e30
S5_DT_MIN_CLAMP = -1e-4

HG_DK = 128
S5_CH = 16
S5_SUPER = 8
ML_HEADS = 4
CONV_W = 4
N_GAINS = 6

LANES = 128
SUBLANES = 8
VMEM_LIMIT = 56 * 1024 * 1024


def _cparams(sem):
    return pltpu.CompilerParams(dimension_semantics=sem, vmem_limit_bytes=VMEM_LIMIT)


def _rms_scale(y):
    return lax.rsqrt(jnp.mean(y * y, axis=-1, keepdims=True) + EPS)


def _silu(x):
    return x * jax.nn.sigmoid(x)


def _ffn_body(x_ref, gpre_ref, gpost_ref, wa_ref, wb_ref, wd_ref, o_ref, h_ref, acc_ref):
    j = pl.program_id(1)

    @pl.when(j == 0)
    def _():
        x = x_ref[...]
        h_ref[...] = (x * _rms_scale(x) * gpre_ref[...]).astype(BF16)
        acc_ref[...] = jnp.zeros_like(acc_ref)

    h = h_ref[...]
    a = jnp.dot(h, wa_ref[...], preferred_element_type=F32)
    b = jnp.dot(h, wb_ref[...], preferred_element_type=F32)
    act = (_silu(a) * b).astype(BF16)
    acc_ref[...] += jnp.dot(act, wd_ref[...], preferred_element_type=F32)

    @pl.when(j == pl.num_programs(1) - 1)
    def _():
        y = acc_ref[...]
        o_ref[...] = x_ref[...] + 0.5 * (y * _rms_scale(y) * gpost_ref[...])


def _gain_spec(gains, layer, k):
    return pl.BlockSpec((None, 1, gains.shape[-1]), lambda i, j: (layer * N_GAINS + k, 0, 0))


def _ffn(x, gains, k_pre, k_post, w_up, w_down, *, layer, tm=512, tf=512):
    m, d = x.shape
    d_ff = w_down.shape[1]
    nf = d_ff // tf
    return pl.pallas_call(
        _ffn_body,
        grid=(m // tm, nf),
        in_specs=[
            pl.BlockSpec((tm, d), lambda i, j: (i, 0)),
            _gain_spec(gains, layer, k_pre),
            _gain_spec(gains, layer, k_post),
            pl.BlockSpec((None, d, tf), lambda i, j: (layer, 0, j)),
            pl.BlockSpec((None, d, tf), lambda i, j: (layer, 0, j + nf)),
            pl.BlockSpec((None, tf, d), lambda i, j: (layer, j, 0)),
        ],
        out_specs=pl.BlockSpec((tm, d), lambda i, j: (i, 0)),
        out_shape=jax.ShapeDtypeStruct((m, d), F32),
        scratch_shapes=[pltpu.VMEM((tm, d), BF16), pltpu.VMEM((tm, d), F32)],
        compiler_params=_cparams(("parallel", "arbitrary")),
        name="ffn",
    )(x, gains, gains, w_up, w_up, w_down)


def _inproj_body(x_ref, g_ref, w_ref, o_ref, h_ref):
    @pl.when(pl.program_id(1) == 0)
    def _():
        x = x_ref[...]
        h_ref[...] = (x * _rms_scale(x) * g_ref[...]).astype(BF16)

    o_ref[...] = jnp.dot(h_ref[...], w_ref[...], preferred_element_type=F32)


def _inproj(x, gains, k_gain, w, *, layer, tm=1024, tn=1024):
    m, d = x.shape
    n = w.shape[2]
    return pl.pallas_call(
        _inproj_body,
        grid=(m // tm, n // tn),
        in_specs=[
            pl.BlockSpec((tm, d), lambda i, j: (i, 0)),
            _gain_spec(gains, layer, k_gain),
            pl.BlockSpec((None, d, tn), lambda i, j: (layer, 0, j)),
        ],
        out_specs=pl.BlockSpec((tm, tn), lambda i, j: (i, j)),
        out_shape=jax.ShapeDtypeStruct((m, n), F32),
        scratch_shapes=[pltpu.VMEM((tm, d), BF16)],
        compiler_params=_cparams(("parallel", "arbitrary")),
        name="inproj",
    )(x, gains, w)


def _prep_body(lbraw_ref, are_ref, aim_ref, ldt_ref, bre_ref, bim_ref,
               lb_ref, abr_ref, abi_ref, bbr_ref, bbi_ref):
    depth = lbraw_ref.shape[0]
    raw = lbraw_ref[...]
    e = jnp.exp(raw - jnp.max(raw, axis=0, keepdims=True))
    lbs = e / jnp.sum(e, axis=0, keepdims=True)
    run = jnp.zeros_like(lbs[0:1])
    for l in range(depth):
        run = run + lbs[l:l + 1]
        lb_ref[l:l + 1, :] = run - lbs[0:1]

    for l in range(depth):
        dt = jnp.exp(ldt_ref[l])
        lam_re = jnp.minimum(are_ref[l], S5_DT_MIN_CLAMP)
        lam_im = aim_ref[l]
        mag = jnp.exp(lam_re * dt)
        ab_re = mag * jnp.cos(lam_im * dt)
        ab_im = mag * jnp.sin(lam_im * dt)
        inv = 1.0 / (lam_re * lam_re + lam_im * lam_im)
        f_re = ((ab_re - 1.0) * lam_re + ab_im * lam_im) * inv
        f_im = (ab_im * lam_re - (ab_re - 1.0) * lam_im) * inv
        abr_ref[l] = ab_re
        abi_ref[l] = ab_im
        b_re = bre_ref[l]
        b_im = bim_ref[l]
        bbr_ref[l] = f_re * b_re - f_im * b_im
        bbi_ref[l] = f_re * b_im + f_im * b_re


def _prep(lb_raw, a_re, a_im, log_dt, b_re, b_im):
    depth, groups, state = a_re.shape
    gp = groups * state
    ch = b_re.shape[-1]
    row = lambda a: a.reshape(depth, 1, gp)
    ldt = jnp.broadcast_to(log_dt[:, :, None], (depth, groups, state))
    chan_major = lambda b: b.transpose(0, 3, 1, 2).reshape(depth, ch, gp)
    out_shape = (
        jax.ShapeDtypeStruct(lb_raw.shape, F32),
        jax.ShapeDtypeStruct((depth, 1, gp), F32),
        jax.ShapeDtypeStruct((depth, 1, gp), F32),
        jax.ShapeDtypeStruct((depth, ch, gp), F32),
        jax.ShapeDtypeStruct((depth, ch, gp), F32),
    )
    return pl.pallas_call(_prep_body, out_shape=out_shape, name="prep")(
        lb_raw, row(a_re), row(a_im), row(ldt), chan_major(b_re), chan_major(b_im))


def _hgrn_body(*refs, nb, tl, c, heads, has_init, n_prev):
    q_ref, f_ref, i_ref, g_ref, lb_ref, nrm_ref = refs[:6]
    s0_ref = refs[6] if has_init else None
    (o_ref, s_ref, st_ref, of_ref, prod_ref, sums_ref, qc_ref, ke_ref, vb_ref, ce_ref, gs_ref
     ) = refs[6 + has_init + n_prev:]
    t = pl.program_id(1)

    @pl.when(t == 0)
    def _():
        def init(n, carry):
            for h in range(heads):
                if has_init:
                    st_ref[n, h] = s0_ref[n, h].T
                else:
                    st_ref[n, h] = jnp.zeros(st_ref.shape[2:], F32)
            return carry

        lax.fori_loop(0, nb, init, 0)

    row = lax.broadcasted_iota(jnp.int32, (c, HG_DK), 0)
    lag = row - lax.broadcasted_iota(jnp.int32, (c, HG_DK), 1)
    ones = jnp.ones((HG_DK, LANES), BF16)
    chunks_per_seq = tl // c
    n_steps = nb * chunks_per_seq
    head_cols = [slice(h * HG_DK, (h + 1) * HG_DK) for h in range(heads)]

    def chunk_rows(idx):
        return pl.ds(pl.multiple_of(idx * c, c), c)

    def prepare(idx):
        rows = chunk_rows(idx)
        for h, hs in enumerate(head_cols):
            lb = lb_ref[:, hs]
            fpre = f_ref[rows, hs]
            q = _silu(q_ref[rows, hs])
            forget = lb + (1.0 - lb) * jax.nn.sigmoid(fpre)
            kin = (1.0 - lb) * jax.nn.sigmoid(-fpre)

            cp = forget
            sh = 1
            while sh < c:
                cp = cp * jnp.where(row >= sh, pltpu.roll(cp, sh, 0), 1.0)
                sh *= 2
            sp = jnp.where(row < c - 1, pltpu.roll(forget, c - 1, 0), 1.0)
            sh = 1
            while sh < c:
                sp = sp * jnp.where(row < c - sh, pltpu.roll(sp, c - sh, 0), 1.0)
                sh *= 2

            kd = kin
            prods = [q * kd]
            for d in range(1, c):
                kd = pltpu.roll(kd, 1, 0) * forget
                prods.append(q * kd)
            prod_ref[h] = jnp.concatenate(prods, axis=0).astype(BF16)
            qc_ref[h] = (q * cp).astype(BF16)
            ke_ref[h] = (kin * sp).astype(BF16)
            vb_ref[h] = i_ref[rows, hs].astype(BF16)
            ce_ref[h] = cp[c - 1:c, :]
            gs_ref[h] = nrm_ref[:, hs] * _silu(g_ref[rows, hs])

    def finish(idx):
        n = idx // chunks_per_seq
        rows = chunk_rows(idx)
        sums_ref[...] = jnp.dot(prod_ref[...].reshape(heads * c * c, HG_DK), ones,
                                preferred_element_type=F32)
        inter = []
        for h in range(heads):
            st = st_ref[n, h]
            vb = vb_ref[h]
            inter.append(lax.dot_general(qc_ref[h], st.astype(BF16), (((1,), (1,)), ((), ())),
                                         preferred_element_type=F32))
            upd = lax.dot_general(vb, ke_ref[h], (((0,), (0,)), ((), ())),
                                  preferred_element_type=F32)
            st_ref[n, h] = st * ce_ref[h] + upd
        outs = []
        for h in range(heads):
            scores = jnp.zeros((c, HG_DK), F32)
            for d in range(c):
                scores = jnp.where(lag == d, sums_ref[pl.ds((h * c + d) * c, c), :], scores)
            outs.append(inter[h] + jnp.dot(scores[:, :c].astype(BF16), vb_ref[h],
                                           preferred_element_type=F32))
        for hs, h in zip(head_cols, range(heads)):
            of_ref[rows, hs] = outs[h] * _rms_scale(outs[h]) * gs_ref[h]

    def step(idx, carry):
        finish(idx - 1)
        prepare(idx)
        return carry

    prepare(0)
    lax.fori_loop(1, n_steps, step, 0, unroll=2)
    finish(n_steps - 1)
    o_ref[...] = of_ref[...].astype(BF16)

    @pl.when(t == pl.num_programs(1) - 1)
    def _():
        def fin(n, carry):
            for h in range(heads):
                s_ref[n, h] = st_ref[n, h].T
            return carry

        lax.fori_loop(0, nb, fin, 0)


def _layer_state_spec(shape, nb, layer):
    tail = (0,) * (len(shape) - 2)
    return pl.BlockSpec((None, nb) + tuple(shape[2:]), lambda b, t: (layer, b) + tail)


def _hgrn(cols, lb, nrm, s_init, prevs, *, layer, nseq, heads, row0, seq, nb, tl, c):
    depth = lb.shape[0]
    dk = HG_DK
    mix_w = heads * dk
    dv = mix_w // heads
    nt = seq // tl
    blk = nb * tl
    rb0 = row0 // blk
    sshape = (depth, nseq, heads, dk, dv)

    def colspec(group):
        return pl.BlockSpec((blk, mix_w), lambda b, t: (rb0 + b * nt + t, group))

    vec = pl.BlockSpec((None, 1, mix_w), lambda b, t: (layer, 0, 0))
    sspec = _layer_state_spec(sshape, nb, layer)
    args = [cols, cols, cols, cols, lb.reshape(depth, 1, mix_w), nrm.reshape(depth, 1, mix_w)]
    st_specs, st_args, aliases = _state_args(
        [sspec], None if s_init is None else (s_init,), prevs, len(args))
    return pl.pallas_call(
        functools.partial(_hgrn_body, nb=nb, tl=tl, c=c, heads=heads,
                          has_init=s_init is not None, n_prev=len(aliases)),
        grid=(nseq // nb, nt),
        in_specs=[colspec(0), colspec(1), colspec(2), colspec(3), vec, vec] + st_specs,
        out_specs=(pl.BlockSpec((blk, mix_w), lambda b, t: (rb0 + b * nt + t, 0)), sspec),
        out_shape=(jax.ShapeDtypeStruct((cols.shape[0], mix_w), BF16),
                   jax.ShapeDtypeStruct(sshape, F32)),
        scratch_shapes=[pltpu.VMEM((nb, heads, dv, dk), F32), pltpu.VMEM((blk, mix_w), F32),
                        pltpu.VMEM((heads, c * c, dk), BF16), pltpu.VMEM((heads * c * c, LANES), F32),
                        pltpu.VMEM((heads, c, dk), BF16),
                        pltpu.VMEM((heads, c, dk), BF16), pltpu.VMEM((heads, c, dv), BF16),
                        pltpu.VMEM((heads, 1, dk), F32), pltpu.VMEM((heads, c, dv), F32)],
        input_output_aliases=aliases,
        compiler_params=_cparams(("parallel", "arbitrary")),
        name="hgrn",
    )(*args, *st_args)


def _gelu_tanh(x):
    return 0.5 * x * (1.0 + jnp.tanh(math.sqrt(2.0 / math.pi) * (x + 0.044715 * (x * x * x))))


def _state_args(specs, inits, prevs, n_args):
    in_specs, args, aliases = [], [], {}
    if inits is not None:
        in_specs += list(specs)
        args += list(inits)
    for out_idx, prev in enumerate(prevs):
        if prev is not None:
            aliases[n_args + len(args)] = out_idx
            in_specs.append(pl.BlockSpec(memory_space=pl.ANY))
            args.append(prev)
    return in_specs, args, aliases


def _s5_body(*refs, nb, tl, has_init, n_prev):
    u_ref, wb_ref, wc_ref, abr_ref, abi_ref, d_ref = refs[:6]
    y_ref, xr_ref, xi_ref, sr_ref, si_ref = refs[6 + 2 * has_init + n_prev:]
    rows = nb * tl
    pitch = rows + SUBLANES
    nsb = wb_ref.shape[0]
    cw = wb_ref.shape[1]
    half = wb_ref.shape[2] // 2
    per = half // LANES
    slabs = nsb * per
    t = pl.program_id(1)

    @pl.when(t == 0)
    def _():
        if has_init:
            xr_ref[...] = refs[6][...]
            xi_ref[...] = refs[7][...]
        else:
            xr_ref[...] = jnp.zeros_like(xr_ref)
            xi_ref[...] = jnp.zeros_like(xi_ref)

    for sb in range(nsb):
        ub = u_ref[:, sb * cw:(sb + 1) * cw].astype(BF16)
        res = jnp.dot(ub, wb_ref[sb], preferred_element_type=F32)
        for j in range(per):
            base = (sb * per + j) * pitch
            sr_ref[pl.ds(base, rows), :] = res[:, j * LANES:(j + 1) * LANES]
            si_ref[pl.ds(base, rows), :] = res[:, half + j * LANES:half + (j + 1) * LANES]

    ar = abr_ref[...]
    ai = abi_ref[...]

    def advance(n, tt, xr, xi):
        at = pl.ds(n * tl + tt, slabs, stride=pitch)
        nxr = ar * xr - ai * xi + sr_ref[at, :]
        nxi = ar * xi + ai * xr + si_ref[at, :]
        sr_ref[at, :] = nxr
        si_ref[at, :] = nxi
        return nxr, nxi

    par = 2 if nb % 2 == 0 else 1

    def group_scan(p, carry):
        seqs = [par * p + i for i in range(par)]

        def step(tt, x):
            out = ()
            for i, n in enumerate(seqs):
                out += advance(n, tt, x[2 * i], x[2 * i + 1])
            return out

        x0 = ()
        for n in seqs:
            x0 += (xr_ref[n], xi_ref[n])
        x = lax.fori_loop(0, tl, step, x0, unroll=SUBLANES)
        for i, n in enumerate(seqs):
            xr_ref[n] = x[2 * i]
            xi_ref[n] = x[2 * i + 1]
        return carry

    lax.fori_loop(0, nb // par, group_scan, 0)

    for sb in range(nsb):
        parts = [sr_ref[pl.ds((sb * per + j) * pitch, rows), :] for j in range(per)]
        parts += [si_ref[pl.ds((sb * per + j) * pitch, rows), :] for j in range(per)]
        xs = jnp.concatenate(parts, axis=1).astype(BF16)
        cs = slice(sb * cw, (sb + 1) * cw)
        y = jnp.dot(xs, wc_ref[sb], preferred_element_type=F32) + d_ref[:, cs] * u_ref[:, cs]
        y_ref[:, cs] = _gelu_tanh(y).astype(BF16)


def _s5(cols, wb, wc, ab_re, ab_im, d, inits, prevs, *, layer, depth, nseq, row0, col0, seq, nb, tl):
    slabs = ab_re.shape[0]
    nt = seq // tl
    blk = nb * tl
    rb0 = row0 // blk
    mix_w = d.shape[-1]
    cb = col0 // mix_w
    const3 = lambda a: pl.BlockSpec(a.shape, lambda b, t: (0, 0, 0))
    const2 = lambda a: pl.BlockSpec(a.shape, lambda b, t: (0, 0))
    sshape = (depth, nseq, slabs, LANES)
    xspec = _layer_state_spec(sshape, nb, layer)
    pitch = blk + SUBLANES
    args = [cols, wb, wc, ab_re, ab_im, d]
    st_specs, st_args, aliases = _state_args([xspec, xspec], inits, prevs, len(args))
    return pl.pallas_call(
        functools.partial(_s5_body, nb=nb, tl=tl, has_init=inits is not None, n_prev=len(aliases)),
        grid=(nseq // nb, nt),
        in_specs=[pl.BlockSpec((blk, mix_w), lambda b, t: (rb0 + b * nt + t, cb)),
                  const3(wb), const3(wc), const2(ab_re), const2(ab_im), const2(d)] + st_specs,
        out_specs=(pl.BlockSpec((blk, mix_w), lambda b, t: (rb0 + b * nt + t, 0)), xspec, xspec),
        out_shape=(jax.ShapeDtypeStruct((cols.shape[0], mix_w), BF16),
                   jax.ShapeDtypeStruct(sshape, F32),
                   jax.ShapeDtypeStruct(sshape, F32)),
        scratch_shapes=[pltpu.VMEM((slabs * pitch, LANES), F32),
                        pltpu.VMEM((slabs * pitch, LANES), F32)],
        input_output_aliases=aliases,
        compiler_params=_cparams(("parallel", "arbitrary")),
        name="s5",
    )(*args, *st_args)


def _s5_weights(bb_re, bb_im, c_re, c_im):
    ch, gp = bb_re.shape
    groups = c_re.shape[0]
    state = gp // groups
    nsb = groups // S5_SUPER
    eye = jnp.eye(S5_SUPER, dtype=F32)

    def in_w(bb):
        b4 = bb.reshape(ch, nsb, S5_SUPER, state)
        return jnp.einsum('csgp,gh->sgchp', b4, eye).reshape(nsb, S5_SUPER * ch, S5_SUPER * state)

    def out_w(c):
        c4 = c.reshape(nsb, S5_SUPER, ch, state)
        return jnp.einsum('sgcp,gh->sgphc', c4, eye).reshape(nsb, S5_SUPER * state, S5_SUPER * ch)

    wb = jnp.concatenate([in_w(bb_re), in_w(bb_im)], axis=2).astype(BF16)
    wc = jnp.concatenate([out_w(c_re), -out_w(c_im)], axis=1).astype(BF16)
    return wb, wc


def _mlproj_body(*refs, nb, tl, has_init, n_prev):
    x_ref, cw_ref, cb_ref, wq_ref, wk_ref, wv_ref, wg_ref, bg_ref = refs[:8]
    q_ref, k_ref, v_ref, g_ref, cn_ref, xx_ref = refs[8 + has_init + n_prev:]
    t = pl.program_id(1)
    width = x_ref.shape[1]
    halo = CONV_W - 1
    lo = SUBLANES - halo

    @pl.when(t == 0)
    def _():
        if has_init:
            xx_ref[:, lo:SUBLANES, :] = refs[8][...]
        else:
            xx_ref[:, lo:SUBLANES, :] = jnp.zeros((nb, halo, width), F32)

    @pl.when(t > 0)
    def _():
        xx_ref[:, lo:SUBLANES, :] = xx_ref[:, tl + lo:tl + SUBLANES, :]

    x = x_ref[...]
    xx_ref[:, SUBLANES:, :] = x.reshape(nb, tl, width)
    xc = jnp.zeros((nb, tl, width), F32) + cb_ref[...][None]
    for j in range(CONV_W):
        xc = xc + xx_ref[:, lo + j:lo + j + tl, :] * cw_ref[j:j + 1, :][None]
    cn_ref[...] = xx_ref[:, tl + lo:tl + SUBLANES, :]
    xc = _silu(xc).reshape(nb * tl, width).astype(BF16)
    xb = x.astype(BF16)

    dh = width // ML_HEADS
    gates = jnp.zeros((nb * tl, 2 * ML_HEADS), F32) + bg_ref[...]
    for h in range(ML_HEADS):
        hs = slice(h * dh, (h + 1) * dh)
        q = jnp.dot(xc[:, hs], wq_ref[h], preferred_element_type=F32)
        k = jnp.dot(xc[:, hs], wk_ref[h], preferred_element_type=F32)
        v = jnp.dot(xb[:, hs], wv_ref[h], preferred_element_type=F32)
        q_ref[:, hs] = q
        k_ref[:, hs] = k
        v_ref[:, hs] = v
        for part, val in enumerate((q, k, v)):
            w = wg_ref[part * width + h * dh:part * width + (h + 1) * dh, :]
            gates = gates + jnp.dot(val.astype(BF16), w, preferred_element_type=F32)
    g_ref[...] = gates


def _mlproj(cols, conv_w, conv_b, wq, wk, wv, wg, bg, init, prev, *, layer, depth, nseq, row0, col0,
            seq, nb, tl):
    width = conv_w.shape[1]
    halo = CONV_W - 1
    nt = seq // tl
    blk = nb * tl
    rb0 = row0 // blk
    cb = col0 // width
    ng = wg.shape[1]
    full = lambda a: pl.BlockSpec(a.shape, lambda b, t: (0,) * a.ndim)
    rowspec = lambda w: pl.BlockSpec((blk, w), lambda b, t: (b * nt + t, 0))
    sshape = (depth, nseq, halo, width)
    cspec = _layer_state_spec(sshape, nb, layer)
    args = [cols, conv_w, conv_b.reshape(1, width), wq, wk, wv, wg, bg.reshape(1, ng)]
    st_specs, st_args, aliases = _state_args(
        [cspec], None if init is None else (init,), (None, None, None, None, prev), len(args))
    return pl.pallas_call(
        functools.partial(_mlproj_body, nb=nb, tl=tl, has_init=init is not None,
                          n_prev=len(aliases)),
        grid=(nseq // nb, nt),
        in_specs=[pl.BlockSpec((blk, width), lambda b, t: (rb0 + b * nt + t, cb))]
                 + [full(a) for a in args[1:]] + st_specs,
        out_specs=(rowspec(width), rowspec(width), rowspec(width), rowspec(ng), cspec),
        out_shape=(jax.ShapeDtypeStruct((nseq * seq, width), F32),) * 3
                  + (jax.ShapeDtypeStruct((nseq * seq, ng), F32),
                     jax.ShapeDtypeStruct(sshape, F32)),
        scratch_shapes=[pltpu.VMEM((nb, tl + SUBLANES, width), F32)],
        input_output_aliases=aliases,
        compiler_params=_cparams(("parallel", "arbitrary")),
        name="mlproj",
    )(*args, *st_args)


def _mlrec_body(*refs, nb, tl, c, heads, unroll, has_init, n_prev):
    q_ref, k_ref, v_ref, g_ref, o_ref, nrm_ref = refs[:6]
    h_ref, c_ref, n_ref, m_ref, hf_ref = refs[6 + 3 * has_init + n_prev:]
    t = pl.program_id(1)
    dh = q_ref.shape[1] // heads

    @pl.when(t == 0)
    def _():
        if has_init:
            c_ref[...] = refs[6][...]
            n_ref[...] = refs[7][...]
            m_ref[...] = refs[8][...]
        else:
            c_ref[...] = jnp.zeros_like(c_ref)
            n_ref[...] = jnp.zeros_like(n_ref)
            m_ref[...] = jnp.zeros_like(m_ref)

    glane = lax.broadcasted_iota(jnp.int32, (c, 2 * heads), 1)
    r = lax.broadcasted_iota(jnp.int32, (c, c), 0)
    s = lax.broadcasted_iota(jnp.int32, (c, c), 1)
    causal = r >= s
    chunks_per_seq = tl // c
    k_scale = dh ** -0.5

    def chunk(idx, carry):
        n = idx // chunks_per_seq
        rows = pl.ds(pl.multiple_of(idx * c, c), c)
        gt = g_ref[rows, :]
        for h in range(heads):
            hs = slice(h * dh, (h + 1) * dh)
            q = q_ref[rows, hs]
            k = k_ref[rows, hs] * k_scale
            v = v_ref[rows, hs]
            ig_col = jnp.sum(jnp.where(glane == h, gt, 0.0), axis=1, keepdims=True)
            fp_col = jnp.sum(jnp.where(glane == h + heads, gt, 0.0), axis=1, keepdims=True)
            lf_col = jnp.minimum(fp_col, 0.0) - jnp.log1p(jnp.exp(-jnp.abs(fp_col)))
            lf_row = jnp.sum(jnp.where(r == s, lf_col, 0.0), axis=0, keepdims=True)
            ig_row = jnp.sum(jnp.where(r == s, ig_col, 0.0), axis=0, keepdims=True)
            b_col = jnp.sum(jnp.where(causal, lf_row, 0.0), axis=1, keepdims=True)
            b_row = jnp.sum(jnp.where(r <= s, lf_col, 0.0), axis=0, keepdims=True)

            m_prev = m_ref[n, h][:, 0:1]
            n_prev = n_ref[n, h]
            c_prev = c_ref[n, h]

            logw = jnp.where(causal, b_col - b_row + ig_row, NEG_BIG)
            m_t = jnp.maximum(b_col + m_prev, jnp.max(logw, axis=1, keepdims=True))
            w_prev = jnp.exp(b_col + m_prev - m_t)
            w = jnp.exp(logw - m_t)
            qb = q.astype(BF16)
            vb = v.astype(BF16)
            sc = lax.dot_general(qb, k.astype(BF16), (((1,), (1,)), ((), ())),
                                 preferred_element_type=F32) * w
            num = (w_prev * jnp.dot(qb, c_prev.astype(BF16), preferred_element_type=F32)
                   + jnp.dot(sc.astype(BF16), vb, preferred_element_type=F32))
            den = (w_prev * jnp.sum(q * n_prev, axis=1, keepdims=True)
                   + jnp.sum(sc, axis=1, keepdims=True))
            hval = num / jnp.maximum(jnp.abs(den), jnp.exp(-m_t))

            m_end = m_t[c - 1:c, :]
            b_end = b_col[c - 1:c, :]
            g_prev = jnp.exp(b_end + m_prev - m_end)
            w_in = jnp.exp(b_end - b_col + ig_col - m_end)
            kw = k * w_in
            c_ref[n, h] = g_prev * c_prev + lax.dot_general(
                kw.astype(BF16), vb, (((0,), (0,)), ((), ())), preferred_element_type=F32)
            n_ref[n, h] = g_prev * n_prev + jnp.sum(kw, axis=0, keepdims=True)
            m_ref[n, h] = jnp.broadcast_to(m_end, (1, LANES))
            hf_ref[rows, hs] = (hval * _rms_scale(hval) * nrm_ref[:, hs]
                                * jax.nn.sigmoid(o_ref[rows, hs]))
        return carry

    lax.fori_loop(0, nb * chunks_per_seq, chunk, 0, unroll=unroll)
    h_ref[...] = hf_ref[...].astype(BF16)


def _mlrec(q, k, v, gates, cols, nrm, inits, prevs, *, layer, depth, nseq, heads, row0, col0,
           nb, tl, c, unroll):
    width = q.shape[1]
    dh = width // heads
    seq = q.shape[0] // nseq
    nt = seq // tl
    blk = nb * tl
    rb0 = row0 // blk
    cb = col0 // width
    ng = gates.shape[1]
    qspec = pl.BlockSpec((blk, width), lambda b, t: (b * nt + t, 0))
    ospec = pl.BlockSpec((blk, width), lambda b, t: (rb0 + b * nt + t, 0))
    shapes = ((depth, nseq, heads, dh, dh), (depth, nseq, heads, 1, dh), (depth, nseq, heads, 1, LANES))
    sspecs = [_layer_state_spec(sh, nb, layer) for sh in shapes]
    args = [q, k, v, gates, cols, nrm.reshape(1, width)]
    st_specs, st_args, aliases = _state_args(sspecs, inits, prevs, len(args))
    return pl.pallas_call(
        functools.partial(_mlrec_body, nb=nb, tl=tl, c=c, heads=heads, unroll=unroll,
                          has_init=inits is not None, n_prev=len(aliases)),
        grid=(nseq // nb, nt),
        in_specs=[qspec, qspec, qspec,
                  pl.BlockSpec((blk, ng), lambda b, t: (b * nt + t, 0)),
                  pl.BlockSpec((blk, width), lambda b, t: (rb0 + b * nt + t, cb)),
                  pl.BlockSpec((1, width), lambda b, t: (0, 0))] + st_specs,
        out_specs=(ospec, *sspecs),
        out_shape=(jax.ShapeDtypeStruct((cols.shape[0], width), BF16),)
                  + tuple(jax.ShapeDtypeStruct(sh, F32) for sh in shapes),
        scratch_shapes=[pltpu.VMEM((blk, width), F32)],
        input_output_aliases=aliases,
        compiler_params=_cparams(("parallel", "arbitrary")),
        name="mlrec",
    )(*args, *st_args)


def _merge_body(x_ref, oa_ref, ob_ref, oc_ref, ga_ref, gb_ref, gc_ref, wa_ref, wba_ref, wbb_ref,
                wc_ref, wo_ref, g_ref, o_ref, acc_ref):
    j = pl.program_id(1)

    @pl.when(j == 0)
    def _():
        acc_ref[...] = jnp.zeros_like(acc_ref)

    yb = ob_ref[...]
    br_a = jnp.dot(oa_ref[...], wa_ref[...], preferred_element_type=F32)
    br_b = (jnp.dot(yb, wba_ref[...], preferred_element_type=F32)
            * jax.nn.sigmoid(jnp.dot(yb, wbb_ref[...], preferred_element_type=F32)))
    br_c = jnp.dot(oc_ref[...], wc_ref[...], preferred_element_type=F32)
    merged = (jax.nn.sigmoid(ga_ref[...]) * br_a + jax.nn.sigmoid(gb_ref[...]) * br_b
              + jax.nn.sigmoid(gc_ref[...]) * br_c)
    acc_ref[...] += jnp.dot(merged.astype(BF16), wo_ref[...], preferred_element_type=F32)

    @pl.when(j == pl.num_programs(1) - 1)
    def _():
        y = acc_ref[...]
        o_ref[...] = x_ref[...] + y * _rms_scale(y) * g_ref[...]


def _merge(x, oa, ob, oc, cols, w_a, w_ba, w_bb, w_c, w_out, gains, k_gain, *, layer, col0,
           tm=512, tn=512):
    m, d = x.shape
    mix_w = oa.shape[1]
    nj = d // tn
    gb0 = col0 // tn
    act = pl.BlockSpec((tm, mix_w), lambda i, j: (i, 0))
    wspec = pl.BlockSpec((None, mix_w, tn), lambda i, j: (layer, 0, j))

    def gate(branch):
        return pl.BlockSpec((tm, tn), lambda i, j: (i, gb0 + branch * nj + j))

    return pl.pallas_call(
        _merge_body,
        grid=(m // tm, nj),
        in_specs=[pl.BlockSpec((tm, d), lambda i, j: (i, 0)), act, act, act,
                  gate(0), gate(1), gate(2), wspec, wspec, wspec, wspec,
                  pl.BlockSpec((None, tn, d), lambda i, j: (layer, j, 0)),
                  _gain_spec(gains, layer, k_gain)],
        out_specs=pl.BlockSpec((tm, d), lambda i, j: (i, 0)),
        out_shape=jax.ShapeDtypeStruct((m, d), F32),
        scratch_shapes=[pltpu.VMEM((tm, d), F32)],
        compiler_params=_cparams(("parallel", "arbitrary")),
        name="merge",
    )(x, oa, ob, oc, cols, cols, cols, w_a, w_ba, w_bb, w_c, w_out, gains)


def kernel(x_prompt, x_sample, state_hgrn, state_s5_re, state_s5_im, state_mlstm_c, state_mlstm_n, state_mlstm_m, state_mlstm_conv, norm_gains, w_ffn1_up, w_ffn1_down, w_in, hgrn_lower_bounds, hgrn_norm, w_hgrn_out, s5_a_re, s5_a_im, s5_log_dt, s5_b_re, s5_b_im, s5_c_re, s5_c_im, s5_d, w_s5_glu_a, w_s5_glu_b, mlstm_conv_w, mlstm_conv_b, mlstm_wq, mlstm_wk, mlstm_wv, mlstm_w_gates, mlstm_b_gates, mlstm_norm, w_mlstm_out, w_out, w_ffn2_up, w_ffn2_down):
    batch, seq, d_model = x_prompt.shape
    dec_batch, dec_seq, _ = x_sample.shape
    depth = w_in.shape[0]
    mix_w = w_hgrn_out.shape[1]
    n_prompt = batch * seq
    heads_hg = state_hgrn.shape[2]
    groups, state = s5_a_re.shape[1:]
    slabs = groups * state // LANES

    x = jnp.concatenate([x_prompt.reshape(n_prompt, d_model),
                         x_sample.reshape(dec_batch * dec_seq, d_model)], axis=0)

    lb_all, ab_re, ab_im, bb_re, bb_im = _prep(hgrn_lower_bounds, s5_a_re, s5_a_im, s5_log_dt,
                                               s5_b_re, s5_b_im)

    col_su, col_mx, col_mo, col_gz = 4 * mix_w, 5 * mix_w, 6 * mix_w, 7 * mix_w
    dh = mix_w // ML_HEADS

    sample_init = dict(
        hg=state_hgrn,
        s5=(state_s5_re.reshape(depth, dec_batch, slabs, LANES),
            state_s5_im.reshape(depth, dec_batch, slabs, LANES)),
        conv=state_mlstm_conv,
        ml=(state_mlstm_c, state_mlstm_n.reshape(depth, dec_batch, ML_HEADS, 1, dh),
            jnp.broadcast_to(state_mlstm_m[..., None, None], (depth, dec_batch, ML_HEADS, 1, LANES))))
    groups_cfg = (
        dict(row0=0, nseq=batch, seq=seq, init=dict(hg=None, s5=None, conv=None, ml=None),
             hg=dict(nb=1, tl=512, c=16), s5=dict(nb=1, tl=128), mp=dict(nb=1, tl=256),
             mr=dict(nb=1, tl=256, c=128, unroll=2)),
        dict(row0=n_prompt, nseq=dec_batch, seq=dec_seq, init=sample_init,
             hg=dict(nb=4, tl=dec_seq, c=dec_seq), s5=dict(nb=16, tl=dec_seq),
             mp=dict(nb=16, tl=dec_seq), mr=dict(nb=4, tl=dec_seq, c=dec_seq, unroll=2)),
    )
    new = [dict(hg=None, s5=(None, None), conv=None, ml=(None, None, None)) for _ in groups_cfg]

    gains = norm_gains.reshape(depth * N_GAINS, 1, d_model)
    bf = lambda w: w.astype(BF16)
    w_ffn1_up, w_ffn1_down, w_ffn2_up, w_ffn2_down = map(bf, (w_ffn1_up, w_ffn1_down, w_ffn2_up, w_ffn2_down))
    w_in = bf(w_in)
    w_hgrn_out, w_s5_glu_a, w_s5_glu_b, w_mlstm_out, w_out = map(
        bf, (w_hgrn_out, w_s5_glu_a, w_s5_glu_b, w_mlstm_out, w_out))

    for l in range(depth):
        x = _ffn(x, gains, 0, 1, w_ffn1_up, w_ffn1_down, layer=l)
        cols = _inproj(x, gains, 2, w_in, layer=l)

        wb, wc = _s5_weights(bb_re[l], bb_im[l], s5_c_re[l], s5_c_im[l])
        abr = ab_re[l].reshape(slabs, LANES)
        abi = ab_im[l].reshape(slabs, LANES)
        s5d = s5_d[l].reshape(1, mix_w)
        wq, wk, wv = (w[l].astype(BF16) for w in (mlstm_wq, mlstm_wk, mlstm_wv))
        wg = mlstm_w_gates[l].astype(BF16)

        oa = ob = oc = None
        for cfg, st in zip(groups_cfg, new):
            common = dict(layer=l, nseq=cfg['nseq'], row0=cfg['row0'])
            init = cfg['init']
            oa, st['hg'] = _hgrn(cols, lb_all, hgrn_norm, init['hg'], (oa, st['hg']),
                                 heads=heads_hg, seq=cfg['seq'], **common, **cfg['hg'])
            ob, *st['s5'] = _s5(cols, wb, wc, abr, abi, s5d, init['s5'], (ob, *st['s5']),
                                depth=depth, col0=col_su, seq=cfg['seq'], **common, **cfg['s5'])
            q, k, v, gates, st['conv'] = _mlproj(
                cols, mlstm_conv_w[l], mlstm_conv_b[l], wq, wk, wv, wg, mlstm_b_gates[l],
                init['conv'], st['conv'], depth=depth, col0=col_mx, seq=cfg['seq'], **common,
                **cfg['mp'])
            oc, *st['ml'] = _mlrec(q, k, v, gates, cols, mlstm_norm[l], init['ml'], (oc, *st['ml']),
                                   depth=depth, heads=ML_HEADS, col0=col_mo, **common, **cfg['mr'])

        x = _merge(x, oa, ob, oc, cols, w_hgrn_out, w_s5_glu_a, w_s5_glu_b, w_mlstm_out, w_out,
                   gains, 3, layer=l, col0=col_gz)
        x = _ffn(x, gains, 4, 5, w_ffn2_up, w_ffn2_down, layer=l)

    y_prompt = x[:n_prompt].reshape(batch, seq, d_model)
    y_sample = x[n_prompt:].reshape(dec_batch, dec_seq, d_model)
    states = ()
    for cfg, st in zip(groups_cfg, new):
        nseq = cfg['nseq']
        c_new, n_new, m_new = st['ml']
        states += (st['hg'],
                   st['s5'][0].reshape(depth, nseq, groups, state),
                   st['s5'][1].reshape(depth, nseq, groups, state),
                   c_new, n_new.reshape(depth, nseq, ML_HEADS, dh), m_new[:, :, :, 0, 0], st['conv'])
    return (y_prompt, y_sample) + states
```

```python
import functools
import math

import jax
import jax.numpy as jnp
from jax import lax
from jax.experimental import pallas as pl
from jax.experimental.pallas import tpu as pltpu

F32 = jnp.float32
BF16 = jnp.bfloat16

EPS = 1e-6
NEG_BIG = -1e30
S5_DT_MIN_CLAMP = -1e-4

HG_DK = 128
S5_CH = 16
S5_SUPER = 8
ML_HEADS = 4
CONV_W = 4
N_GAINS = 6

LANES = 128
SUBLANES = 8
VMEM_LIMIT = 56 * 1024 * 1024


def _cparams(sem):
    return pltpu.CompilerParams(dimension_semantics=sem, vmem_limit_bytes=VMEM_LIMIT)


def _rms_scale(y):
    return lax.rsqrt(jnp.mean(y * y, axis=-1, keepdims=True) + EPS)


def _silu(x):
    return x * jax.nn.sigmoid(x)


NORM_ROWS = 16


def _for_row_chunks(n_rows, body):
    def step(r, carry):
        body(pl.ds(pl.multiple_of(r * NORM_ROWS, NORM_ROWS), NORM_ROWS))
        return carry

    lax.fori_loop(0, n_rows // NORM_ROWS, step, 0, unroll=8)


def _prenorm_to(h_ref, x_ref, g_ref):
    g = g_ref[...]

    def body(rows):
        x = x_ref[rows, :]
        h_ref[rows, :] = (x * _rms_scale(x) * g).astype(BF16)

    _for_row_chunks(x_ref.shape[0], body)


def _postnorm_residual_to(o_ref, x_ref, y_ref, g_ref, weight):
    g = g_ref[...] if weight == 1.0 else g_ref[...] * weight

    def body(rows):
        y = y_ref[rows, :]
        o_ref[rows, :] = x_ref[rows, :] + y * _rms_scale(y) * g

    _for_row_chunks(x_ref.shape[0], body)


def _ffn_body(*refs, n_x, n_out, tiles0):
    x_refs = refs[:n_x]
    gpre_ref, gpost_ref, wa_ref, wb_ref, wd_ref = refs[n_x:n_x + 5]
    o_refs = refs[n_x + 5:n_x + 5 + n_out]
    h_ref, acc_ref = refs[n_x + 5 + n_out:]
    i = pl.program_id(0)
    j = pl.program_id(1)
    in_part = (i < tiles0, i >= tiles0)

    for part in range(n_x):
        @pl.when((j == 0) & in_part[part] if n_x == 2 else j == 0)
        def _():
            _prenorm_to(h_ref, x_refs[part], gpre_ref)

    @pl.when(j == 0)
    def _():
        acc_ref[...] = jnp.zeros_like(acc_ref)

    h = h_ref[...]
    a = jnp.dot(h, wa_ref[...], preferred_element_type=F32)
    b = jnp.dot(h, wb_ref[...], preferred_element_type=F32)
    act = (_silu(a) * b).astype(BF16)
    acc_ref[...] += jnp.dot(act, wd_ref[...], preferred_element_type=F32)

    last = j == pl.num_programs(1) - 1
    split = max(n_x, n_out) == 2
    for part in range(2 if split else 1):
        @pl.when(last & in_part[part] if split else last)
        def _():
            _postnorm_residual_to(o_refs[part if n_out == 2 else 0], x_refs[part if n_x == 2 else 0],
                                  acc_ref, gpost_ref, 0.5)


def _gain_spec(gains, layer, k):
    return pl.BlockSpec((None, 1, gains.shape[-1]), lambda i, j: (layer * N_GAINS + k, 0, 0))


def _ffn(x, gains, k_pre, k_post, w_up, w_down, *, layer, rows0, split_out=False, tm=512, tf=512):
    xs = x if isinstance(x, tuple) else (x,)
    d = xs[0].shape[1]
    m = sum(a.shape[0] for a in xs)
    d_ff = w_down.shape[1]
    nf = d_ff // tf
    tiles0 = rows0 // tm
    whole = pl.BlockSpec((tm, d), lambda i, j: (i, 0))
    parts = [pl.BlockSpec((tm, d), lambda i, j: (jnp.minimum(i, tiles0 - 1), 0)),
             pl.BlockSpec((tm, d), lambda i, j: (jnp.maximum(i - tiles0, 0), 0))]
    part_shapes = [jax.ShapeDtypeStruct((rows0, d), F32), jax.ShapeDtypeStruct((m - rows0, d), F32)]
    out = pl.pallas_call(
        functools.partial(_ffn_body, n_x=len(xs), n_out=2 if split_out else 1, tiles0=tiles0),
        grid=(m // tm, nf),
        in_specs=(parts if len(xs) == 2 else [whole]) + [
            _gain_spec(gains, layer, k_pre),
            _gain_spec(gains, layer, k_post),
            pl.BlockSpec((None, d, tf), lambda i, j: (layer, 0, j)),
            pl.BlockSpec((None, d, tf), lambda i, j: (layer, 0, j + nf)),
            pl.BlockSpec((None, tf, d), lambda i, j: (layer, j, 0)),
        ],
        out_specs=parts if split_out else whole,
        out_shape=part_shapes if split_out else jax.ShapeDtypeStruct((m, d), F32),
        scratch_shapes=[pltpu.VMEM((tm, d), BF16), pltpu.VMEM((tm, d), F32)],
        compiler_params=_cparams(("parallel", "arbitrary")),
        name="ffn",
    )(*xs, gains, gains, w_up, w_up, w_down)
    return tuple(out) if split_out else out


def _inproj_body(x_ref, g_ref, w_ref, o_ref, h_ref):
    @pl.when(pl.program_id(1) == 0)
    def _():
        _prenorm_to(h_ref, x_ref, g_ref)

    o_ref[...] = jnp.dot(h_ref[...], w_ref[...].astype(BF16), preferred_element_type=F32)


def _inproj(x, gains, k_gain, w, *, layer, tm=1024, tn=1024):
    m, d = x.shape
    n = w.shape[2]
    return pl.pallas_call(
        _inproj_body,
        grid=(m // tm, n // tn),
        in_specs=[
            pl.BlockSpec((tm, d), lambda i, j: (i, 0)),
            _gain_spec(gains, layer, k_gain),
            pl.BlockSpec((None, d, tn), lambda i, j: (layer, 0, j)),
        ],
        out_specs=pl.BlockSpec((tm, tn), lambda i, j: (i, j)),
        out_shape=jax.ShapeDtypeStruct((m, n), F32),
        scratch_shapes=[pltpu.VMEM((tm, d), BF16)],
        compiler_params=_cparams(("parallel", "arbitrary")),
        name="inproj",
    )(x, gains, w)


def _prep_body(lbraw_ref, are_ref, aim_ref, ldt_ref, bre_ref, bim_ref,
               lb_ref, abr_ref, abi_ref, bbr_ref, bbi_ref):
    depth = lbraw_ref.shape[0]
    raw = lbraw_ref[...]
    e = jnp.exp(raw - jnp.max(raw, axis=0, keepdims=True))
    lbs = e / jnp.sum(e, axis=0, keepdims=True)
    run = jnp.zeros_like(lbs[0:1])
    for l in range(depth):
        run = run + lbs[l:l + 1]
        lb_ref[l:l + 1, :] = run - lbs[0:1]

    for l in range(depth):
        dt = jnp.exp(ldt_ref[l])
        lam_re = jnp.minimum(are_ref[l], S5_DT_MIN_CLAMP)
        lam_im = aim_ref[l]
        mag = jnp.exp(lam_re * dt)
        ab_re = mag * jnp.cos(lam_im * dt)
        ab_im = mag * jnp.sin(lam_im * dt)
        inv = 1.0 / (lam_re * lam_re + lam_im * lam_im)
        f_re = ((ab_re - 1.0) * lam_re + ab_im * lam_im) * inv
        f_im = (ab_im * lam_re - (ab_re - 1.0) * lam_im) * inv
        abr_ref[l] = ab_re
        abi_ref[l] = ab_im
        b_re = bre_ref[l]
        b_im = bim_ref[l]
        bbr_ref[l] = f_re * b_re - f_im * b_im
        bbi_ref[l] = f_re * b_im + f_im * b_re


def _prep(lb_raw, a_re, a_im, log_dt, b_re, b_im):
    depth, groups, state = a_re.shape
    gp = groups * state
    ch = b_re.shape[-1]
    row = lambda a: a.reshape(depth, 1, gp)
    ldt = jnp.broadcast_to(log_dt[:, :, None], (depth, groups, state))
    chan_major = lambda b: b.transpose(0, 3, 1, 2).reshape(depth, ch, gp)
    out_shape = (
        jax.ShapeDtypeStruct(lb_raw.shape, F32),
        jax.ShapeDtypeStruct((depth, 1, gp), F32),
        jax.ShapeDtypeStruct((depth, 1, gp), F32),
        jax.ShapeDtypeStruct((depth, ch, gp), F32),
        jax.ShapeDtypeStruct((depth, ch, gp), F32),
    )
    return pl.pallas_call(_prep_body, out_shape=out_shape, name="prep")(
        lb_raw, row(a_re), row(a_im), row(ldt), chan_major(b_re), chan_major(b_im))


def _hgrn_body(*refs, nb, tl, c, heads, has_init, n_prev):
    q_ref, f_ref, i_ref, g_ref, lb_ref, nrm_ref = refs[:6]
    s0_ref = refs[6] if has_init else None
    (o_ref, s_ref, st_ref, of_ref, prod_ref, sums_ref, qc_ref, ke_ref, vb_ref, ce_ref, gs_ref
     ) = refs[6 + has_init + n_prev:]
    t = pl.program_id(1)

    @pl.when(t == 0)
    def _():
        def init(n, carry):
            for h in range(heads):
                if has_init:
                    st_ref[n, h] = s0_ref[n, h].T
                else:
                    st_ref[n, h] = jnp.zeros(st_ref.shape[2:], F32)
            return carry

        lax.fori_loop(0, nb, init, 0)

    row = lax.broadcasted_iota(jnp.int32, (c, HG_DK), 0)
    lag = row - lax.broadcasted_iota(jnp.int32, (c, HG_DK), 1)
    ones = jnp.ones((HG_DK, LANES), BF16)
    chunks_per_seq = tl // c
    n_steps = nb * chunks_per_seq
    head_cols = [slice(h * HG_DK, (h + 1) * HG_DK) for h in range(heads)]

    def chunk_rows(idx):
        return pl.ds(pl.multiple_of(idx * c, c), c)

    def prepare(idx):
        rows = chunk_rows(idx)
        for h, hs in enumerate(head_cols):
            lb = lb_ref[:, hs]
            fpre = f_ref[rows, hs]
            q = _silu(q_ref[rows, hs])
            forget = lb + (1.0 - lb) * jax.nn.sigmoid(fpre)
            kin = (1.0 - lb) * jax.nn.sigmoid(-fpre)

            cp = forget
            sh = 1
            while sh < c:
                cp = cp * jnp.where(row >= sh, pltpu.roll(cp, sh, 0), 1.0)
                sh *= 2
            sp = jnp.where(row < c - 1, pltpu.roll(forget, c - 1, 0), 1.0)
            sh = 1
            while sh < c:
                sp = sp * jnp.where(row < c - sh, pltpu.roll(sp, c - sh, 0), 1.0)
                sh *= 2

            kd = kin
            prods = [q * kd]
            for d in range(1, c):
                kd = pltpu.roll(kd, 1, 0) * forget
                prods.append(q * kd)
            prod_ref[h] = jnp.concatenate(prods, axis=0).astype(BF16)
            qc_ref[h] = (q * cp).astype(BF16)
            ke_ref[h] = (kin * sp).astype(BF16)
            vb_ref[h] = i_ref[rows, hs].astype(BF16)
            ce_ref[h] = cp[c - 1:c, :]
            gs_ref[h] = nrm_ref[:, hs] * _silu(g_ref[rows, hs])

    def finish(idx):
        n = idx // chunks_per_seq
        rows = chunk_rows(idx)
        sums_ref[...] = jnp.dot(prod_ref[...].reshape(heads * c * c, HG_DK), ones,
                                preferred_element_type=F32)
        inter = []
        for h in range(heads):
            st = st_ref[n, h]
            vb = vb_ref[h]
            inter.append(lax.dot_general(qc_ref[h], st.astype(BF16), (((1,), (1,)), ((), ())),
                                         preferred_element_type=F32))
            upd = lax.dot_general(vb, ke_ref[h], (((0,), (0,)), ((), ())),
                                  preferred_element_type=F32)
            st_ref[n, h] = st * ce_ref[h] + upd
        outs = []
        for h in range(heads):
            scores = jnp.zeros((c, HG_DK), F32)
            for d in range(c):
                scores = jnp.where(lag == d, sums_ref[pl.ds((h * c + d) * c, c), :], scores)
            outs.append(inter[h] + jnp.dot(scores[:, :c].astype(BF16), vb_ref[h],
                                           preferred_element_type=F32))
        for hs, h in zip(head_cols, range(heads)):
            of_ref[rows, hs] = outs[h] * _rms_scale(outs[h]) * gs_ref[h]

    def step(idx, carry):
        finish(idx - 1)
        prepare(idx)
        return carry

    prepare(0)
    lax.fori_loop(1, n_steps, step, 0, unroll=2)
    finish(n_steps - 1)
    o_ref[...] = of_ref[...].astype(BF16)

    @pl.when(t == pl.num_programs(1) - 1)
    def _():
        def fin(n, carry):
            for h in range(heads):
                s_ref[n, h] = st_ref[n, h].T
            return carry

        lax.fori_loop(0, nb, fin, 0)


def _layer_state_spec(shape, nb, layer):
    tail = (0,) * (len(shape) - 2)
    return pl.BlockSpec((None, nb) + tuple(shape[2:]), lambda b, t: (layer, b) + tail)


def _hgrn(cols, lb, nrm, s_init, prevs, *, layer, nseq, heads, row0, seq, nb, tl, c):
    depth = lb.shape[0]
    dk = HG_DK
    mix_w = heads * dk
    dv = mix_w // heads
    nt = seq // tl
    blk = nb * tl
    rb0 = row0 // blk
    sshape = (depth, nseq, heads, dk, dv)

    def colspec(group):
        return pl.BlockSpec((blk, mix_w), lambda b, t: (rb0 + b * nt + t, group))

    vec = pl.BlockSpec((None, 1, mix_w), lambda b, t: (layer, 0, 0))
    sspec = _layer_state_spec(sshape, nb, layer)
    args = [cols, cols, cols, cols, lb.reshape(depth, 1, mix_w), nrm.reshape(depth, 1, mix_w)]
    st_specs, st_args, aliases = _state_args(
        [sspec], None if s_init is None else (s_init,), prevs, len(args))
    return pl.pallas_call(
        functools.partial(_hgrn_body, nb=nb, tl=tl, c=c, heads=heads,
                          has_init=s_init is not None, n_prev=len(aliases)),
        grid=(nseq // nb, nt),
        in_specs=[colspec(0), colspec(1), colspec(2), colspec(3), vec, vec] + st_specs,
        out_specs=(pl.BlockSpec((blk, mix_w), lambda b, t: (rb0 + b * nt + t, 0)), sspec),
        out_shape=(jax.ShapeDtypeStruct((cols.shape[0], mix_w), BF16),
                   jax.ShapeDtypeStruct(sshape, F32)),
        scratch_shapes=[pltpu.VMEM((nb, heads, dv, dk), F32), pltpu.VMEM((blk, mix_w), F32),
                        pltpu.VMEM((heads, c * c, dk), BF16), pltpu.VMEM((heads * c * c, LANES), F32),
                        pltpu.VMEM((heads, c, dk), BF16),
                        pltpu.VMEM((heads, c, dk), BF16), pltpu.VMEM((heads, c, dv), BF16),
                        pltpu.VMEM((heads, 1, dk), F32), pltpu.VMEM((heads, c, dv), F32)],
        input_output_aliases=aliases,
        compiler_params=_cparams(("parallel", "arbitrary")),
        name="hgrn",
    )(*args, *st_args)


def _gelu_tanh(x):
    return 0.5 * x * (1.0 + jnp.tanh(math.sqrt(2.0 / math.pi) * (x + 0.044715 * (x * x * x))))


def _state_args(specs, inits, prevs, n_args):
    in_specs, args, aliases = [], [], {}
    if inits is not None:
        in_specs += list(specs)
        args += list(inits)
    for out_idx, prev in enumerate(prevs):
        if prev is not None:
            aliases[n_args + len(args)] = out_idx
            in_specs.append(pl.BlockSpec(memory_space=pl.ANY))
            args.append(prev)
    return in_specs, args, aliases


def _s5_body(*refs, nb, tl, has_init, n_prev):
    u_ref, wb_ref, wc_ref, abr_ref, abi_ref, d_ref = refs[:6]
    y_ref, xr_ref, xi_ref, sr_ref, si_ref = refs[6 + 2 * has_init + n_prev:]
    rows = nb * tl
    pitch = rows + SUBLANES
    nsb = wb_ref.shape[0]
    cw = wb_ref.shape[1]
    half = wb_ref.shape[2] // 2
    per = half // LANES
    slabs = nsb * per
    t = pl.program_id(1)

    @pl.when(t == 0)
    def _():
        if has_init:
            xr_ref[...] = refs[6][...]
            xi_ref[...] = refs[7][...]
        else:
            xr_ref[...] = jnp.zeros_like(xr_ref)
            xi_ref[...] = jnp.zeros_like(xi_ref)

    for sb in range(nsb):
        ub = u_ref[:, sb * cw:(sb + 1) * cw].astype(BF16)
        res = jnp.dot(ub, wb_ref[sb], preferred_element_type=F32)
        for j in range(per):
            base = (sb * per + j) * pitch
            sr_ref[pl.ds(base, rows), :] = res[:, j * LANES:(j + 1) * LANES]
            si_ref[pl.ds(base, rows), :] = res[:, half + j * LANES:half + (j + 1) * LANES]

    ar = abr_ref[...]
    ai = abi_ref[...]

    def advance(n, tt, xr, xi):
        at = pl.ds(n * tl + tt, slabs, stride=pitch)
        nxr = ar * xr - ai * xi + sr_ref[at, :]
        nxi = ar * xi + ai * xr + si_ref[at, :]
        sr_ref[at, :] = nxr
        si_ref[at, :] = nxi
        return nxr, nxi

    par = 2 if nb % 2 == 0 else 1

    def group_scan(p, carry):
        seqs = [par * p + i for i in range(par)]

        def step(tt, x):
            out = ()
            for i, n in enumerate(seqs):
                out += advance(n, tt, x[2 * i], x[2 * i + 1])
            return out

        x0 = ()
        for n in seqs:
            x0 += (xr_ref[n], xi_ref[n])
        x = lax.fori_loop(0, tl, step, x0, unroll=SUBLANES)
        for i, n in enumerate(seqs):
            xr_ref[n] = x[2 * i]
            xi_ref[n] = x[2 * i + 1]
        return carry

    lax.fori_loop(0, nb // par, group_scan, 0)

    for sb in range(nsb):
        parts = [sr_ref[pl.ds((sb * per + j) * pitch, rows), :] for j in range(per)]
        parts += [si_ref[pl.ds((sb * per + j) * pitch, rows), :] for j in range(per)]
        xs = jnp.concatenate(parts, axis=1).astype(BF16)
        cs = slice(sb * cw, (sb + 1) * cw)
        y = jnp.dot(xs, wc_ref[sb], preferred_element_type=F32) + d_ref[:, cs] * u_ref[:, cs]
        y_ref[:, cs] = _gelu_tanh(y).astype(BF16)


def _s5(cols, wb, wc, ab_re, ab_im, d, inits, prevs, *, layer, depth, nseq, row0, col0, seq, nb, tl):
    slabs = ab_re.shape[0]
    nt = seq // tl
    blk = nb * tl
    rb0 = row0 // blk
    mix_w = d.shape[-1]
    cb = col0 // mix_w
    const3 = lambda a: pl.BlockSpec(a.shape, lambda b, t: (0, 0, 0))
    const2 = lambda a: pl.BlockSpec(a.shape, lambda b, t: (0, 0))
    sshape = (depth, nseq, slabs, LANES)
    xspec = _layer_state_spec(sshape, nb, layer)
    pitch = blk + SUBLANES
    args = [cols, wb, wc, ab_re, ab_im, d]
    st_specs, st_args, aliases = _state_args([xspec, xspec], inits, prevs, len(args))
    return pl.pallas_call(
        functools.partial(_s5_body, nb=nb, tl=tl, has_init=inits is not None, n_prev=len(aliases)),
        grid=(nseq // nb, nt),
        in_specs=[pl.BlockSpec((blk, mix_w), lambda b, t: (rb0 + b * nt + t, cb)),
                  const3(wb), const3(wc), const2(ab_re), const2(ab_im), const2(d)] + st_specs,
        out_specs=(pl.BlockSpec((blk, mix_w), lambda b, t: (rb0 + b * nt + t, 0)), xspec, xspec),
        out_shape=(jax.ShapeDtypeStruct((cols.shape[0], mix_w), BF16),
                   jax.ShapeDtypeStruct(sshape, F32),
                   jax.ShapeDtypeStruct(sshape, F32)),
        scratch_shapes=[pltpu.VMEM((slabs * pitch, LANES), F32),
                        pltpu.VMEM((slabs * pitch, LANES), F32)],
        input_output_aliases=aliases,
        compiler_params=_cparams(("parallel", "arbitrary")),
        name="s5",
    )(*args, *st_args)


def _s5_weights(bb_re, bb_im, c_re, c_im):
    ch, gp = bb_re.shape
    groups = c_re.shape[0]
    state = gp // groups
    nsb = groups // S5_SUPER
    eye = jnp.eye(S5_SUPER, dtype=F32)

    def in_w(bb):
        b4 = bb.reshape(ch, nsb, S5_SUPER, state)
        return jnp.einsum('csgp,gh->sgchp', b4, eye).reshape(nsb, S5_SUPER * ch, S5_SUPER * state)

    def out_w(c):
        c4 = c.reshape(nsb, S5_SUPER, ch, state)
        return jnp.einsum('sgcp,gh->sgphc', c4, eye).reshape(nsb, S5_SUPER * state, S5_SUPER * ch)

    wb = jnp.concatenate([in_w(bb_re), in_w(bb_im)], axis=2).astype(BF16)
    wc = jnp.concatenate([out_w(c_re), -out_w(c_im)], axis=1).astype(BF16)
    return wb, wc


def _mlproj_body(*refs, nb, tl, has_init, n_prev):
    x_ref, cw_ref, cb_ref, wq_ref, wk_ref, wv_ref, wg_ref, bg_ref = refs[:8]
    q_ref, k_ref, v_ref, g_ref, cn_ref, xx_ref = refs[8 + has_init + n_prev:]
    t = pl.program_id(1)
    width = x_ref.shape[1]
    halo = CONV_W - 1
    lo = SUBLANES - halo

    @pl.when(t == 0)
    def _():
        if has_init:
            xx_ref[:, lo:SUBLANES, :] = refs[8][...]
        else:
            xx_ref[:, lo:SUBLANES, :] = jnp.zeros((nb, halo, width), F32)

    @pl.when(t > 0)
    def _():
        xx_ref[:, lo:SUBLANES, :] = xx_ref[:, tl + lo:tl + SUBLANES, :]

    x = x_ref[...]
    xx_ref[:, SUBLANES:, :] = x.reshape(nb, tl, width)
    xc = jnp.zeros((nb, tl, width), F32) + cb_ref[...][None]
    for j in range(CONV_W):
        xc = xc + xx_ref[:, lo + j:lo + j + tl, :] * cw_ref[j:j + 1, :][None]
    cn_ref[...] = xx_ref[:, tl + lo:tl + SUBLANES, :]
    xc = _silu(xc).reshape(nb * tl, width).astype(BF16)
    xb = x.astype(BF16)

    dh = width // ML_HEADS
    gates = jnp.zeros((nb * tl, 2 * ML_HEADS), F32) + bg_ref[...]
    for h in range(ML_HEADS):
        hs = slice(h * dh, (h + 1) * dh)
        q = jnp.dot(xc[:, hs], wq_ref[h], preferred_element_type=F32)
        k = jnp.dot(xc[:, hs], wk_ref[h], preferred_element_type=F32)
        v = jnp.dot(xb[:, hs], wv_ref[h], preferred_element_type=F32)
        q_ref[:, hs] = q
        k_ref[:, hs] = k
        v_ref[:, hs] = v
        for part, val in enumerate((q, k, v)):
            w = wg_ref[part * width + h * dh:part * width + (h + 1) * dh, :]
            gates = gates + jnp.dot(val.astype(BF16), w, preferred_element_type=F32)
    g_ref[...] = gates


def _mlproj(cols, conv_w, conv_b, wq, wk, wv, wg, bg, init, prev, *, layer, depth, nseq, row0, col0,
            seq, nb, tl):
    width = conv_w.shape[1]
    halo = CONV_W - 1
    nt = seq // tl
    blk = nb * tl
    rb0 = row0 // blk
    cb = col0 // width
    ng = wg.shape[1]
    full = lambda a: pl.BlockSpec(a.shape, lambda b, t: (0,) * a.ndim)
    rowspec = lambda w: pl.BlockSpec((blk, w), lambda b, t: (b * nt + t, 0))
    sshape = (depth, nseq, halo, width)
    cspec = _layer_state_spec(sshape, nb, layer)
    args = [cols, conv_w, conv_b.reshape(1, width), wq, wk, wv, wg, bg.reshape(1, ng)]
    st_specs, st_args, aliases = _state_args(
        [cspec], None if init is None else (init,), (None, None, None, None, prev), len(args))
    return pl.pallas_call(
        functools.partial(_mlproj_body, nb=nb, tl=tl, has_init=init is not None,
                          n_prev=len(aliases)),
        grid=(nseq // nb, nt),
        in_specs=[pl.BlockSpec((blk, width), lambda b, t: (rb0 + b * nt + t, cb))]
                 + [full(a) for a in args[1:]] + st_specs,
        out_specs=(rowspec(width), rowspec(width), rowspec(width), rowspec(ng), cspec),
        out_shape=(jax.ShapeDtypeStruct((nseq * seq, width), F32),) * 3
                  + (jax.ShapeDtypeStruct((nseq * seq, ng), F32),
                     jax.ShapeDtypeStruct(sshape, F32)),
        scratch_shapes=[pltpu.VMEM((nb, tl + SUBLANES, width), F32)],
        input_output_aliases=aliases,
        compiler_params=_cparams(("parallel", "arbitrary")),
        name="mlproj",
    )(*args, *st_args)


def _mlrec_body(*refs, nb, tl, c, heads, unroll, has_init, n_prev):
    q_ref, k_ref, v_ref, g_ref, o_ref, nrm_ref = refs[:6]
    h_ref, c_ref, n_ref, m_ref, hf_ref = refs[6 + 3 * has_init + n_prev:]
    t = pl.program_id(1)
    dh = q_ref.shape[1] // heads

    @pl.when(t == 0)
    def _():
        if has_init:
            c_ref[...] = refs[6][...]
            n_ref[...] = refs[7][...]
            m_ref[...] = refs[8][...]
        else:
            c_ref[...] = jnp.zeros_like(c_ref)
            n_ref[...] = jnp.zeros_like(n_ref)
            m_ref[...] = jnp.zeros_like(m_ref)

    glane = lax.broadcasted_iota(jnp.int32, (c, 2 * heads), 1)
    r = lax.broadcasted_iota(jnp.int32, (c, c), 0)
    s = lax.broadcasted_iota(jnp.int32, (c, c), 1)
    causal = r >= s
    chunks_per_seq = tl // c
    k_scale = dh ** -0.5

    def chunk(idx, carry):
        n = idx // chunks_per_seq
        rows = pl.ds(pl.multiple_of(idx * c, c), c)
        gt = g_ref[rows, :]
        for h in range(heads):
            hs = slice(h * dh, (h + 1) * dh)
            q = q_ref[rows, hs]
            k = k_ref[rows, hs] * k_scale
            v = v_ref[rows, hs]
            ig_col = jnp.sum(jnp.where(glane == h, gt, 0.0), axis=1, keepdims=True)
            fp_col = jnp.sum(jnp.where(glane == h + heads, gt, 0.0), axis=1, keepdims=True)
            lf_col = jnp.minimum(fp_col, 0.0) - jnp.log1p(jnp.exp(-jnp.abs(fp_col)))
            lf_row = jnp.sum(jnp.where(r == s, lf_col, 0.0), axis=0, keepdims=True)
            ig_row = jnp.sum(jnp.where(r == s, ig_col, 0.0), axis=0, keepdims=True)
            b_col = jnp.sum(jnp.where(causal, lf_row, 0.0), axis=1, keepdims=True)
            b_row = jnp.sum(jnp.where(r <= s, lf_col, 0.0), axis=0, keepdims=True)

            m_prev = m_ref[n, h][:, 0:1]
            n_prev = n_ref[n, h]
            c_prev = c_ref[n, h]

            logw = jnp.where(causal, b_col - b_row + ig_row, NEG_BIG)
            m_t = jnp.maximum(b_col + m_prev, jnp.max(logw, axis=1, keepdims=True))
            w_prev = jnp.exp(b_col + m_prev - m_t)
            w = jnp.exp(logw - m_t)
            qb = q.astype(BF16)
            vb = v.astype(BF16)
            sc = lax.dot_general(qb, k.astype(BF16), (((1,), (1,)), ((), ())),
                                 preferred_element_type=F32) * w
            num = (w_prev * jnp.dot(qb, c_prev.astype(BF16), preferred_element_type=F32)
                   + jnp.dot(sc.astype(BF16), vb, preferred_element_type=F32))
            den = (w_prev * jnp.sum(q * n_prev, axis=1, keepdims=True)
                   + jnp.sum(sc, axis=1, keepdims=True))
            hval = num / jnp.maximum(jnp.abs(den), jnp.exp(-m_t))

            m_end = m_t[c - 1:c, :]
            b_end = b_col[c - 1:c, :]
            g_prev = jnp.exp(b_end + m_prev - m_end)
            w_in = jnp.exp(b_end - b_col + ig_col - m_end)
            kw = k * w_in
            c_ref[n, h] = g_prev * c_prev + lax.dot_general(
                kw.astype(BF16), vb, (((0,), (0,)), ((), ())), preferred_element_type=F32)
            n_ref[n, h] = g_prev * n_prev + jnp.sum(kw, axis=0, keepdims=True)
            m_ref[n, h] = jnp.broadcast_to(m_end, (1, LANES))
            hf_ref[rows, hs] = (hval * _rms_scale(hval) * nrm_ref[:, hs]
                                * jax.nn.sigmoid(o_ref[rows, hs]))
        return carry

    lax.fori_loop(0, nb * chunks_per_seq, chunk, 0, unroll=unroll)
    h_ref[...] = hf_ref[...].astype(BF16)


def _mlrec(q, k, v, gates, cols, nrm, inits, prevs, *, layer, depth, nseq, heads, row0, col0,
           nb, tl, c, unroll):
    width = q.shape[1]
    dh = width // heads
    seq = q.shape[0] // nseq
    nt = seq // tl
    blk = nb * tl
    rb0 = row0 // blk
    cb = col0 // width
    ng = gates.shape[1]
    qspec = pl.BlockSpec((blk, width), lambda b, t: (b * nt + t, 0))
    ospec = pl.BlockSpec((blk, width), lambda b, t: (rb0 + b * nt + t, 0))
    shapes = ((depth, nseq, heads, dh, dh), (depth, nseq, heads, 1, dh), (depth, nseq, heads, 1, LANES))
    sspecs = [_layer_state_spec(sh, nb, layer) for sh in shapes]
    args = [q, k, v, gates, cols, nrm.reshape(1, width)]
    st_specs, st_args, aliases = _state_args(sspecs, inits, prevs, len(args))
    return pl.pallas_call(
        functools.partial(_mlrec_body, nb=nb, tl=tl, c=c, heads=heads, unroll=unroll,
                          has_init=inits is not None, n_prev=len(aliases)),
        grid=(nseq // nb, nt),
        in_specs=[qspec, qspec, qspec,
                  pl.BlockSpec((blk, ng), lambda b, t: (b * nt + t, 0)),
                  pl.BlockSpec((blk, width), lambda b, t: (rb0 + b * nt + t, cb)),
                  pl.BlockSpec((1, width), lambda b, t: (0, 0))] + st_specs,
        out_specs=(ospec, *sspecs),
        out_shape=(jax.ShapeDtypeStruct((cols.shape[0], width), BF16),)
                  + tuple(jax.ShapeDtypeStruct(sh, F32) for sh in shapes),
        scratch_shapes=[pltpu.VMEM((blk, width), F32)],
        input_output_aliases=aliases,
        compiler_params=_cparams(("parallel", "arbitrary")),
        name="mlrec",
    )(*args, *st_args)


def _merge_body(x_ref, oa_ref, ob_ref, oc_ref, ga_ref, gb_ref, gc_ref, wa_ref, wba_ref, wbb_ref,
                wc_ref, wo_ref, g_ref, o_ref, acc_ref):
    j = pl.program_id(1)

    @pl.when(j == 0)
    def _():
        acc_ref[...] = jnp.zeros_like(acc_ref)

    yb = ob_ref[...]
    br_a = jnp.dot(oa_ref[...], wa_ref[...], preferred_element_type=F32)
    br_b = (jnp.dot(yb, wba_ref[...], preferred_element_type=F32)
            * jax.nn.sigmoid(jnp.dot(yb, wbb_ref[...], preferred_element_type=F32)))
    br_c = jnp.dot(oc_ref[...], wc_ref[...], preferred_element_type=F32)
    merged = (jax.nn.sigmoid(ga_ref[...]) * br_a + jax.nn.sigmoid(gb_ref[...]) * br_b
              + jax.nn.sigmoid(gc_ref[...]) * br_c)
    acc_ref[...] += jnp.dot(merged.astype(BF16), wo_ref[...], preferred_element_type=F32)

    @pl.when(j == pl.num_programs(1) - 1)
    def _():
        _postnorm_residual_to(o_ref, x_ref, acc_ref, g_ref, 1.0)


def _merge(x, oa, ob, oc, cols, w_a, w_ba, w_bb, w_c, w_out, gains, k_gain, *, layer, col0,
           tm=512, tn=512):
    m, d = x.shape
    mix_w = oa.shape[1]
    nj = d // tn
    gb0 = col0 // tn
    act = pl.BlockSpec((tm, mix_w), lambda i, j: (i, 0))
    wspec = pl.BlockSpec((None, mix_w, tn), lambda i, j: (layer, 0, j))

    def gate(branch):
        return pl.BlockSpec((tm, tn), lambda i, j: (i, gb0 + branch * nj + j))

    return pl.pallas_call(
        _merge_body,
        grid=(m // tm, nj),
        in_specs=[pl.BlockSpec((tm, d), lambda i, j: (i, 0)), act, act, act,
                  gate(0), gate(1), gate(2), wspec, wspec, wspec, wspec,
                  pl.BlockSpec((None, tn, d), lambda i, j: (layer, j, 0)),
                  _gain_spec(gains, layer, k_gain)],
        out_specs=pl.BlockSpec((tm, d), lambda i, j: (i, 0)),
        out_shape=jax.ShapeDtypeStruct((m, d), F32),
        scratch_shapes=[pltpu.VMEM((tm, d), F32)],
        compiler_params=_cparams(("parallel", "arbitrary")),
        name="merge",
    )(x, oa, ob, oc, cols, cols, cols, w_a, w_ba, w_bb, w_c, w_out, gains)


def kernel(x_prompt, x_sample, state_hgrn, state_s5_re, state_s5_im, state_mlstm_c, state_mlstm_n, state_mlstm_m, state_mlstm_conv, norm_gains, w_ffn1_up, w_ffn1_down, w_in, hgrn_lower_bounds, hgrn_norm, w_hgrn_out, s5_a_re, s5_a_im, s5_log_dt, s5_b_re, s5_b_im, s5_c_re, s5_c_im, s5_d, w_s5_glu_a, w_s5_glu_b, mlstm_conv_w, mlstm_conv_b, mlstm_wq, mlstm_wk, mlstm_wv, mlstm_w_gates, mlstm_b_gates, mlstm_norm, w_mlstm_out, w_out, w_ffn2_up, w_ffn2_down):
    batch, seq, d_model = x_prompt.shape
    dec_batch, dec_seq, _ = x_sample.shape
    depth = w_in.shape[0]
    mix_w = w_hgrn_out.shape[1]
    n_prompt = batch * seq
    heads_hg = state_hgrn.shape[2]
    groups, state = s5_a_re.shape[1:]
    slabs = groups * state // LANES

    x = (x_prompt.reshape(n_prompt, d_model), x_sample.reshape(dec_batch * dec_seq, d_model))

    lb_all, ab_re, ab_im, bb_re, bb_im = _prep(hgrn_lower_bounds, s5_a_re, s5_a_im, s5_log_dt,
                                               s5_b_re, s5_b_im)

    col_su, col_mx, col_mo, col_gz = 4 * mix_w, 5 * mix_w, 6 * mix_w, 7 * mix_w
    dh = mix_w // ML_HEADS

    sample_init = dict(
        hg=state_hgrn,
        s5=(state_s5_re.reshape(depth, dec_batch, slabs, LANES),
            state_s5_im.reshape(depth, dec_batch, slabs, LANES)),
        conv=state_mlstm_conv,
        ml=(state_mlstm_c, state_mlstm_n.reshape(depth, dec_batch, ML_HEADS, 1, dh),
            jnp.broadcast_to(state_mlstm_m[..., None, None], (depth, dec_batch, ML_HEADS, 1, LANES))))
    groups_cfg = (
        dict(row0=0, nseq=batch, seq=seq, init=dict(hg=None, s5=None, conv=None, ml=None),
             hg=dict(nb=1, tl=512, c=16), s5=dict(nb=1, tl=128), mp=dict(nb=1, tl=256),
             mr=dict(nb=1, tl=256, c=128, unroll=2)),
        dict(row0=n_prompt, nseq=dec_batch, seq=dec_seq, init=sample_init,
             hg=dict(nb=4, tl=dec_seq, c=dec_seq), s5=dict(nb=16, tl=dec_seq),
             mp=dict(nb=16, tl=dec_seq), mr=dict(nb=4, tl=dec_seq, c=dec_seq, unroll=2)),
    )
    new = [dict(hg=None, s5=(None, None), conv=None, ml=(None, None, None)) for _ in groups_cfg]

    gains = norm_gains.reshape(depth * N_GAINS, 1, d_model)
    bf = lambda w: w.astype(BF16)
    w_ffn1_up, w_ffn1_down, w_ffn2_up, w_ffn2_down = map(bf, (w_ffn1_up, w_ffn1_down, w_ffn2_up, w_ffn2_down))
    w_hgrn_out, w_s5_glu_a, w_s5_glu_b, w_mlstm_out, w_out = map(
        bf, (w_hgrn_out, w_s5_glu_a, w_s5_glu_b, w_mlstm_out, w_out))

    for l in range(depth):
        x = _ffn(x, gains, 0, 1, w_ffn1_up, w_ffn1_down, layer=l, rows0=n_prompt)
        cols = _inproj(x, gains, 2, w_in, layer=l)

        wb, wc = _s5_weights(bb_re[l], bb_im[l], s5_c_re[l], s5_c_im[l])
        abr = ab_re[l].reshape(slabs, LANES)
        abi = ab_im[l].reshape(slabs, LANES)
        s5d = s5_d[l].reshape(1, mix_w)
        wq, wk, wv = (w[l].astype(BF16) for w in (mlstm_wq, mlstm_wk, mlstm_wv))
        wg = mlstm_w_gates[l].astype(BF16)

        oa = ob = oc = None
        for cfg, st in zip(groups_cfg, new):
            common = dict(layer=l, nseq=cfg['nseq'], row0=cfg['row0'])
            init = cfg['init']
            oa, st['hg'] = _hgrn(cols, lb_all, hgrn_norm, init['hg'], (oa, st['hg']),
                                 heads=heads_hg, seq=cfg['seq'], **common, **cfg['hg'])
            ob, *st['s5'] = _s5(cols, wb, wc, abr, abi, s5d, init['s5'], (ob, *st['s5']),
                                depth=depth, col0=col_su, seq=cfg['seq'], **common, **cfg['s5'])
            q, k, v, gates, st['conv'] = _mlproj(
                cols, mlstm_conv_w[l], mlstm_conv_b[l], wq, wk, wv, wg, mlstm_b_gates[l],
                init['conv'], st['conv'], depth=depth, col0=col_mx, seq=cfg['seq'], **common,
                **cfg['mp'])
            oc, *st['ml'] = _mlrec(q, k, v, gates, cols, mlstm_norm[l], init['ml'], (oc, *st['ml']),
                                   depth=depth, heads=ML_HEADS, col0=col_mo, **common, **cfg['mr'])

        x = _merge(x, oa, ob, oc, cols, w_hgrn_out, w_s5_glu_a, w_s5_glu_b, w_mlstm_out, w_out,
                   gains, 3, layer=l, col0=col_gz)
        x = _ffn(x, gains, 4, 5, w_ffn2_up, w_ffn2_down, layer=l, rows0=n_prompt,
                 split_out=l == depth - 1)

    y_prompt = x[0].reshape(batch, seq, d_model)
    y_sample = x[1].reshape(dec_batch, dec_seq, d_model)
    states = ()
    for cfg, st in zip(groups_cfg, new):
        nseq = cfg['nseq']
        c_new, n_new, m_new = st['ml']
        states += (st['hg'],
                   st['s5'][0].reshape(depth, nseq, groups, state),
                   st['s5'][1].reshape(depth, nseq, groups, state),
                   c_new, n_new.reshape(depth, nseq, ML_HEADS, dh), m_new[:, :, :, 0, 0], st['conv'])
    return (y_prompt, y_sample) + states
```

```python
import functools
import math

import jax
import jax.numpy as jnp
from jax import lax
from jax.experimental import pallas as pl
from jax.experimental.pallas import tpu as pltpu

F32 = jnp.float32
BF16 = jnp.bfloat16

EPS = 1e-6
NEG_BIG = -1e30
S5_DT_MIN_CLAMP = -1e-4

HG_DK = 128
S5_CH = 16
S5_SUPER = 8
ML_HEADS = 4
CONV_W = 4
N_GAINS = 6

LANES = 128
SUBLANES = 8
VMEM_LIMIT = 56 * 1024 * 1024


def _cparams(sem):
    return pltpu.CompilerParams(dimension_semantics=sem, vmem_limit_bytes=VMEM_LIMIT)


def _rms_scale(y):
    return lax.rsqrt(jnp.mean(y * y, axis=-1, keepdims=True) + EPS)


def _silu(x):
    return x * jax.nn.sigmoid(x)


NORM_ROWS = 16


def _for_row_chunks(n_rows, body):
    def step(r, carry):
        body(pl.ds(pl.multiple_of(r * NORM_ROWS, NORM_ROWS), NORM_ROWS))
        return carry

    lax.fori_loop(0, n_rows // NORM_ROWS, step, 0, unroll=8)


def _prenorm_to(h_ref, x_ref, g_ref):
    g = g_ref[...]

    def body(rows):
        x = x_ref[rows, :]
        h_ref[rows, :] = (x * _rms_scale(x) * g).astype(BF16)

    _for_row_chunks(x_ref.shape[0], body)


def _postnorm_residual_to(o_ref, x_ref, y_ref, g_ref, weight):
    g = g_ref[...] if weight == 1.0 else g_ref[...] * weight

    def body(rows):
        y = y_ref[rows, :]
        o_ref[rows, :] = x_ref[rows, :] + y * _rms_scale(y) * g

    _for_row_chunks(x_ref.shape[0], body)


def _ffn_body(*refs, n_x, n_out, tiles0):
    x_refs = refs[:n_x]
    gpre_ref, gpost_ref, wa_ref, wb_ref, wd_ref = refs[n_x:n_x + 5]
    o_refs = refs[n_x + 5:n_x + 5 + n_out]
    h_ref, acc_ref = refs[n_x + 5 + n_out:]
    i = pl.program_id(0)
    j = pl.program_id(1)
    in_part = (i < tiles0, i >= tiles0)

    for part in range(n_x):
        @pl.when((j == 0) & in_part[part] if n_x == 2 else j == 0)
        def _():
            _prenorm_to(h_ref, x_refs[part], gpre_ref)

    @pl.when(j == 0)
    def _():
        acc_ref[...] = jnp.zeros_like(acc_ref)

    h = h_ref[...]
    a = jnp.dot(h, wa_ref[...], preferred_element_type=F32)
    b = jnp.dot(h, wb_ref[...], preferred_element_type=F32)
    act = (_silu(a) * b).astype(BF16)
    acc_ref[...] += jnp.dot(act, wd_ref[...], preferred_element_type=F32)

    last = j == pl.num_programs(1) - 1
    split = max(n_x, n_out) == 2
    for part in range(2 if split else 1):
        @pl.when(last & in_part[part] if split else last)
        def _():
            _postnorm_residual_to(o_refs[part if n_out == 2 else 0], x_refs[part if n_x == 2 else 0],
                                  acc_ref, gpost_ref, 0.5)


def _gain_spec(gains, layer, k):
    return pl.BlockSpec((None, 1, gains.shape[-1]), lambda i, j: (layer * N_GAINS + k, 0, 0))


def _ffn(x, gains, k_pre, k_post, w_up, w_down, *, layer, rows0, split_out=False, tm=512, tf=512):
    xs = x if isinstance(x, tuple) else (x,)
    d = xs[0].shape[1]
    m = sum(a.shape[0] for a in xs)
    d_ff = w_down.shape[1]
    nf = d_ff // tf
    tiles0 = rows0 // tm
    whole = pl.BlockSpec((tm, d), lambda i, j: (i, 0))
    parts = [pl.BlockSpec((tm, d), lambda i, j: (jnp.minimum(i, tiles0 - 1), 0)),
             pl.BlockSpec((tm, d), lambda i, j: (jnp.maximum(i - tiles0, 0), 0))]
    part_shapes = [jax.ShapeDtypeStruct((rows0, d), F32), jax.ShapeDtypeStruct((m - rows0, d), F32)]
    out = pl.pallas_call(
        functools.partial(_ffn_body, n_x=len(xs), n_out=2 if split_out else 1, tiles0=tiles0),
        grid=(m // tm, nf),
        in_specs=(parts if len(xs) == 2 else [whole]) + [
            _gain_spec(gains, layer, k_pre),
            _gain_spec(gains, layer, k_post),
            pl.BlockSpec((None, d, tf), lambda i, j: (layer, 0, j)),
            pl.BlockSpec((None, d, tf), lambda i, j: (layer, 0, j + nf)),
            pl.BlockSpec((None, tf, d), lambda i, j: (layer, j, 0)),
        ],
        out_specs=parts if split_out else whole,
        out_shape=part_shapes if split_out else jax.ShapeDtypeStruct((m, d), F32),
        scratch_shapes=[pltpu.VMEM((tm, d), BF16), pltpu.VMEM((tm, d), F32)],
        compiler_params=_cparams(("parallel", "arbitrary")),
        name="ffn",
    )(*xs, gains, gains, w_up, w_up, w_down)
    return tuple(out) if split_out else out


def _inproj_body(x_ref, g_ref, w_ref, o_ref, h_ref):
    @pl.when(pl.program_id(1) == 0)
    def _():
        _prenorm_to(h_ref, x_ref, g_ref)

    o_ref[...] = jnp.dot(h_ref[...], w_ref[...].astype(BF16), preferred_element_type=F32)


def _inproj(x, gains, k_gain, w, *, layer, tm=1024, tn=1024):
    m, d = x.shape
    n = w.shape[2]
    return pl.pallas_call(
        _inproj_body,
        grid=(m // tm, n // tn),
        in_specs=[
            pl.BlockSpec((tm, d), lambda i, j: (i, 0)),
            _gain_spec(gains, layer, k_gain),
            pl.BlockSpec((None, d, tn), lambda i, j: (layer, 0, j)),
        ],
        out_specs=pl.BlockSpec((tm, tn), lambda i, j: (i, j)),
        out_shape=jax.ShapeDtypeStruct((m, n), F32),
        scratch_shapes=[pltpu.VMEM((tm, d), BF16)],
        compiler_params=_cparams(("parallel", "arbitrary")),
        name="inproj",
    )(x, gains, w)


def _prep_body(lbraw_ref, are_ref, aim_ref, ldt_ref, bre_ref, bim_ref,
               lb_ref, abr_ref, abi_ref, bbr_ref, bbi_ref):
    depth = lbraw_ref.shape[0]
    raw = lbraw_ref[...]
    e = jnp.exp(raw - jnp.max(raw, axis=0, keepdims=True))
    lbs = e / jnp.sum(e, axis=0, keepdims=True)
    run = jnp.zeros_like(lbs[0:1])
    for l in range(depth):
        run = run + lbs[l:l + 1]
        lb_ref[l:l + 1, :] = run - lbs[0:1]

    for l in range(depth):
        dt = jnp.exp(ldt_ref[l])
        lam_re = jnp.minimum(are_ref[l], S5_DT_MIN_CLAMP)
        lam_im = aim_ref[l]
        mag = jnp.exp(lam_re * dt)
        ab_re = mag * jnp.cos(lam_im * dt)
        ab_im = mag * jnp.sin(lam_im * dt)
        inv = 1.0 / (lam_re * lam_re + lam_im * lam_im)
        f_re = ((ab_re - 1.0) * lam_re + ab_im * lam_im) * inv
        f_im = (ab_im * lam_re - (ab_re - 1.0) * lam_im) * inv
        abr_ref[l] = ab_re
        abi_ref[l] = ab_im
        b_re = bre_ref[l]
        b_im = bim_ref[l]
        bbr_ref[l] = f_re * b_re - f_im * b_im
        bbi_ref[l] = f_re * b_im + f_im * b_re


def _prep(lb_raw, a_re, a_im, log_dt, b_re, b_im):
    depth, groups, state = a_re.shape
    gp = groups * state
    ch = b_re.shape[-1]
    row = lambda a: a.reshape(depth, 1, gp)
    ldt = jnp.broadcast_to(log_dt[:, :, None], (depth, groups, state))
    chan_major = lambda b: b.transpose(0, 3, 1, 2).reshape(depth, ch, gp)
    out_shape = (
        jax.ShapeDtypeStruct(lb_raw.shape, F32),
        jax.ShapeDtypeStruct((depth, 1, gp), F32),
        jax.ShapeDtypeStruct((depth, 1, gp), F32),
        jax.ShapeDtypeStruct((depth, ch, gp), F32),
        jax.ShapeDtypeStruct((depth, ch, gp), F32),
    )
    return pl.pallas_call(_prep_body, out_shape=out_shape, name="prep")(
        lb_raw, row(a_re), row(a_im), row(ldt), chan_major(b_re), chan_major(b_im))


def _hgrn_stage_shapes(heads, c):
    return [pltpu.VMEM((heads, c * c, HG_DK), BF16), pltpu.VMEM((heads * c * c, LANES), F32),
            pltpu.VMEM((heads, c, HG_DK), BF16), pltpu.VMEM((heads, c, HG_DK), BF16),
            pltpu.VMEM((heads, c, HG_DK), BF16), pltpu.VMEM((heads, 1, HG_DK), F32),
            pltpu.VMEM((heads, c, HG_DK), F32)]


def _hgrn_run(q_ref, f_ref, i_ref, g_ref, lb_ref, nrm_ref, st_ref, of_ref, o_ref, stage,
              *, nb, tl, c, heads):
    prod_ref, sums_ref, qc_ref, ke_ref, vb_ref, ce_ref, gs_ref = stage
    row = lax.broadcasted_iota(jnp.int32, (c, HG_DK), 0)
    lag = row - lax.broadcasted_iota(jnp.int32, (c, HG_DK), 1)
    ones = jnp.ones((HG_DK, LANES), BF16)
    chunks_per_seq = tl // c
    n_steps = nb * chunks_per_seq
    head_cols = [slice(h * HG_DK, (h + 1) * HG_DK) for h in range(heads)]

    def chunk_rows(idx):
        return pl.ds(pl.multiple_of(idx * c, c), c)

    def prepare(idx):
        rows = chunk_rows(idx)
        for h, hs in enumerate(head_cols):
            lb = lb_ref[:, hs]
            fpre = f_ref[rows, hs]
            q = _silu(q_ref[rows, hs])
            forget = lb + (1.0 - lb) * jax.nn.sigmoid(fpre)
            kin = (1.0 - lb) * jax.nn.sigmoid(-fpre)

            cp = forget
            sh = 1
            while sh < c:
                cp = cp * jnp.where(row >= sh, pltpu.roll(cp, sh, 0), 1.0)
                sh *= 2
            sp = jnp.where(row < c - 1, pltpu.roll(forget, c - 1, 0), 1.0)
            sh = 1
            while sh < c:
                sp = sp * jnp.where(row < c - sh, pltpu.roll(sp, c - sh, 0), 1.0)
                sh *= 2

            kd = kin
            prods = [q * kd]
            for d in range(1, c):
                kd = pltpu.roll(kd, 1, 0) * forget
                prods.append(q * kd)
            prod_ref[h] = jnp.concatenate(prods, axis=0).astype(BF16)
            qc_ref[h] = (q * cp).astype(BF16)
            ke_ref[h] = (kin * sp).astype(BF16)
            vb_ref[h] = i_ref[rows, hs].astype(BF16)
            ce_ref[h] = cp[c - 1:c, :]
            gs_ref[h] = nrm_ref[:, hs] * _silu(g_ref[rows, hs])

    def finish(idx):
        n = idx // chunks_per_seq
        rows = chunk_rows(idx)
        sums_ref[...] = jnp.dot(prod_ref[...].reshape(heads * c * c, HG_DK), ones,
                                preferred_element_type=F32)
        inter = []
        for h in range(heads):
            st = st_ref[n, h]
            vb = vb_ref[h]
            inter.append(lax.dot_general(qc_ref[h], st.astype(BF16), (((1,), (1,)), ((), ())),
                                         preferred_element_type=F32))
            upd = lax.dot_general(vb, ke_ref[h], (((0,), (0,)), ((), ())),
                                  preferred_element_type=F32)
            st_ref[n, h] = st * ce_ref[h] + upd
        outs = []
        for h in range(heads):
            scores = jnp.zeros((c, HG_DK), F32)
            for d in range(c):
                scores = jnp.where(lag == d, sums_ref[pl.ds((h * c + d) * c, c), :], scores)
            outs.append(inter[h] + jnp.dot(scores[:, :c].astype(BF16), vb_ref[h],
                                           preferred_element_type=F32))
        for hs, h in zip(head_cols, range(heads)):
            of_ref[rows, hs] = outs[h] * _rms_scale(outs[h]) * gs_ref[h]

    def step(idx, carry):
        finish(idx - 1)
        prepare(idx)
        return carry

    prepare(0)
    lax.fori_loop(1, n_steps, step, 0, unroll=2)
    finish(n_steps - 1)
    o_ref[...] = of_ref[...].astype(BF16)


def _hgrn_body(*refs, heads, n_prev, long_cfg, short_cfg):
    long_in, short_in = refs[0:4], refs[4:8]
    lb_ref, nrm_ref, s0_ref = refs[8:11]
    ol_ref, os_ref, sl_ref, ss_ref, stl_ref, ofl_ref, sts_ref, ofs_ref = refs[11 + n_prev:19 + n_prev]
    stage = refs[19 + n_prev:]
    stage_long, stage_short = stage[:len(stage) // 2], stage[len(stage) // 2:]
    t = pl.program_id(1)
    nb_s = short_cfg['nb']

    @pl.when(t == 0)
    def _():
        for h in range(heads):
            stl_ref[0, h] = jnp.zeros(stl_ref.shape[2:], F32)

    _hgrn_run(*long_in, lb_ref, nrm_ref, stl_ref, ofl_ref, ol_ref, stage_long, heads=heads, **long_cfg)

    @pl.when(t == pl.num_programs(1) - 1)
    def _():
        for h in range(heads):
            sl_ref[0, h] = stl_ref[0, h].T

    def load(n, carry):
        for h in range(heads):
            sts_ref[n, h] = s0_ref[n, h].T
        return carry

    lax.fori_loop(0, nb_s, load, 0)
    _hgrn_run(*short_in, lb_ref, nrm_ref, sts_ref, ofs_ref, os_ref, stage_short, heads=heads,
              **short_cfg)

    def store(n, carry):
        for h in range(heads):
            ss_ref[n, h] = sts_ref[n, h].T
        return carry

    lax.fori_loop(0, nb_s, store, 0)


def _layer_state_spec(shape, nb, layer):
    tail = (0,) * (len(shape) - 2)
    return pl.BlockSpec((None, nb) + tuple(shape[2:]), lambda b, t: (layer, b) + tail)


def _hgrn(cols, lb, nrm, short_init, prevs, *, layer, heads, batch, seq, tl, c,
          short_nseq, short_seq, short_row0):
    depth = lb.shape[0]
    dk = HG_DK
    mix_w = heads * dk
    dv = mix_w // heads
    nt = seq // tl
    nb_s = short_nseq // (batch * nt)
    blk_s = nb_s * short_seq
    rb0_s = short_row0 // blk_s
    step = lambda b, t: b * nt + t
    long_shape = (depth, batch, heads, dk, dv)
    short_shape = (depth, short_nseq, heads, dk, dv)

    def colspecs(blk, rb0):
        return [pl.BlockSpec((blk, mix_w), lambda b, t, g=g: (rb0 + step(b, t), g)) for g in range(4)]

    vec = pl.BlockSpec((None, 1, mix_w), lambda b, t: (layer, 0, 0))
    long_spec = _layer_state_spec(long_shape, 1, layer)
    short_spec = pl.BlockSpec((None, nb_s, heads, dk, dv), lambda b, t: (layer, step(b, t), 0, 0, 0))
    args = [cols] * 8 + [lb.reshape(depth, 1, mix_w), nrm.reshape(depth, 1, mix_w)]
    st_specs, st_args, aliases = _state_args([short_spec], (short_init,), (None, None) + tuple(prevs),
                                             len(args))
    out = pl.pallas_call(
        functools.partial(_hgrn_body, heads=heads, n_prev=len(aliases),
                          long_cfg=dict(nb=1, tl=tl, c=c),
                          short_cfg=dict(nb=nb_s, tl=short_seq, c=short_seq)),
        grid=(batch, nt),
        in_specs=colspecs(tl, 0) + colspecs(blk_s, rb0_s) + [vec, vec] + st_specs,
        out_specs=(pl.BlockSpec((tl, mix_w), lambda b, t: (step(b, t), 0)),
                   pl.BlockSpec((blk_s, mix_w), lambda b, t: (step(b, t), 0)),
                   long_spec, short_spec),
        out_shape=(jax.ShapeDtypeStruct((cols.shape[0], mix_w), BF16),
                   jax.ShapeDtypeStruct((short_nseq * short_seq, mix_w), BF16),
                   jax.ShapeDtypeStruct(long_shape, F32), jax.ShapeDtypeStruct(short_shape, F32)),
        scratch_shapes=[pltpu.VMEM((1, heads, dv, dk), F32), pltpu.VMEM((tl, mix_w), F32),
                        pltpu.VMEM((nb_s, heads, dv, dk), F32), pltpu.VMEM((blk_s, mix_w), F32)]
                       + _hgrn_stage_shapes(heads, c) + _hgrn_stage_shapes(heads, short_seq),
        input_output_aliases=aliases,
        compiler_params=_cparams(("parallel", "arbitrary")),
        name="hgrn",
    )(*args, *st_args)
    return out


def _gelu_tanh(x):
    return 0.5 * x * (1.0 + jnp.tanh(math.sqrt(2.0 / math.pi) * (x + 0.044715 * (x * x * x))))


def _state_args(specs, inits, prevs, n_args):
    in_specs, args, aliases = [], [], {}
    if inits is not None:
        in_specs += list(specs)
        args += list(inits)
    for out_idx, prev in enumerate(prevs):
        if prev is not None:
            aliases[n_args + len(args)] = out_idx
            in_specs.append(pl.BlockSpec(memory_space=pl.ANY))
            args.append(prev)
    return in_specs, args, aliases


def _s5_body(*refs, nb, tl, has_init, n_prev):
    u_ref, wb_ref, wc_ref, abr_ref, abi_ref, d_ref = refs[:6]
    y_ref, xr_ref, xi_ref, sr_ref, si_ref = refs[6 + 2 * has_init + n_prev:]
    rows = nb * tl
    pitch = rows + SUBLANES
    nsb = wb_ref.shape[0]
    cw = wb_ref.shape[1]
    half = wb_ref.shape[2] // 2
    per = half // LANES
    slabs = nsb * per
    t = pl.program_id(1)

    @pl.when(t == 0)
    def _():
        if has_init:
            xr_ref[...] = refs[6][...]
            xi_ref[...] = refs[7][...]
        else:
            xr_ref[...] = jnp.zeros_like(xr_ref)
            xi_ref[...] = jnp.zeros_like(xi_ref)

    for sb in range(nsb):
        ub = u_ref[:, sb * cw:(sb + 1) * cw].astype(BF16)
        res = jnp.dot(ub, wb_ref[sb], preferred_element_type=F32)
        for j in range(per):
            base = (sb * per + j) * pitch
            sr_ref[pl.ds(base, rows), :] = res[:, j * LANES:(j + 1) * LANES]
            si_ref[pl.ds(base, rows), :] = res[:, half + j * LANES:half + (j + 1) * LANES]

    ar = abr_ref[...]
    ai = abi_ref[...]

    def advance(n, tt, xr, xi):
        at = pl.ds(n * tl + tt, slabs, stride=pitch)
        nxr = ar * xr - ai * xi + sr_ref[at, :]
        nxi = ar * xi + ai * xr + si_ref[at, :]
        sr_ref[at, :] = nxr
        si_ref[at, :] = nxi
        return nxr, nxi

    par = 2 if nb % 2 == 0 else 1

    def group_scan(p, carry):
        seqs = [par * p + i for i in range(par)]

        def step(tt, x):
            out = ()
            for i, n in enumerate(seqs):
                out += advance(n, tt, x[2 * i], x[2 * i + 1])
            return out

        x0 = ()
        for n in seqs:
            x0 += (xr_ref[n], xi_ref[n])
        x = lax.fori_loop(0, tl, step, x0, unroll=SUBLANES)
        for i, n in enumerate(seqs):
            xr_ref[n] = x[2 * i]
            xi_ref[n] = x[2 * i + 1]
        return carry

    lax.fori_loop(0, nb // par, group_scan, 0)

    for sb in range(nsb):
        parts = [sr_ref[pl.ds((sb * per + j) * pitch, rows), :] for j in range(per)]
        parts += [si_ref[pl.ds((sb * per + j) * pitch, rows), :] for j in range(per)]
        xs = jnp.concatenate(parts, axis=1).astype(BF16)
        cs = slice(sb * cw, (sb + 1) * cw)
        y = jnp.dot(xs, wc_ref[sb], preferred_element_type=F32) + d_ref[:, cs] * u_ref[:, cs]
        y_ref[:, cs] = _gelu_tanh(y).astype(BF16)


def _s5(cols, wb, wc, ab_re, ab_im, d, inits, prevs, *, layer, depth, nseq, row0, col0, seq, nb, tl):
    slabs = ab_re.shape[0]
    nt = seq // tl
    blk = nb * tl
    rb0 = row0 // blk
    mix_w = d.shape[-1]
    cb = col0 // mix_w
    const3 = lambda a: pl.BlockSpec(a.shape, lambda b, t: (0, 0, 0))
    const2 = lambda a: pl.BlockSpec(a.shape, lambda b, t: (0, 0))
    sshape = (depth, nseq, slabs, LANES)
    xspec = _layer_state_spec(sshape, nb, layer)
    pitch = blk + SUBLANES
    args = [cols, wb, wc, ab_re, ab_im, d]
    st_specs, st_args, aliases = _state_args([xspec, xspec], inits, prevs, len(args))
    return pl.pallas_call(
        functools.partial(_s5_body, nb=nb, tl=tl, has_init=inits is not None, n_prev=len(aliases)),
        grid=(nseq // nb, nt),
        in_specs=[pl.BlockSpec((blk, mix_w), lambda b, t: (rb0 + b * nt + t, cb)),
                  const3(wb), const3(wc), const2(ab_re), const2(ab_im), const2(d)] + st_specs,
        out_specs=(pl.BlockSpec((blk, mix_w), lambda b, t: (rb0 + b * nt + t, 0)), xspec, xspec),
        out_shape=(jax.ShapeDtypeStruct((cols.shape[0], mix_w), BF16),
                   jax.ShapeDtypeStruct(sshape, F32),
                   jax.ShapeDtypeStruct(sshape, F32)),
        scratch_shapes=[pltpu.VMEM((slabs * pitch, LANES), F32),
                        pltpu.VMEM((slabs * pitch, LANES), F32)],
        input_output_aliases=aliases,
        compiler_params=_cparams(("parallel", "arbitrary")),
        name="s5",
    )(*args, *st_args)


def _s5_weights(bb_re, bb_im, c_re, c_im):
    ch, gp = bb_re.shape
    groups = c_re.shape[0]
    state = gp // groups
    nsb = groups // S5_SUPER
    eye = jnp.eye(S5_SUPER, dtype=F32)

    def in_w(bb):
        b4 = bb.reshape(ch, nsb, S5_SUPER, state)
        return jnp.einsum('csgp,gh->sgchp', b4, eye).reshape(nsb, S5_SUPER * ch, S5_SUPER * state)

    def out_w(c):
        c4 = c.reshape(nsb, S5_SUPER, ch, state)
        return jnp.einsum('sgcp,gh->sgphc', c4, eye).reshape(nsb, S5_SUPER * state, S5_SUPER * ch)

    wb = jnp.concatenate([in_w(bb_re), in_w(bb_im)], axis=2).astype(BF16)
    wc = jnp.concatenate([out_w(c_re), -out_w(c_im)], axis=1).astype(BF16)
    return wb, wc


def _mlproj_body(*refs, nb, tl, has_init, n_prev):
    x_ref, cw_ref, cb_ref, wq_ref, wk_ref, wv_ref, wg_ref, bg_ref = refs[:8]
    q_ref, k_ref, v_ref, g_ref, cn_ref, xx_ref = refs[8 + has_init + n_prev:]
    t = pl.program_id(1)
    width = x_ref.shape[1]
    halo = CONV_W - 1
    lo = SUBLANES - halo

    @pl.when(t == 0)
    def _():
        if has_init:
            xx_ref[:, lo:SUBLANES, :] = refs[8][...]
        else:
            xx_ref[:, lo:SUBLANES, :] = jnp.zeros((nb, halo, width), F32)

    @pl.when(t > 0)
    def _():
        xx_ref[:, lo:SUBLANES, :] = xx_ref[:, tl + lo:tl + SUBLANES, :]

    x = x_ref[...]
    xx_ref[:, SUBLANES:, :] = x.reshape(nb, tl, width)
    xc = jnp.zeros((nb, tl, width), F32) + cb_ref[...][None]
    for j in range(CONV_W):
        xc = xc + xx_ref[:, lo + j:lo + j + tl, :] * cw_ref[j:j + 1, :][None]
    cn_ref[...] = xx_ref[:, tl + lo:tl + SUBLANES, :]
    xc = _silu(xc).reshape(nb * tl, width).astype(BF16)
    xb = x.astype(BF16)

    dh = width // ML_HEADS
    gates = jnp.zeros((nb * tl, 2 * ML_HEADS), F32) + bg_ref[...]
    for h in range(ML_HEADS):
        hs = slice(h * dh, (h + 1) * dh)
        q = jnp.dot(xc[:, hs], wq_ref[h], preferred_element_type=F32)
        k = jnp.dot(xc[:, hs], wk_ref[h], preferred_element_type=F32)
        v = jnp.dot(xb[:, hs], wv_ref[h], preferred_element_type=F32)
        q_ref[:, hs] = q
        k_ref[:, hs] = k
        v_ref[:, hs] = v
        for part, val in enumerate((q, k, v)):
            w = wg_ref[part * width + h * dh:part * width + (h + 1) * dh, :]
            gates = gates + jnp.dot(val.astype(BF16), w, preferred_element_type=F32)
    g_ref[...] = gates


def _mlproj(cols, conv_w, conv_b, wq, wk, wv, wg, bg, init, prev, *, layer, depth, nseq, row0, col0,
            seq, nb, tl):
    width = conv_w.shape[1]
    halo = CONV_W - 1
    nt = seq // tl
    blk = nb * tl
    rb0 = row0 // blk
    cb = col0 // width
    ng = wg.shape[1]
    full = lambda a: pl.BlockSpec(a.shape, lambda b, t: (0,) * a.ndim)
    rowspec = lambda w: pl.BlockSpec((blk, w), lambda b, t: (b * nt + t, 0))
    sshape = (depth, nseq, halo, width)
    cspec = _layer_state_spec(sshape, nb, layer)
    args = [cols, conv_w, conv_b.reshape(1, width), wq, wk, wv, wg, bg.reshape(1, ng)]
    st_specs, st_args, aliases = _state_args(
        [cspec], None if init is None else (init,), (None, None, None, None, prev), len(args))
    return pl.pallas_call(
        functools.partial(_mlproj_body, nb=nb, tl=tl, has_init=init is not None,
                          n_prev=len(aliases)),
        grid=(nseq // nb, nt),
        in_specs=[pl.BlockSpec((blk, width), lambda b, t: (rb0 + b * nt + t, cb))]
                 + [full(a) for a in args[1:]] + st_specs,
        out_specs=(rowspec(width), rowspec(width), rowspec(width), rowspec(ng), cspec),
        out_shape=(jax.ShapeDtypeStruct((nseq * seq, width), F32),) * 3
                  + (jax.ShapeDtypeStruct((nseq * seq, ng), F32),
                     jax.ShapeDtypeStruct(sshape, F32)),
        scratch_shapes=[pltpu.VMEM((nb, tl + SUBLANES, width), F32)],
        input_output_aliases=aliases,
        compiler_params=_cparams(("parallel", "arbitrary")),
        name="mlproj",
    )(*args, *st_args)


def _mlrec_run(q_ref, k_ref, v_ref, g_ref, o_ref, nrm_ref, c_ref, n_ref, m_ref, hf_ref, h_ref,
               *, nb, tl, c, heads, unroll):
    dh = q_ref.shape[1] // heads
    glane = lax.broadcasted_iota(jnp.int32, (c, 2 * heads), 1)
    r = lax.broadcasted_iota(jnp.int32, (c, c), 0)
    s = lax.broadcasted_iota(jnp.int32, (c, c), 1)
    causal = r >= s
    chunks_per_seq = tl // c
    k_scale = dh ** -0.5

    def chunk(idx, carry):
        n = idx // chunks_per_seq
        rows = pl.ds(pl.multiple_of(idx * c, c), c)
        gt = g_ref[rows, :]
        for h in range(heads):
            hs = slice(h * dh, (h + 1) * dh)
            q = q_ref[rows, hs]
            k = k_ref[rows, hs] * k_scale
            v = v_ref[rows, hs]
            ig_col = jnp.sum(jnp.where(glane == h, gt, 0.0), axis=1, keepdims=True)
            fp_col = jnp.sum(jnp.where(glane == h + heads, gt, 0.0), axis=1, keepdims=True)
            lf_col = jnp.minimum(fp_col, 0.0) - jnp.log1p(jnp.exp(-jnp.abs(fp_col)))
            lf_row = jnp.sum(jnp.where(r == s, lf_col, 0.0), axis=0, keepdims=True)
            ig_row = jnp.sum(jnp.where(r == s, ig_col, 0.0), axis=0, keepdims=True)
            b_col = jnp.sum(jnp.where(causal, lf_row, 0.0), axis=1, keepdims=True)
            b_row = jnp.sum(jnp.where(r <= s, lf_col, 0.0), axis=0, keepdims=True)

            m_prev = m_ref[n, h][:, 0:1]
            n_prev = n_ref[n, h]
            c_prev = c_ref[n, h]

            logw = jnp.where(causal, b_col - b_row + ig_row, NEG_BIG)
            m_t = jnp.maximum(b_col + m_prev, jnp.max(logw, axis=1, keepdims=True))
            w_prev = jnp.exp(b_col + m_prev - m_t)
            w = jnp.exp(logw - m_t)
            qb = q.astype(BF16)
            vb = v.astype(BF16)
            sc = lax.dot_general(qb, k.astype(BF16), (((1,), (1,)), ((), ())),
                                 preferred_element_type=F32) * w
            num = (w_prev * jnp.dot(qb, c_prev.astype(BF16), preferred_element_type=F32)
                   + jnp.dot(sc.astype(BF16), vb, preferred_element_type=F32))
            den = (w_prev * jnp.sum(q * n_prev, axis=1, keepdims=True)
                   + jnp.sum(sc, axis=1, keepdims=True))
            hval = num / jnp.maximum(jnp.abs(den), jnp.exp(-m_t))

            m_end = m_t[c - 1:c, :]
            b_end = b_col[c - 1:c, :]
            g_prev = jnp.exp(b_end + m_prev - m_end)
            w_in = jnp.exp(b_end - b_col + ig_col - m_end)
            kw = k * w_in
            c_ref[n, h] = g_prev * c_prev + lax.dot_general(
                kw.astype(BF16), vb, (((0,), (0,)), ((), ())), preferred_element_type=F32)
            n_ref[n, h] = g_prev * n_prev + jnp.sum(kw, axis=0, keepdims=True)
            m_ref[n, h] = jnp.broadcast_to(m_end, (1, LANES))
            hf_ref[rows, hs] = (hval * _rms_scale(hval) * nrm_ref[:, hs]
                                * jax.nn.sigmoid(o_ref[rows, hs]))
        return carry

    lax.fori_loop(0, nb * chunks_per_seq, chunk, 0, unroll=unroll)
    h_ref[...] = hf_ref[...].astype(BF16)


def _mlrec_body(*refs, heads, n_prev, long_cfg, short_cfg):
    long_in, short_in = refs[0:5], refs[5:10]
    nrm_ref = refs[10]
    c0_ref, n0_ref, m0_ref = refs[11:14]
    (hl_ref, hs_ref, cl_ref, nl_ref, ml_ref, cs_ref, ns_ref, ms_ref, hfl_ref, hfs_ref
     ) = refs[14 + n_prev:]

    @pl.when(pl.program_id(1) == 0)
    def _():
        cl_ref[...] = jnp.zeros_like(cl_ref)
        nl_ref[...] = jnp.zeros_like(nl_ref)
        ml_ref[...] = jnp.zeros_like(ml_ref)

    _mlrec_run(*long_in, nrm_ref, cl_ref, nl_ref, ml_ref, hfl_ref, hl_ref, heads=heads, **long_cfg)
    cs_ref[...] = c0_ref[...]
    ns_ref[...] = n0_ref[...]
    ms_ref[...] = m0_ref[...]
    _mlrec_run(*short_in, nrm_ref, cs_ref, ns_ref, ms_ref, hfs_ref, hs_ref, heads=heads, **short_cfg)


def _mlrec(long_qkvg, short_qkvg, cols, nrm, short_init, prevs, *, layer, depth, heads, col0,
           batch, seq, tl, c, short_nseq, short_seq, short_row0, unroll):
    width = long_qkvg[0].shape[1]
    dh = width // heads
    nt = seq // tl
    steps = batch * nt
    nb_s = short_nseq // steps
    blk_s = nb_s * short_seq
    cb = col0 // width
    ng = long_qkvg[3].shape[1]
    rb0_s = short_row0 // blk_s
    step = lambda b, t: b * nt + t

    def group_specs(blk, rb0):
        rows = pl.BlockSpec((blk, width), lambda b, t: (step(b, t), 0))
        return [rows, rows, rows,
                pl.BlockSpec((blk, ng), lambda b, t: (step(b, t), 0)),
                pl.BlockSpec((blk, width), lambda b, t: (rb0 + step(b, t), cb))]

    def state_shapes(nseq):
        return ((depth, nseq, heads, dh, dh), (depth, nseq, heads, 1, dh), (depth, nseq, heads, 1, LANES))

    long_specs = [_layer_state_spec(sh, 1, layer) for sh in state_shapes(batch)]
    short_specs = [pl.BlockSpec((None, nb_s) + sh[2:],
                                lambda b, t, nd=len(sh): (layer, step(b, t)) + (0,) * (nd - 2))
                   for sh in state_shapes(short_nseq)]
    args = [*long_qkvg, cols, *short_qkvg, cols, nrm.reshape(1, width)]
    st_specs, st_args, aliases = _state_args(short_specs, short_init, (None, None) + tuple(prevs),
                                             len(args))
    out = pl.pallas_call(
        functools.partial(_mlrec_body, heads=heads, n_prev=len(aliases),
                          long_cfg=dict(nb=1, tl=tl, c=c, unroll=unroll),
                          short_cfg=dict(nb=nb_s, tl=short_seq, c=short_seq, unroll=unroll)),
        grid=(batch, nt),
        in_specs=group_specs(tl, 0) + group_specs(blk_s, rb0_s)
                 + [pl.BlockSpec((1, width), lambda b, t: (0, 0))] + st_specs,
        out_specs=(pl.BlockSpec((tl, width), lambda b, t: (step(b, t), 0)),
                   pl.BlockSpec((blk_s, width), lambda b, t: (step(b, t), 0)),
                   *long_specs, *short_specs),
        out_shape=(jax.ShapeDtypeStruct((cols.shape[0], width), BF16),
                   jax.ShapeDtypeStruct((short_nseq * short_seq, width), BF16))
                  + tuple(jax.ShapeDtypeStruct(sh, F32) for sh in state_shapes(batch))
                  + tuple(jax.ShapeDtypeStruct(sh, F32) for sh in state_shapes(short_nseq)),
        scratch_shapes=[pltpu.VMEM((tl, width), F32), pltpu.VMEM((blk_s, width), F32)],
        input_output_aliases=aliases,
        compiler_params=_cparams(("parallel", "arbitrary")),
        name="mlrec",
    )(*args, *st_args)
    return out[0], out[1], out[2:5], out[5:8]


def _merge_body(x_ref, oa_ref, ob_ref, oc_ref, ga_ref, gb_ref, gc_ref, wa_ref, wba_ref, wbb_ref,
                wc_ref, wo_ref, g_ref, o_ref, acc_ref):
    j = pl.program_id(1)

    @pl.when(j == 0)
    def _():
        acc_ref[...] = jnp.zeros_like(acc_ref)

    yb = ob_ref[...]
    br_a = jnp.dot(oa_ref[...], wa_ref[...], preferred_element_type=F32)
    br_b = (jnp.dot(yb, wba_ref[...], preferred_element_type=F32)
            * jax.nn.sigmoid(jnp.dot(yb, wbb_ref[...], preferred_element_type=F32)))
    br_c = jnp.dot(oc_ref[...], wc_ref[...], preferred_element_type=F32)
    merged = (jax.nn.sigmoid(ga_ref[...]) * br_a + jax.nn.sigmoid(gb_ref[...]) * br_b
              + jax.nn.sigmoid(gc_ref[...]) * br_c)
    acc_ref[...] += jnp.dot(merged.astype(BF16), wo_ref[...], preferred_element_type=F32)

    @pl.when(j == pl.num_programs(1) - 1)
    def _():
        _postnorm_residual_to(o_ref, x_ref, acc_ref, g_ref, 1.0)


def _merge(x, oa, ob, oc, cols, w_a, w_ba, w_bb, w_c, w_out, gains, k_gain, *, layer, col0,
           tm=512, tn=512):
    m, d = x.shape
    mix_w = oa.shape[1]
    nj = d // tn
    gb0 = col0 // tn
    act = pl.BlockSpec((tm, mix_w), lambda i, j: (i, 0))
    wspec = pl.BlockSpec((None, mix_w, tn), lambda i, j: (layer, 0, j))

    def gate(branch):
        return pl.BlockSpec((tm, tn), lambda i, j: (i, gb0 + branch * nj + j))

    return pl.pallas_call(
        _merge_body,
        grid=(m // tm, nj),
        in_specs=[pl.BlockSpec((tm, d), lambda i, j: (i, 0)), act, act, act,
                  gate(0), gate(1), gate(2), wspec, wspec, wspec, wspec,
                  pl.BlockSpec((None, tn, d), lambda i, j: (layer, j, 0)),
                  _gain_spec(gains, layer, k_gain)],
        out_specs=pl.BlockSpec((tm, d), lambda i, j: (i, 0)),
        out_shape=jax.ShapeDtypeStruct((m, d), F32),
        scratch_shapes=[pltpu.VMEM((tm, d), F32)],
        compiler_params=_cparams(("parallel", "arbitrary")),
        name="merge",
    )(x, oa, ob, oc, cols, cols, cols, w_a, w_ba, w_bb, w_c, w_out, gains)


def kernel(x_prompt, x_sample, state_hgrn, state_s5_re, state_s5_im, state_mlstm_c, state_mlstm_n, state_mlstm_m, state_mlstm_conv, norm_gains, w_ffn1_up, w_ffn1_down, w_in, hgrn_lower_bounds, hgrn_norm, w_hgrn_out, s5_a_re, s5_a_im, s5_log_dt, s5_b_re, s5_b_im, s5_c_re, s5_c_im, s5_d, w_s5_glu_a, w_s5_glu_b, mlstm_conv_w, mlstm_conv_b, mlstm_wq, mlstm_wk, mlstm_wv, mlstm_w_gates, mlstm_b_gates, mlstm_norm, w_mlstm_out, w_out, w_ffn2_up, w_ffn2_down):
    batch, seq, d_model = x_prompt.shape
    dec_batch, dec_seq, _ = x_sample.shape
    depth = w_in.shape[0]
    mix_w = w_hgrn_out.shape[1]
    n_prompt = batch * seq
    heads_hg = state_hgrn.shape[2]
    groups, state = s5_a_re.shape[1:]
    slabs = groups * state // LANES

    x = (x_prompt.reshape(n_prompt, d_model), x_sample.reshape(dec_batch * dec_seq, d_model))

    lb_all, ab_re, ab_im, bb_re, bb_im = _prep(hgrn_lower_bounds, s5_a_re, s5_a_im, s5_log_dt,
                                               s5_b_re, s5_b_im)

    col_su, col_mx, col_mo, col_gz = 4 * mix_w, 5 * mix_w, 6 * mix_w, 7 * mix_w
    dh = mix_w // ML_HEADS

    sample_init = dict(
        hg=state_hgrn,
        s5=(state_s5_re.reshape(depth, dec_batch, slabs, LANES),
            state_s5_im.reshape(depth, dec_batch, slabs, LANES)),
        conv=state_mlstm_conv,
        ml=(state_mlstm_c, state_mlstm_n.reshape(depth, dec_batch, ML_HEADS, 1, dh),
            jnp.broadcast_to(state_mlstm_m[..., None, None], (depth, dec_batch, ML_HEADS, 1, LANES))))
    groups_cfg = (
        dict(row0=0, nseq=batch, seq=seq, init=dict(hg=None, s5=None, conv=None, ml=None),
             s5=dict(nb=1, tl=128), mp=dict(nb=1, tl=256)),
        dict(row0=n_prompt, nseq=dec_batch, seq=dec_seq, init=sample_init,
             s5=dict(nb=16, tl=dec_seq), mp=dict(nb=16, tl=dec_seq)),
    )
    hgrn_cfg = dict(tl=512, c=16)
    mlrec_cfg = dict(tl=256, c=128, unroll=2)
    new = [dict(hg=None, s5=(None, None), conv=None, ml=(None, None, None)) for _ in groups_cfg]

    gains = norm_gains.reshape(depth * N_GAINS, 1, d_model)
    bf = lambda w: w.astype(BF16)
    w_ffn1_up, w_ffn1_down, w_ffn2_up, w_ffn2_down = map(bf, (w_ffn1_up, w_ffn1_down, w_ffn2_up, w_ffn2_down))
    w_hgrn_out, w_s5_glu_a, w_s5_glu_b, w_mlstm_out, w_out = map(
        bf, (w_hgrn_out, w_s5_glu_a, w_s5_glu_b, w_mlstm_out, w_out))

    for l in range(depth):
        x = _ffn(x, gains, 0, 1, w_ffn1_up, w_ffn1_down, layer=l, rows0=n_prompt)
        cols = _inproj(x, gains, 2, w_in, layer=l)

        wb, wc = _s5_weights(bb_re[l], bb_im[l], s5_c_re[l], s5_c_im[l])
        abr = ab_re[l].reshape(slabs, LANES)
        abi = ab_im[l].reshape(slabs, LANES)
        s5d = s5_d[l].reshape(1, mix_w)
        wq, wk, wv = (w[l].astype(BF16) for w in (mlstm_wq, mlstm_wk, mlstm_wv))
        wg = mlstm_w_gates[l].astype(BF16)

        both = dict(layer=l, batch=batch, seq=seq, short_nseq=dec_batch, short_seq=dec_seq,
                    short_row0=n_prompt)
        oa, oa_sample, new[0]['hg'], new[1]['hg'] = _hgrn(
            cols, lb_all, hgrn_norm, sample_init['hg'], (new[0]['hg'], new[1]['hg']),
            heads=heads_hg, **both, **hgrn_cfg)
        oa = lax.dynamic_update_slice(oa, oa_sample, (n_prompt, 0))

        ob = None
        qkvg = []
        for cfg, st in zip(groups_cfg, new):
            common = dict(layer=l, nseq=cfg['nseq'], row0=cfg['row0'])
            init = cfg['init']
            ob, *st['s5'] = _s5(cols, wb, wc, abr, abi, s5d, init['s5'], (ob, *st['s5']),
                                depth=depth, col0=col_su, seq=cfg['seq'], **common, **cfg['s5'])
            *group_qkvg, st['conv'] = _mlproj(
                cols, mlstm_conv_w[l], mlstm_conv_b[l], wq, wk, wv, wg, mlstm_b_gates[l],
                init['conv'], st['conv'], depth=depth, col0=col_mx, seq=cfg['seq'], **common,
                **cfg['mp'])
            qkvg.append(group_qkvg)
        oc, oc_sample, new[0]['ml'], new[1]['ml'] = _mlrec(
            qkvg[0], qkvg[1], cols, mlstm_norm[l], sample_init['ml'],
            tuple(new[0]['ml']) + tuple(new[1]['ml']), depth=depth, heads=ML_HEADS,
            col0=col_mo, **both, **mlrec_cfg)
        oc = lax.dynamic_update_slice(oc, oc_sample, (n_prompt, 0))

        x = _merge(x, oa, ob, oc, cols, w_hgrn_out, w_s5_glu_a, w_s5_glu_b, w_mlstm_out, w_out,
                   gains, 3, layer=l, col0=col_gz)
        x = _ffn(x, gains, 4, 5, w_ffn2_up, w_ffn2_down, layer=l, rows0=n_prompt,
                 split_out=l == depth - 1)

    y_prompt = x[0].reshape(batch, seq, d_model)
    y_sample = x[1].reshape(dec_batch, dec_seq, d_model)
    states = ()
    for cfg, st in zip(groups_cfg, new):
        nseq = cfg['nseq']
        c_new, n_new, m_new = st['ml']
        states += (st['hg'],
                   st['s5'][0].reshape(depth, nseq, groups, state),
                   st['s5'][1].reshape(depth, nseq, groups, state),
                   c_new, n_new.reshape(depth, nseq, ML_HEADS, dh), m_new[:, :, :, 0, 0], st['conv'])
    return (y_prompt, y_sample) + states
```

```python
import functools
import math

import jax
import jax.numpy as jnp
from jax import lax
from jax.experimental import pallas as pl
from jax.experimental.pallas import tpu as pltpu

F32 = jnp.float32
BF16 = jnp.bfloat16

EPS = 1e-6
NEG_BIG = -1e30
S5_DT_MIN_CLAMP = -1e-4

HG_DK = 128
S5_CH = 16
S5_SUPER = 8
ML_HEADS = 4
CONV_W = 4
N_GAINS = 6

LANES = 128
SUBLANES = 8
VMEM_LIMIT = 56 * 1024 * 1024


def _cparams(sem):
    return pltpu.CompilerParams(dimension_semantics=sem, vmem_limit_bytes=VMEM_LIMIT)


def _rms_scale(y):
    return lax.rsqrt(jnp.mean(y * y, axis=-1, keepdims=True) + EPS)


def _silu(x):
    return x * jax.nn.sigmoid(x)


NORM_ROWS = 16


def _for_row_chunks(n_rows, body):
    def step(r, carry):
        body(pl.ds(pl.multiple_of(r * NORM_ROWS, NORM_ROWS), NORM_ROWS))
        return carry

    lax.fori_loop(0, n_rows // NORM_ROWS, step, 0, unroll=8)


def _prenorm_to(h_ref, x_ref, g_ref):
    g = g_ref[...]

    def body(rows):
        x = x_ref[rows, :]
        h_ref[rows, :] = (x * _rms_scale(x) * g).astype(BF16)

    _for_row_chunks(x_ref.shape[0], body)


def _postnorm_residual_to(o_ref, x_ref, y_ref, g_ref, weight):
    g = g_ref[...] if weight == 1.0 else g_ref[...] * weight

    def body(rows):
        y = y_ref[rows, :]
        o_ref[rows, :] = x_ref[rows, :] + y * _rms_scale(y) * g

    _for_row_chunks(x_ref.shape[0], body)


def _ffn_body(*refs, n_x, n_out, tiles0):
    x_refs = refs[:n_x]
    gpre_ref, gpost_ref, wa_ref, wb_ref, wd_ref = refs[n_x:n_x + 5]
    o_refs = refs[n_x + 5:n_x + 5 + n_out]
    h_ref, acc_ref = refs[n_x + 5 + n_out:]
    i = pl.program_id(0)
    j = pl.program_id(1)
    in_part = (i < tiles0, i >= tiles0)

    for part in range(n_x):
        @pl.when((j == 0) & in_part[part] if n_x == 2 else j == 0)
        def _():
            _prenorm_to(h_ref, x_refs[part], gpre_ref)

    @pl.when(j == 0)
    def _():
        acc_ref[...] = jnp.zeros_like(acc_ref)

    h = h_ref[...]
    a = jnp.dot(h, wa_ref[...], preferred_element_type=F32)
    b = jnp.dot(h, wb_ref[...], preferred_element_type=F32)
    act = (_silu(a) * b).astype(BF16)
    acc_ref[...] += jnp.dot(act, wd_ref[...], preferred_element_type=F32)

    last = j == pl.num_programs(1) - 1
    split = max(n_x, n_out) == 2
    for part in range(2 if split else 1):
        @pl.when(last & in_part[part] if split else last)
        def _():
            _postnorm_residual_to(o_refs[part if n_out == 2 else 0], x_refs[part if n_x == 2 else 0],
                                  acc_ref, gpost_ref, 0.5)


def _gain_spec(gains, layer, k):
    return pl.BlockSpec((None, 1, gains.shape[-1]), lambda i, j: (layer * N_GAINS + k, 0, 0))


def _ffn(x, gains, k_pre, k_post, w_up, w_down, *, layer, rows0, split_out=False, tm=512, tf=512):
    xs = x if isinstance(x, tuple) else (x,)
    d = xs[0].shape[1]
    m = sum(a.shape[0] for a in xs)
    d_ff = w_down.shape[1]
    nf = d_ff // tf
    tiles0 = rows0 // tm
    whole = pl.BlockSpec((tm, d), lambda i, j: (i, 0))
    parts = [pl.BlockSpec((tm, d), lambda i, j: (jnp.minimum(i, tiles0 - 1), 0)),
             pl.BlockSpec((tm, d), lambda i, j: (jnp.maximum(i - tiles0, 0), 0))]
    part_shapes = [jax.ShapeDtypeStruct((rows0, d), F32), jax.ShapeDtypeStruct((m - rows0, d), F32)]
    out = pl.pallas_call(
        functools.partial(_ffn_body, n_x=len(xs), n_out=2 if split_out else 1, tiles0=tiles0),
        grid=(m // tm, nf),
        in_specs=(parts if len(xs) == 2 else [whole]) + [
            _gain_spec(gains, layer, k_pre),
            _gain_spec(gains, layer, k_post),
            pl.BlockSpec((None, d, tf), lambda i, j: (layer, 0, j)),
            pl.BlockSpec((None, d, tf), lambda i, j: (layer, 0, j + nf)),
            pl.BlockSpec((None, tf, d), lambda i, j: (layer, j, 0)),
        ],
        out_specs=parts if split_out else whole,
        out_shape=part_shapes if split_out else jax.ShapeDtypeStruct((m, d), F32),
        scratch_shapes=[pltpu.VMEM((tm, d), BF16), pltpu.VMEM((tm, d), F32)],
        compiler_params=_cparams(("parallel", "arbitrary")),
        name="ffn",
    )(*xs, gains, gains, w_up, w_up, w_down)
    return tuple(out) if split_out else out


def _inproj_body(x_ref, g_ref, w_ref, o_ref, h_ref):
    @pl.when(pl.program_id(1) == 0)
    def _():
        _prenorm_to(h_ref, x_ref, g_ref)

    o_ref[...] = jnp.dot(h_ref[...], w_ref[...].astype(BF16), preferred_element_type=F32)


def _inproj(x, gains, k_gain, w, *, layer, tm=1024, tn=1024):
    m, d = x.shape
    n = w.shape[2]
    return pl.pallas_call(
        _inproj_body,
        grid=(m // tm, n // tn),
        in_specs=[
            pl.BlockSpec((tm, d), lambda i, j: (i, 0)),
            _gain_spec(gains, layer, k_gain),
            pl.BlockSpec((None, d, tn), lambda i, j: (layer, 0, j)),
        ],
        out_specs=pl.BlockSpec((tm, tn), lambda i, j: (i, j)),
        out_shape=jax.ShapeDtypeStruct((m, n), F32),
        scratch_shapes=[pltpu.VMEM((tm, d), BF16)],
        compiler_params=_cparams(("parallel", "arbitrary")),
        name="inproj",
    )(x, gains, w)


def _prep_body(lbraw_ref, are_ref, aim_ref, ldt_ref, bre_ref, bim_ref,
               lb_ref, abr_ref, abi_ref, bbr_ref, bbi_ref):
    depth = lbraw_ref.shape[0]
    raw = lbraw_ref[...]
    e = jnp.exp(raw - jnp.max(raw, axis=0, keepdims=True))
    lbs = e / jnp.sum(e, axis=0, keepdims=True)
    run = jnp.zeros_like(lbs[0:1])
    for l in range(depth):
        run = run + lbs[l:l + 1]
        lb_ref[l:l + 1, :] = run - lbs[0:1]

    for l in range(depth):
        dt = jnp.exp(ldt_ref[l])
        lam_re = jnp.minimum(are_ref[l], S5_DT_MIN_CLAMP)
        lam_im = aim_ref[l]
        mag = jnp.exp(lam_re * dt)
        ab_re = mag * jnp.cos(lam_im * dt)
        ab_im = mag * jnp.sin(lam_im * dt)
        inv = 1.0 / (lam_re * lam_re + lam_im * lam_im)
        f_re = ((ab_re - 1.0) * lam_re + ab_im * lam_im) * inv
        f_im = (ab_im * lam_re - (ab_re - 1.0) * lam_im) * inv
        abr_ref[l] = ab_re
        abi_ref[l] = ab_im
        b_re = bre_ref[l]
        b_im = bim_ref[l]
        bbr_ref[l] = f_re * b_re - f_im * b_im
        bbi_ref[l] = f_re * b_im + f_im * b_re


def _prep(lb_raw, a_re, a_im, log_dt, b_re, b_im):
    depth, groups, state = a_re.shape
    gp = groups * state
    ch = b_re.shape[-1]
    row = lambda a: a.reshape(depth, 1, gp)
    ldt = jnp.broadcast_to(log_dt[:, :, None], (depth, groups, state))
    chan_major = lambda b: b.transpose(0, 3, 1, 2).reshape(depth, ch, gp)
    out_shape = (
        jax.ShapeDtypeStruct(lb_raw.shape, F32),
        jax.ShapeDtypeStruct((depth, 1, gp), F32),
        jax.ShapeDtypeStruct((depth, 1, gp), F32),
        jax.ShapeDtypeStruct((depth, ch, gp), F32),
        jax.ShapeDtypeStruct((depth, ch, gp), F32),
    )
    return pl.pallas_call(_prep_body, out_shape=out_shape, name="prep")(
        lb_raw, row(a_re), row(a_im), row(ldt), chan_major(b_re), chan_major(b_im))


def _hgrn_stage_shapes(heads, c):
    return [pltpu.VMEM((heads, c * c, HG_DK), BF16), pltpu.VMEM((heads * c * c, LANES), F32),
            pltpu.VMEM((heads, c, HG_DK), BF16), pltpu.VMEM((heads, c, HG_DK), BF16),
            pltpu.VMEM((heads, c, HG_DK), BF16), pltpu.VMEM((heads, 1, HG_DK), F32),
            pltpu.VMEM((heads, c, HG_DK), F32)]


def _hgrn_run(q_ref, f_ref, i_ref, g_ref, lb_ref, nrm_ref, st_in_ref, st_out_ref, of_ref, o_ref,
              stage, *, nb, tl, c, heads, transposed):
    assert st_in_ref is st_out_ref or tl == c
    prod_ref, sums_ref, qc_ref, ke_ref, vb_ref, ce_ref, gs_ref = stage
    eye = (lax.broadcasted_iota(jnp.int32, (HG_DK, HG_DK), 0)
           == lax.broadcasted_iota(jnp.int32, (HG_DK, HG_DK), 1))
    row = lax.broadcasted_iota(jnp.int32, (c, HG_DK), 0)
    lag = row - lax.broadcasted_iota(jnp.int32, (c, HG_DK), 1)
    ones = jnp.ones((HG_DK, LANES), BF16)
    chunks_per_seq = tl // c
    n_steps = nb * chunks_per_seq
    head_cols = [slice(h * HG_DK, (h + 1) * HG_DK) for h in range(heads)]

    def chunk_rows(idx):
        return pl.ds(pl.multiple_of(idx * c, c), c)

    def prepare(idx):
        rows = chunk_rows(idx)
        for h, hs in enumerate(head_cols):
            lb = lb_ref[:, hs]
            fpre = f_ref[rows, hs]
            q = _silu(q_ref[rows, hs])
            forget = lb + (1.0 - lb) * jax.nn.sigmoid(fpre)
            kin = (1.0 - lb) * jax.nn.sigmoid(-fpre)

            cp = forget
            sh = 1
            while sh < c:
                cp = cp * jnp.where(row >= sh, pltpu.roll(cp, sh, 0), 1.0)
                sh *= 2
            sp = jnp.where(row < c - 1, pltpu.roll(forget, c - 1, 0), 1.0)
            sh = 1
            while sh < c:
                sp = sp * jnp.where(row < c - sh, pltpu.roll(sp, c - sh, 0), 1.0)
                sh *= 2

            kd = kin
            prods = [q * kd]
            for d in range(1, c):
                kd = pltpu.roll(kd, 1, 0) * forget
                prods.append(q * kd)
            prod_ref[h] = jnp.concatenate(prods, axis=0).astype(BF16)
            qc_ref[h] = (q * cp).astype(BF16)
            ke_ref[h] = (kin * sp).astype(BF16)
            vb_ref[h] = i_ref[rows, hs].astype(BF16)
            ce_ref[h] = cp[c - 1:c, :]
            gs_ref[h] = nrm_ref[:, hs] * _silu(g_ref[rows, hs])

    def finish(idx):
        n = idx // chunks_per_seq
        rows = chunk_rows(idx)
        sums_ref[...] = jnp.dot(prod_ref[...].reshape(heads * c * c, HG_DK), ones,
                                preferred_element_type=F32)
        inter = []
        for h in range(heads):
            st = st_in_ref[n, h]
            vb = vb_ref[h]
            if transposed:
                inter.append(lax.dot_general(qc_ref[h], st.astype(BF16), (((1,), (1,)), ((), ())),
                                             preferred_element_type=F32))
                upd = lax.dot_general(vb, ke_ref[h], (((0,), (0,)), ((), ())),
                                      preferred_element_type=F32)
                st_out_ref[n, h] = st * ce_ref[h] + upd
            else:
                inter.append(jnp.dot(qc_ref[h], st.astype(BF16), preferred_element_type=F32))
                upd = lax.dot_general(ke_ref[h], vb, (((0,), (0,)), ((), ())),
                                      preferred_element_type=F32)
                decay_col = jnp.sum(jnp.where(eye, ce_ref[h], 0.0), axis=1, keepdims=True)
                st_out_ref[n, h] = st * decay_col + upd
        outs = []
        for h in range(heads):
            scores = jnp.zeros((c, HG_DK), F32)
            for d in range(c):
                scores = jnp.where(lag == d, sums_ref[pl.ds((h * c + d) * c, c), :], scores)
            outs.append(inter[h] + jnp.dot(scores[:, :c].astype(BF16), vb_ref[h],
                                           preferred_element_type=F32))
        for hs, h in zip(head_cols, range(heads)):
            of_ref[rows, hs] = outs[h] * _rms_scale(outs[h]) * gs_ref[h]

    def step(idx, carry):
        finish(idx - 1)
        prepare(idx)
        return carry

    prepare(0)
    lax.fori_loop(1, n_steps, step, 0, unroll=True if n_steps <= SUBLANES else 2)
    finish(n_steps - 1)
    o_ref[...] = of_ref[...].astype(BF16)


def _hgrn_body(*refs, heads, n_prev, long_cfg, short_cfg):
    long_in, short_in = refs[0:4], refs[4:8]
    lb_ref, nrm_ref, s0_ref = refs[8:11]
    ol_ref, os_ref, sl_ref, ss_ref, stl_ref, ofl_ref, ofs_ref = refs[11 + n_prev:18 + n_prev]
    stage = refs[18 + n_prev:]
    stage_long, stage_short = stage[:len(stage) // 2], stage[len(stage) // 2:]
    t = pl.program_id(1)

    @pl.when(t == 0)
    def _():
        for h in range(heads):
            stl_ref[0, h] = jnp.zeros(stl_ref.shape[2:], F32)

    _hgrn_run(*long_in, lb_ref, nrm_ref, stl_ref, stl_ref, ofl_ref, ol_ref, stage_long,
              heads=heads, transposed=True, **long_cfg)

    @pl.when(t == pl.num_programs(1) - 1)
    def _():
        for h in range(heads):
            sl_ref[0, h] = stl_ref[0, h].T

    _hgrn_run(*short_in, lb_ref, nrm_ref, s0_ref, ss_ref, ofs_ref, os_ref, stage_short,
              heads=heads, transposed=False, **short_cfg)


def _layer_state_spec(shape, nb, layer):
    tail = (0,) * (len(shape) - 2)
    return pl.BlockSpec((None, nb) + tuple(shape[2:]), lambda b, t: (layer, b) + tail)


def _hgrn(cols, lb, nrm, short_init, prevs, *, layer, heads, batch, seq, tl, c,
          short_nseq, short_seq, short_row0):
    depth = lb.shape[0]
    dk = HG_DK
    mix_w = heads * dk
    dv = mix_w // heads
    nt = seq // tl
    nb_s = short_nseq // (batch * nt)
    blk_s = nb_s * short_seq
    rb0_s = short_row0 // blk_s
    step = lambda b, t: b * nt + t
    long_shape = (depth, batch, heads, dk, dv)
    short_shape = (depth, short_nseq, heads, dk, dv)

    def colspecs(blk, rb0):
        return [pl.BlockSpec((blk, mix_w), lambda b, t, g=g: (rb0 + step(b, t), g)) for g in range(4)]

    vec = pl.BlockSpec((None, 1, mix_w), lambda b, t: (layer, 0, 0))
    long_spec = _layer_state_spec(long_shape, 1, layer)
    short_spec = pl.BlockSpec((None, nb_s, heads, dk, dv), lambda b, t: (layer, step(b, t), 0, 0, 0))
    args = [cols] * 8 + [lb.reshape(depth, 1, mix_w), nrm.reshape(depth, 1, mix_w)]
    st_specs, st_args, aliases = _state_args([short_spec], (short_init,), (None, None) + tuple(prevs),
                                             len(args))
    out = pl.pallas_call(
        functools.partial(_hgrn_body, heads=heads, n_prev=len(aliases),
                          long_cfg=dict(nb=1, tl=tl, c=c),
                          short_cfg=dict(nb=nb_s, tl=short_seq, c=short_seq)),
        grid=(batch, nt),
        in_specs=colspecs(tl, 0) + colspecs(blk_s, rb0_s) + [vec, vec] + st_specs,
        out_specs=(pl.BlockSpec((tl, mix_w), lambda b, t: (step(b, t), 0)),
                   pl.BlockSpec((blk_s, mix_w), lambda b, t: (step(b, t), 0)),
                   long_spec, short_spec),
        out_shape=(jax.ShapeDtypeStruct((cols.shape[0], mix_w), BF16),
                   jax.ShapeDtypeStruct((short_nseq * short_seq, mix_w), BF16),
                   jax.ShapeDtypeStruct(long_shape, F32), jax.ShapeDtypeStruct(short_shape, F32)),
        scratch_shapes=[pltpu.VMEM((1, heads, dv, dk), F32), pltpu.VMEM((tl, mix_w), F32),
                        pltpu.VMEM((blk_s, mix_w), F32)]
                       + _hgrn_stage_shapes(heads, c) + _hgrn_stage_shapes(heads, short_seq),
        input_output_aliases=aliases,
        compiler_params=_cparams(("parallel", "arbitrary")),
        name="hgrn",
    )(*args, *st_args)
    return out


def _gelu_tanh(x):
    return 0.5 * x * (1.0 + jnp.tanh(math.sqrt(2.0 / math.pi) * (x + 0.044715 * (x * x * x))))


def _state_args(specs, inits, prevs, n_args):
    in_specs, args, aliases = [], [], {}
    if inits is not None:
        in_specs += list(specs)
        args += list(inits)
    for out_idx, prev in enumerate(prevs):
        if prev is not None:
            aliases[n_args + len(args)] = out_idx
            in_specs.append(pl.BlockSpec(memory_space=pl.ANY))
            args.append(prev)
    return in_specs, args, aliases


def _s5_body(*refs, nb, tl, has_init, n_prev):
    u_ref, wb_ref, wc_ref, abr_ref, abi_ref, d_ref = refs[:6]
    y_ref, xr_ref, xi_ref, sr_ref, si_ref = refs[6 + 2 * has_init + n_prev:]
    rows = nb * tl
    pitch = rows + SUBLANES
    nsb = wb_ref.shape[0]
    cw = wb_ref.shape[1]
    half = wb_ref.shape[2] // 2
    per = half // LANES
    slabs = nsb * per
    t = pl.program_id(1)

    @pl.when(t == 0)
    def _():
        if has_init:
            xr_ref[...] = refs[6][...]
            xi_ref[...] = refs[7][...]
        else:
            xr_ref[...] = jnp.zeros_like(xr_ref)
            xi_ref[...] = jnp.zeros_like(xi_ref)

    for sb in range(nsb):
        ub = u_ref[:, sb * cw:(sb + 1) * cw].astype(BF16)
        res = jnp.dot(ub, wb_ref[sb], preferred_element_type=F32)
        for j in range(per):
            base = (sb * per + j) * pitch
            sr_ref[pl.ds(base, rows), :] = res[:, j * LANES:(j + 1) * LANES]
            si_ref[pl.ds(base, rows), :] = res[:, half + j * LANES:half + (j + 1) * LANES]

    ar = abr_ref[...]
    ai = abi_ref[...]

    def advance(n, tt, xr, xi):
        at = pl.ds(n * tl + tt, slabs, stride=pitch)
        nxr = ar * xr - ai * xi + sr_ref[at, :]
        nxi = ar * xi + ai * xr + si_ref[at, :]
        sr_ref[at, :] = nxr
        si_ref[at, :] = nxi
        return nxr, nxi

    par = 2 if nb % 2 == 0 else 1

    def group_scan(p, carry):
        seqs = [par * p + i for i in range(par)]

        def step(tt, x):
            out = ()
            for i, n in enumerate(seqs):
                out += advance(n, tt, x[2 * i], x[2 * i + 1])
            return out

        x0 = ()
        for n in seqs:
            x0 += (xr_ref[n], xi_ref[n])
        x = lax.fori_loop(0, tl, step, x0, unroll=SUBLANES)
        for i, n in enumerate(seqs):
            xr_ref[n] = x[2 * i]
            xi_ref[n] = x[2 * i + 1]
        return carry

    lax.fori_loop(0, nb // par, group_scan, 0)

    for sb in range(nsb):
        parts = [sr_ref[pl.ds((sb * per + j) * pitch, rows), :] for j in range(per)]
        parts += [si_ref[pl.ds((sb * per + j) * pitch, rows), :] for j in range(per)]
        xs = jnp.concatenate(parts, axis=1).astype(BF16)
        cs = slice(sb * cw, (sb + 1) * cw)
        y = jnp.dot(xs, wc_ref[sb], preferred_element_type=F32) + d_ref[:, cs] * u_ref[:, cs]
        y_ref[:, cs] = _gelu_tanh(y).astype(BF16)


def _s5(cols, wb, wc, ab_re, ab_im, d, inits, prevs, *, layer, depth, nseq, row0, col0, seq, nb, tl):
    slabs = ab_re.shape[0]
    nt = seq // tl
    blk = nb * tl
    rb0 = row0 // blk
    mix_w = d.shape[-1]
    cb = col0 // mix_w
    const3 = lambda a: pl.BlockSpec(a.shape, lambda b, t: (0, 0, 0))
    const2 = lambda a: pl.BlockSpec(a.shape, lambda b, t: (0, 0))
    sshape = (depth, nseq, slabs, LANES)
    xspec = _layer_state_spec(sshape, nb, layer)
    pitch = blk + SUBLANES
    args = [cols, wb, wc, ab_re, ab_im, d]
    st_specs, st_args, aliases = _state_args([xspec, xspec], inits, prevs, len(args))
    return pl.pallas_call(
        functools.partial(_s5_body, nb=nb, tl=tl, has_init=inits is not None, n_prev=len(aliases)),
        grid=(nseq // nb, nt),
        in_specs=[pl.BlockSpec((blk, mix_w), lambda b, t: (rb0 + b * nt + t, cb)),
                  const3(wb), const3(wc), const2(ab_re), const2(ab_im), const2(d)] + st_specs,
        out_specs=(pl.BlockSpec((blk, mix_w), lambda b, t: (rb0 + b * nt + t, 0)), xspec, xspec),
        out_shape=(jax.ShapeDtypeStruct((cols.shape[0], mix_w), BF16),
                   jax.ShapeDtypeStruct(sshape, F32),
                   jax.ShapeDtypeStruct(sshape, F32)),
        scratch_shapes=[pltpu.VMEM((slabs * pitch, LANES), F32),
                        pltpu.VMEM((slabs * pitch, LANES), F32)],
        input_output_aliases=aliases,
        compiler_params=_cparams(("parallel", "arbitrary")),
        name="s5",
    )(*args, *st_args)


def _s5_weights(bb_re, bb_im, c_re, c_im):
    ch, gp = bb_re.shape
    groups = c_re.shape[0]
    state = gp // groups
    nsb = groups // S5_SUPER
    eye = jnp.eye(S5_SUPER, dtype=F32)

    def in_w(bb):
        b4 = bb.reshape(ch, nsb, S5_SUPER, state)
        return jnp.einsum('csgp,gh->sgchp', b4, eye).reshape(nsb, S5_SUPER * ch, S5_SUPER * state)

    def out_w(c):
        c4 = c.reshape(nsb, S5_SUPER, ch, state)
        return jnp.einsum('sgcp,gh->sgphc', c4, eye).reshape(nsb, S5_SUPER * state, S5_SUPER * ch)

    wb = jnp.concatenate([in_w(bb_re), in_w(bb_im)], axis=2).astype(BF16)
    wc = jnp.concatenate([out_w(c_re), -out_w(c_im)], axis=1).astype(BF16)
    return wb, wc


def _mlproj_body(*refs, nb, tl, has_init, n_prev):
    x_ref, cw_ref, cb_ref, wq_ref, wk_ref, wv_ref, wg_ref, bg_ref = refs[:8]
    q_ref, k_ref, v_ref, g_ref, cn_ref, xx_ref = refs[8 + has_init + n_prev:]
    t = pl.program_id(1)
    width = x_ref.shape[1]
    halo = CONV_W - 1
    lo = SUBLANES - halo

    @pl.when(t == 0)
    def _():
        if has_init:
            xx_ref[:, lo:SUBLANES, :] = refs[8][...]
        else:
            xx_ref[:, lo:SUBLANES, :] = jnp.zeros((nb, halo, width), F32)

    @pl.when(t > 0)
    def _():
        xx_ref[:, lo:SUBLANES, :] = xx_ref[:, tl + lo:tl + SUBLANES, :]

    x = x_ref[...]
    xx_ref[:, SUBLANES:, :] = x.reshape(nb, tl, width)
    xc = jnp.zeros((nb, tl, width), F32) + cb_ref[...][None]
    for j in range(CONV_W):
        xc = xc + xx_ref[:, lo + j:lo + j + tl, :] * cw_ref[j:j + 1, :][None]
    cn_ref[...] = xx_ref[:, tl + lo:tl + SUBLANES, :]
    xc = _silu(xc).reshape(nb * tl, width).astype(BF16)
    xb = x.astype(BF16)

    dh = width // ML_HEADS
    gates = jnp.zeros((nb * tl, 2 * ML_HEADS), F32) + bg_ref[...]
    for h in range(ML_HEADS):
        hs = slice(h * dh, (h + 1) * dh)
        q = jnp.dot(xc[:, hs], wq_ref[h], preferred_element_type=F32)
        k = jnp.dot(xc[:, hs], wk_ref[h], preferred_element_type=F32)
        v = jnp.dot(xb[:, hs], wv_ref[h], preferred_element_type=F32)
        q_ref[:, hs] = q
        k_ref[:, hs] = k
        v_ref[:, hs] = v
        for part, val in enumerate((q, k, v)):
            w = wg_ref[part * width + h * dh:part * width + (h + 1) * dh, :]
            gates = gates + jnp.dot(val.astype(BF16), w, preferred_element_type=F32)
    g_ref[...] = gates


def _mlproj(cols, conv_w, conv_b, wq, wk, wv, wg, bg, init, prev, *, layer, depth, nseq, row0, col0,
            seq, nb, tl):
    width = conv_w.shape[1]
    halo = CONV_W - 1
    nt = seq // tl
    blk = nb * tl
    rb0 = row0 // blk
    cb = col0 // width
    ng = wg.shape[1]
    full = lambda a: pl.BlockSpec(a.shape, lambda b, t: (0,) * a.ndim)
    rowspec = lambda w: pl.BlockSpec((blk, w), lambda b, t: (b * nt + t, 0))
    sshape = (depth, nseq, halo, width)
    cspec = _layer_state_spec(sshape, nb, layer)
    args = [cols, conv_w, conv_b.reshape(1, width), wq, wk, wv, wg, bg.reshape(1, ng)]
    st_specs, st_args, aliases = _state_args(
        [cspec], None if init is None else (init,), (None, None, None, None, prev), len(args))
    return pl.pallas_call(
        functools.partial(_mlproj_body, nb=nb, tl=tl, has_init=init is not None,
                          n_prev=len(aliases)),
        grid=(nseq // nb, nt),
        in_specs=[pl.BlockSpec((blk, width), lambda b, t: (rb0 + b * nt + t, cb))]
                 + [full(a) for a in args[1:]] + st_specs,
        out_specs=(rowspec(width), rowspec(width), rowspec(width), rowspec(ng), cspec),
        out_shape=(jax.ShapeDtypeStruct((nseq * seq, width), F32),) * 3
                  + (jax.ShapeDtypeStruct((nseq * seq, ng), F32),
                     jax.ShapeDtypeStruct(sshape, F32)),
        scratch_shapes=[pltpu.VMEM((nb, tl + SUBLANES, width), F32)],
        input_output_aliases=aliases,
        compiler_params=_cparams(("parallel", "arbitrary")),
        name="mlproj",
    )(*args, *st_args)


def _mlrec_run(q_ref, k_ref, v_ref, g_ref, o_ref, nrm_ref, c_ref, n_ref, m_ref, hf_ref, h_ref,
               *, nb, tl, c, heads, unroll):
    dh = q_ref.shape[1] // heads
    glane = lax.broadcasted_iota(jnp.int32, (c, 2 * heads), 1)
    r = lax.broadcasted_iota(jnp.int32, (c, c), 0)
    s = lax.broadcasted_iota(jnp.int32, (c, c), 1)
    causal = r >= s
    chunks_per_seq = tl // c
    k_scale = dh ** -0.5

    def chunk(idx, carry):
        n = idx // chunks_per_seq
        rows = pl.ds(pl.multiple_of(idx * c, c), c)
        gt = g_ref[rows, :]
        for h in range(heads):
            hs = slice(h * dh, (h + 1) * dh)
            q = q_ref[rows, hs]
            k = k_ref[rows, hs] * k_scale
            v = v_ref[rows, hs]
            ig_col = jnp.sum(jnp.where(glane == h, gt, 0.0), axis=1, keepdims=True)
            fp_col = jnp.sum(jnp.where(glane == h + heads, gt, 0.0), axis=1, keepdims=True)
            lf_col = jnp.minimum(fp_col, 0.0) - jnp.log1p(jnp.exp(-jnp.abs(fp_col)))
            lf_row = jnp.sum(jnp.where(r == s, lf_col, 0.0), axis=0, keepdims=True)
            ig_row = jnp.sum(jnp.where(r == s, ig_col, 0.0), axis=0, keepdims=True)
            b_col = jnp.sum(jnp.where(causal, lf_row, 0.0), axis=1, keepdims=True)
            b_row = jnp.sum(jnp.where(r <= s, lf_col, 0.0), axis=0, keepdims=True)

            m_prev = m_ref[n, h][:, 0:1]
            n_prev = n_ref[n, h]
            c_prev = c_ref[n, h]

            logw = jnp.where(causal, b_col - b_row + ig_row, NEG_BIG)
            m_t = jnp.maximum(b_col + m_prev, jnp.max(logw, axis=1, keepdims=True))
            w_prev = jnp.exp(b_col + m_prev - m_t)
            w = jnp.exp(logw - m_t)
            qb = q.astype(BF16)
            vb = v.astype(BF16)
            sc = lax.dot_general(qb, k.astype(BF16), (((1,), (1,)), ((), ())),
                                 preferred_element_type=F32) * w
            num = (w_prev * jnp.dot(qb, c_prev.astype(BF16), preferred_element_type=F32)
                   + jnp.dot(sc.astype(BF16), vb, preferred_element_type=F32))
            den = (w_prev * jnp.sum(q * n_prev, axis=1, keepdims=True)
                   + jnp.sum(sc, axis=1, keepdims=True))
            hval = num / jnp.maximum(jnp.abs(den), jnp.exp(-m_t))

            m_end = m_t[c - 1:c, :]
            b_end = b_col[c - 1:c, :]
            g_prev = jnp.exp(b_end + m_prev - m_end)
            w_in = jnp.exp(b_end - b_col + ig_col - m_end)
            kw = k * w_in
            c_ref[n, h] = g_prev * c_prev + lax.dot_general(
                kw.astype(BF16), vb, (((0,), (0,)), ((), ())), preferred_element_type=F32)
            n_ref[n, h] = g_prev * n_prev + jnp.sum(kw, axis=0, keepdims=True)
            m_ref[n, h] = jnp.broadcast_to(m_end, (1, LANES))
            hf_ref[rows, hs] = (hval * _rms_scale(hval) * nrm_ref[:, hs]
                                * jax.nn.sigmoid(o_ref[rows, hs]))
        return carry

    lax.fori_loop(0, nb * chunks_per_seq, chunk, 0, unroll=unroll)
    h_ref[...] = hf_ref[...].astype(BF16)


def _mlrec_body(*refs, heads, n_prev, long_cfg, short_cfg):
    long_in, short_in = refs[0:5], refs[5:10]
    nrm_ref = refs[10]
    c0_ref, n0_ref, m0_ref = refs[11:14]
    (hl_ref, hs_ref, cl_ref, nl_ref, ml_ref, cs_ref, ns_ref, ms_ref, hfl_ref, hfs_ref
     ) = refs[14 + n_prev:]

    @pl.when(pl.program_id(1) == 0)
    def _():
        cl_ref[...] = jnp.zeros_like(cl_ref)
        nl_ref[...] = jnp.zeros_like(nl_ref)
        ml_ref[...] = jnp.zeros_like(ml_ref)

    _mlrec_run(*long_in, nrm_ref, cl_ref, nl_ref, ml_ref, hfl_ref, hl_ref, heads=heads, **long_cfg)
    cs_ref[...] = c0_ref[...]
    ns_ref[...] = n0_ref[...]
    ms_ref[...] = m0_ref[...]
    _mlrec_run(*short_in, nrm_ref, cs_ref, ns_ref, ms_ref, hfs_ref, hs_ref, heads=heads, **short_cfg)


def _mlrec(long_qkvg, short_qkvg, cols, nrm, short_init, prevs, *, layer, depth, heads, col0,
           batch, seq, tl, c, short_nseq, short_seq, short_row0, unroll):
    width = long_qkvg[0].shape[1]
    dh = width // heads
    nt = seq // tl
    steps = batch * nt
    nb_s = short_nseq // steps
    blk_s = nb_s * short_seq
    cb = col0 // width
    ng = long_qkvg[3].shape[1]
    rb0_s = short_row0 // blk_s
    step = lambda b, t: b * nt + t

    def group_specs(blk, rb0):
        rows = pl.BlockSpec((blk, width), lambda b, t: (step(b, t), 0))
        return [rows, rows, rows,
                pl.BlockSpec((blk, ng), lambda b, t: (step(b, t), 0)),
                pl.BlockSpec((blk, width), lambda b, t: (rb0 + step(b, t), cb))]

    def state_shapes(nseq):
        return ((depth, nseq, heads, dh, dh), (depth, nseq, heads, 1, dh), (depth, nseq, heads, 1, LANES))

    long_specs = [_layer_state_spec(sh, 1, layer) for sh in state_shapes(batch)]
    short_specs = [pl.BlockSpec((None, nb_s) + sh[2:],
                                lambda b, t, nd=len(sh): (layer, step(b, t)) + (0,) * (nd - 2))
                   for sh in state_shapes(short_nseq)]
    args = [*long_qkvg, cols, *short_qkvg, cols, nrm.reshape(1, width)]
    st_specs, st_args, aliases = _state_args(short_specs, short_init, (None, None) + tuple(prevs),
                                             len(args))
    out = pl.pallas_call(
        functools.partial(_mlrec_body, heads=heads, n_prev=len(aliases),
                          long_cfg=dict(nb=1, tl=tl, c=c, unroll=unroll),
                          short_cfg=dict(nb=nb_s, tl=short_seq, c=short_seq, unroll=True)),
        grid=(batch, nt),
        in_specs=group_specs(tl, 0) + group_specs(blk_s, rb0_s)
                 + [pl.BlockSpec((1, width), lambda b, t: (0, 0))] + st_specs,
        out_specs=(pl.BlockSpec((tl, width), lambda b, t: (step(b, t), 0)),
                   pl.BlockSpec((blk_s, width), lambda b, t: (step(b, t), 0)),
                   *long_specs, *short_specs),
        out_shape=(jax.ShapeDtypeStruct((cols.shape[0], width), BF16),
                   jax.ShapeDtypeStruct((short_nseq * short_seq, width), BF16))
                  + tuple(jax.ShapeDtypeStruct(sh, F32) for sh in state_shapes(batch))
                  + tuple(jax.ShapeDtypeStruct(sh, F32) for sh in state_shapes(short_nseq)),
        scratch_shapes=[pltpu.VMEM((tl, width), F32), pltpu.VMEM((blk_s, width), F32)],
        input_output_aliases=aliases,
        compiler_params=_cparams(("parallel", "arbitrary")),
        name="mlrec",
    )(*args, *st_args)
    return out[0], out[1], out[2:5], out[5:8]


def _merge_body(x_ref, oa_ref, ob_ref, oc_ref, ga_ref, gb_ref, gc_ref, wa_ref, wba_ref, wbb_ref,
                wc_ref, wo_ref, g_ref, o_ref, acc_ref):
    j = pl.program_id(1)

    @pl.when(j == 0)
    def _():
        acc_ref[...] = jnp.zeros_like(acc_ref)

    yb = ob_ref[...]
    br_a = jnp.dot(oa_ref[...], wa_ref[...], preferred_element_type=F32)
    br_b = (jnp.dot(yb, wba_ref[...], preferred_element_type=F32)
            * jax.nn.sigmoid(jnp.dot(yb, wbb_ref[...], preferred_element_type=F32)))
    br_c = jnp.dot(oc_ref[...], wc_ref[...], preferred_element_type=F32)
    merged = (jax.nn.sigmoid(ga_ref[...]) * br_a + jax.nn.sigmoid(gb_ref[...]) * br_b
              + jax.nn.sigmoid(gc_ref[...]) * br_c)
    acc_ref[...] += jnp.dot(merged.astype(BF16), wo_ref[...], preferred_element_type=F32)

    @pl.when(j == pl.num_programs(1) - 1)
    def _():
        _postnorm_residual_to(o_ref, x_ref, acc_ref, g_ref, 1.0)


def _merge(x, oa, ob, oc, cols, w_a, w_ba, w_bb, w_c, w_out, gains, k_gain, *, layer, col0,
           tm=512, tn=512):
    m, d = x.shape
    mix_w = oa.shape[1]
    nj = d // tn
    gb0 = col0 // tn
    act = pl.BlockSpec((tm, mix_w), lambda i, j: (i, 0))
    wspec = pl.BlockSpec((None, mix_w, tn), lambda i, j: (layer, 0, j))

    def gate(branch):
        return pl.BlockSpec((tm, tn), lambda i, j: (i, gb0 + branch * nj + j))

    return pl.pallas_call(
        _merge_body,
        grid=(m // tm, nj),
        in_specs=[pl.BlockSpec((tm, d), lambda i, j: (i, 0)), act, act, act,
                  gate(0), gate(1), gate(2), wspec, wspec, wspec, wspec,
                  pl.BlockSpec((None, tn, d), lambda i, j: (layer, j, 0)),
                  _gain_spec(gains, layer, k_gain)],
        out_specs=pl.BlockSpec((tm, d), lambda i, j: (i, 0)),
        out_shape=jax.ShapeDtypeStruct((m, d), F32),
        scratch_shapes=[pltpu.VMEM((tm, d), F32)],
        compiler_params=_cparams(("parallel", "arbitrary")),
        name="merge",
    )(x, oa, ob, oc, cols, cols, cols, w_a, w_ba, w_bb, w_c, w_out, gains)


def kernel(x_prompt, x_sample, state_hgrn, state_s5_re, state_s5_im, state_mlstm_c, state_mlstm_n, state_mlstm_m, state_mlstm_conv, norm_gains, w_ffn1_up, w_ffn1_down, w_in, hgrn_lower_bounds, hgrn_norm, w_hgrn_out, s5_a_re, s5_a_im, s5_log_dt, s5_b_re, s5_b_im, s5_c_re, s5_c_im, s5_d, w_s5_glu_a, w_s5_glu_b, mlstm_conv_w, mlstm_conv_b, mlstm_wq, mlstm_wk, mlstm_wv, mlstm_w_gates, mlstm_b_gates, mlstm_norm, w_mlstm_out, w_out, w_ffn2_up, w_ffn2_down):
    batch, seq, d_model = x_prompt.shape
    dec_batch, dec_seq, _ = x_sample.shape
    depth = w_in.shape[0]
    mix_w = w_hgrn_out.shape[1]
    n_prompt = batch * seq
    heads_hg = state_hgrn.shape[2]
    groups, state = s5_a_re.shape[1:]
    slabs = groups * state // LANES

    x = (x_prompt.reshape(n_prompt, d_model), x_sample.reshape(dec_batch * dec_seq, d_model))

    lb_all, ab_re, ab_im, bb_re, bb_im = _prep(hgrn_lower_bounds, s5_a_re, s5_a_im, s5_log_dt,
                                               s5_b_re, s5_b_im)

    col_su, col_mx, col_mo, col_gz = 4 * mix_w, 5 * mix_w, 6 * mix_w, 7 * mix_w
    dh = mix_w // ML_HEADS

    sample_init = dict(
        hg=state_hgrn,
        s5=(state_s5_re.reshape(depth, dec_batch, slabs, LANES),
            state_s5_im.reshape(depth, dec_batch, slabs, LANES)),
        conv=state_mlstm_conv,
        ml=(state_mlstm_c, state_mlstm_n.reshape(depth, dec_batch, ML_HEADS, 1, dh),
            jnp.broadcast_to(state_mlstm_m[..., None, None], (depth, dec_batch, ML_HEADS, 1, LANES))))
    groups_cfg = (
        dict(row0=0, nseq=batch, seq=seq, init=dict(hg=None, s5=None, conv=None, ml=None),
             s5=dict(nb=1, tl=128), mp=dict(nb=1, tl=256)),
        dict(row0=n_prompt, nseq=dec_batch, seq=dec_seq, init=sample_init,
             s5=dict(nb=16, tl=dec_seq), mp=dict(nb=16, tl=dec_seq)),
    )
    hgrn_cfg = dict(tl=512, c=16)
    mlrec_cfg = dict(tl=256, c=128, unroll=2)
    new = [dict(hg=None, s5=(None, None), conv=None, ml=(None, None, None)) for _ in groups_cfg]

    gains = norm_gains.reshape(depth * N_GAINS, 1, d_model)
    bf = lambda w: w.astype(BF16)
    w_ffn1_up, w_ffn1_down, w_ffn2_up, w_ffn2_down = map(bf, (w_ffn1_up, w_ffn1_down, w_ffn2_up, w_ffn2_down))
    w_hgrn_out, w_s5_glu_a, w_s5_glu_b, w_mlstm_out, w_out = map(
        bf, (w_hgrn_out, w_s5_glu_a, w_s5_glu_b, w_mlstm_out, w_out))

    for l in range(depth):
        x = _ffn(x, gains, 0, 1, w_ffn1_up, w_ffn1_down, layer=l, rows0=n_prompt)
        cols = _inproj(x, gains, 2, w_in, layer=l)

        wb, wc = _s5_weights(bb_re[l], bb_im[l], s5_c_re[l], s5_c_im[l])
        abr = ab_re[l].reshape(slabs, LANES)
        abi = ab_im[l].reshape(slabs, LANES)
        s5d = s5_d[l].reshape(1, mix_w)
        wq, wk, wv = (w[l].astype(BF16) for w in (mlstm_wq, mlstm_wk, mlstm_wv))
        wg = mlstm_w_gates[l].astype(BF16)

        both = dict(layer=l, batch=batch, seq=seq, short_nseq=dec_batch, short_seq=dec_seq,
                    short_row0=n_prompt)
        oa, oa_sample, new[0]['hg'], new[1]['hg'] = _hgrn(
            cols, lb_all, hgrn_norm, sample_init['hg'], (new[0]['hg'], new[1]['hg']),
            heads=heads_hg, **both, **hgrn_cfg)
        oa = lax.dynamic_update_slice(oa, oa_sample, (n_prompt, 0))

        ob = None
        qkvg = []
        for cfg, st in zip(groups_cfg, new):
            common = dict(layer=l, nseq=cfg['nseq'], row0=cfg['row0'])
            init = cfg['init']
            ob, *st['s5'] = _s5(cols, wb, wc, abr, abi, s5d, init['s5'], (ob, *st['s5']),
                                depth=depth, col0=col_su, seq=cfg['seq'], **common, **cfg['s5'])
            *group_qkvg, st['conv'] = _mlproj(
                cols, mlstm_conv_w[l], mlstm_conv_b[l], wq, wk, wv, wg, mlstm_b_gates[l],
                init['conv'], st['conv'], depth=depth, col0=col_mx, seq=cfg['seq'], **common,
                **cfg['mp'])
            qkvg.append(group_qkvg)
        oc, oc_sample, new[0]['ml'], new[1]['ml'] = _mlrec(
            qkvg[0], qkvg[1], cols, mlstm_norm[l], sample_init['ml'],
            tuple(new[0]['ml']) + tuple(new[1]['ml']), depth=depth, heads=ML_HEADS,
            col0=col_mo, **both, **mlrec_cfg)
        oc = lax.dynamic_update_slice(oc, oc_sample, (n_prompt, 0))

        x = _merge(x, oa, ob, oc, cols, w_hgrn_out, w_s5_glu_a, w_s5_glu_b, w_mlstm_out, w_out,
                   gains, 3, layer=l, col0=col_gz)
        x = _ffn(x, gains, 4, 5, w_ffn2_up, w_ffn2_down, layer=l, rows0=n_prompt,
                 split_out=l == depth - 1)

    y_prompt = x[0].reshape(batch, seq, d_model)
    y_sample = x[1].reshape(dec_batch, dec_seq, d_model)
    states = ()
    for cfg, st in zip(groups_cfg, new):
        nseq = cfg['nseq']
        c_new, n_new, m_new = st['ml']
        states += (st['hg'],
                   st['s5'][0].reshape(depth, nseq, groups, state),
                   st['s5'][1].reshape(depth, nseq, groups, state),
                   c_new, n_new.reshape(depth, nseq, ML_HEADS, dh), m_new[:, :, :, 0, 0], st['conv'])
    return (y_prompt, y_sample) + states
```

```python
import functools
import math

import jax
import jax.numpy as jnp
from jax import lax
from jax.experimental import pallas as pl
from jax.experimental.pallas import tpu as pltpu

F32 = jnp.float32
BF16 = jnp.bfloat16

EPS = 1e-6
NEG_BIG = -1e30
S5_DT_MIN_CLAMP = -1e-4

HG_DK = 128
S5_SUPER = 8
ML_HEADS = 4
CONV_W = 4
N_GAINS = 6

LANES = 128
SUBLANES = 8
VMEM_LIMIT = 56 * 1024 * 1024


def _cparams(sem):
    return pltpu.CompilerParams(dimension_semantics=sem, vmem_limit_bytes=VMEM_LIMIT)


def _rms_scale(y):
    return lax.rsqrt(jnp.mean(y * y, axis=-1, keepdims=True) + EPS)


def _silu(x):
    return x * jax.nn.sigmoid(x)


NORM_ROWS = 16


def _for_row_chunks(n_rows, body):
    def step(r, carry):
        body(pl.ds(pl.multiple_of(r * NORM_ROWS, NORM_ROWS), NORM_ROWS))
        return carry

    lax.fori_loop(0, n_rows // NORM_ROWS, step, 0, unroll=8)


def _prenorm_to(h_ref, x_ref, g_ref):
    g = g_ref[...]

    def body(rows):
        x = x_ref[rows, :]
        h_ref[rows, :] = (x * _rms_scale(x) * g).astype(BF16)

    _for_row_chunks(x_ref.shape[0], body)


def _postnorm_residual_to(o_ref, x_ref, y_ref, g_ref, weight):
    g = g_ref[...] if weight == 1.0 else g_ref[...] * weight

    def body(rows):
        y = y_ref[rows, :]
        o_ref[rows, :] = x_ref[rows, :] + y * _rms_scale(y) * g

    _for_row_chunks(x_ref.shape[0], body)


def _ffn_body(*refs, n_x, n_out, tiles0):
    x_refs = refs[:n_x]
    gpre_ref, gpost_ref, wa_ref, wb_ref, wd_ref = refs[n_x:n_x + 5]
    o_refs = refs[n_x + 5:n_x + 5 + n_out]
    h_ref, acc_ref = refs[n_x + 5 + n_out:]
    i = pl.program_id(0)
    j = pl.program_id(1)
    in_part = (i < tiles0, i >= tiles0)

    for part in range(n_x):
        @pl.when((j == 0) & in_part[part] if n_x == 2 else j == 0)
        def _():
            _prenorm_to(h_ref, x_refs[part], gpre_ref)

    @pl.when(j == 0)
    def _():
        acc_ref[...] = jnp.zeros_like(acc_ref)

    h = h_ref[...]
    a = jnp.dot(h, wa_ref[...], preferred_element_type=F32)
    b = jnp.dot(h, wb_ref[...], preferred_element_type=F32)
    act = (_silu(a) * b).astype(BF16)
    acc_ref[...] += jnp.dot(act, wd_ref[...], preferred_element_type=F32)

    last = j == pl.num_programs(1) - 1
    split = max(n_x, n_out) == 2
    for part in range(2 if split else 1):
        @pl.when(last & in_part[part] if split else last)
        def _():
            _postnorm_residual_to(o_refs[part if n_out == 2 else 0], x_refs[part if n_x == 2 else 0],
                                  acc_ref, gpost_ref, 0.5)


def _gain_spec(gains, layer, k):
    return pl.BlockSpec((None, 1, gains.shape[-1]), lambda i, j: (layer * N_GAINS + k, 0, 0))


def _ffn(x, gains, k_pre, k_post, w_up, w_down, *, layer, rows0, split_out=False, tm=512, tf=512):
    xs = x if isinstance(x, tuple) else (x,)
    d = xs[0].shape[1]
    m = sum(a.shape[0] for a in xs)
    d_ff = w_down.shape[1]
    nf = d_ff // tf
    tiles0 = rows0 // tm
    whole = pl.BlockSpec((tm, d), lambda i, j: (i, 0))
    parts = [pl.BlockSpec((tm, d), lambda i, j: (jnp.minimum(i, tiles0 - 1), 0)),
             pl.BlockSpec((tm, d), lambda i, j: (jnp.maximum(i - tiles0, 0), 0))]
    part_shapes = [jax.ShapeDtypeStruct((rows0, d), F32), jax.ShapeDtypeStruct((m - rows0, d), F32)]
    out = pl.pallas_call(
        functools.partial(_ffn_body, n_x=len(xs), n_out=2 if split_out else 1, tiles0=tiles0),
        grid=(m // tm, nf),
        in_specs=(parts if len(xs) == 2 else [whole]) + [
            _gain_spec(gains, layer, k_pre),
            _gain_spec(gains, layer, k_post),
            pl.BlockSpec((None, d, tf), lambda i, j: (layer, 0, j)),
            pl.BlockSpec((None, d, tf), lambda i, j: (layer, 0, j + nf)),
            pl.BlockSpec((None, tf, d), lambda i, j: (layer, j, 0)),
        ],
        out_specs=parts if split_out else whole,
        out_shape=part_shapes if split_out else jax.ShapeDtypeStruct((m, d), F32),
        scratch_shapes=[pltpu.VMEM((tm, d), BF16), pltpu.VMEM((tm, d), F32)],
        compiler_params=_cparams(("parallel", "arbitrary")),
        name="ffn",
    )(*xs, gains, gains, w_up, w_up, w_down)
    return tuple(out) if split_out else out


def _inproj_body(x_ref, g_ref, w_ref, o_ref, h_ref):
    @pl.when(pl.program_id(1) == 0)
    def _():
        _prenorm_to(h_ref, x_ref, g_ref)

    o_ref[...] = jnp.dot(h_ref[...], w_ref[...].astype(BF16), preferred_element_type=F32)


def _inproj(x, gains, k_gain, w, *, layer, tm=1024, tn=1024):
    m, d = x.shape
    n = w.shape[2]
    return pl.pallas_call(
        _inproj_body,
        grid=(m // tm, n // tn),
        in_specs=[
            pl.BlockSpec((tm, d), lambda i, j: (i, 0)),
            _gain_spec(gains, layer, k_gain),
            pl.BlockSpec((None, d, tn), lambda i, j: (layer, 0, j)),
        ],
        out_specs=pl.BlockSpec((tm, tn), lambda i, j: (i, j)),
        out_shape=jax.ShapeDtypeStruct((m, n), F32),
        scratch_shapes=[pltpu.VMEM((tm, d), BF16)],
        compiler_params=_cparams(("parallel", "arbitrary")),
        name="inproj",
    )(x, gains, w)


def _prep_body(lbraw_ref, are_ref, aim_ref, ldt_ref, bre_ref, bim_ref,
               lb_ref, abr_ref, abi_ref, bbr_ref, bbi_ref):
    depth = lbraw_ref.shape[0]
    raw = lbraw_ref[...]
    e = jnp.exp(raw - jnp.max(raw, axis=0, keepdims=True))
    lbs = e / jnp.sum(e, axis=0, keepdims=True)
    run = jnp.zeros_like(lbs[0:1])
    for l in range(depth):
        run = run + lbs[l:l + 1]
        lb_ref[l:l + 1, :] = run - lbs[0:1]

    for l in range(depth):
        dt = jnp.exp(ldt_ref[l])
        lam_re = jnp.minimum(are_ref[l], S5_DT_MIN_CLAMP)
        lam_im = aim_ref[l]
        mag = jnp.exp(lam_re * dt)
        ab_re = mag * jnp.cos(lam_im * dt)
        ab_im = mag * jnp.sin(lam_im * dt)
        inv = 1.0 / (lam_re * lam_re + lam_im * lam_im)
        f_re = ((ab_re - 1.0) * lam_re + ab_im * lam_im) * inv
        f_im = (ab_im * lam_re - (ab_re - 1.0) * lam_im) * inv
        abr_ref[l] = ab_re
        abi_ref[l] = ab_im
        b_re = bre_ref[l]
        b_im = bim_ref[l]
        bbr_ref[l] = f_re * b_re - f_im * b_im
        bbi_ref[l] = f_re * b_im + f_im * b_re


def _prep(lb_raw, a_re, a_im, log_dt, b_re, b_im):
    depth, groups, state = a_re.shape
    gp = groups * state
    ch = b_re.shape[-1]
    row = lambda a: a.reshape(depth, 1, gp)
    ldt = jnp.broadcast_to(log_dt[:, :, None], (depth, groups, state))
    chan_major = lambda b: b.transpose(0, 3, 1, 2).reshape(depth, ch, gp)
    out_shape = (
        jax.ShapeDtypeStruct(lb_raw.shape, F32),
        jax.ShapeDtypeStruct((depth, 1, gp), F32),
        jax.ShapeDtypeStruct((depth, 1, gp), F32),
        jax.ShapeDtypeStruct((depth, ch, gp), F32),
        jax.ShapeDtypeStruct((depth, ch, gp), F32),
    )
    return pl.pallas_call(_prep_body, out_shape=out_shape, name="prep")(
        lb_raw, row(a_re), row(a_im), row(ldt), chan_major(b_re), chan_major(b_im))


def _hgrn_stage_shapes(heads, c):
    return [pltpu.VMEM((heads, c * c, HG_DK), BF16), pltpu.VMEM((heads * c * c, LANES), F32),
            pltpu.VMEM((heads, c, HG_DK), BF16), pltpu.VMEM((heads, c, HG_DK), BF16),
            pltpu.VMEM((heads, c, HG_DK), BF16), pltpu.VMEM((heads, 1, HG_DK), F32),
            pltpu.VMEM((heads, c, HG_DK), F32)]


def _hgrn_run(q_ref, f_ref, i_ref, g_ref, lb_ref, nrm_ref, st_in_ref, st_out_ref, of_ref, o_ref,
              stage, *, nb, tl, c, heads, transposed):
    assert st_in_ref is st_out_ref or tl == c
    prod_ref, sums_ref, qc_ref, ke_ref, vb_ref, ce_ref, gs_ref = stage
    eye = (lax.broadcasted_iota(jnp.int32, (HG_DK, HG_DK), 0)
           == lax.broadcasted_iota(jnp.int32, (HG_DK, HG_DK), 1))
    row = lax.broadcasted_iota(jnp.int32, (c, HG_DK), 0)
    lag = row - lax.broadcasted_iota(jnp.int32, (c, HG_DK), 1)
    ones = jnp.ones((HG_DK, LANES), BF16)
    chunks_per_seq = tl // c
    n_steps = nb * chunks_per_seq
    head_cols = [slice(h * HG_DK, (h + 1) * HG_DK) for h in range(heads)]

    def chunk_rows(idx):
        return pl.ds(pl.multiple_of(idx * c, c), c)

    def prepare(idx):
        rows = chunk_rows(idx)
        for h, hs in enumerate(head_cols):
            lb = lb_ref[:, hs]
            fpre = f_ref[rows, hs]
            q = _silu(q_ref[rows, hs])
            forget = lb + (1.0 - lb) * jax.nn.sigmoid(fpre)
            kin = (1.0 - lb) * jax.nn.sigmoid(-fpre)

            cp = forget
            sh = 1
            while sh < c:
                cp = cp * jnp.where(row >= sh, pltpu.roll(cp, sh, 0), 1.0)
                sh *= 2
            sp = jnp.where(row < c - 1, pltpu.roll(forget, c - 1, 0), 1.0)
            sh = 1
            while sh < c:
                sp = sp * jnp.where(row < c - sh, pltpu.roll(sp, c - sh, 0), 1.0)
                sh *= 2

            kd = kin
            prods = [q * kd]
            for d in range(1, c):
                kd = pltpu.roll(kd, 1, 0) * forget
                prods.append(q * kd)
            prod_ref[h] = jnp.concatenate(prods, axis=0).astype(BF16)
            qc_ref[h] = (q * cp).astype(BF16)
            ke_ref[h] = (kin * sp).astype(BF16)
            vb_ref[h] = i_ref[rows, hs].astype(BF16)
            ce_ref[h] = cp[c - 1:c, :]
            gs_ref[h] = nrm_ref[:, hs] * _silu(g_ref[rows, hs])

    def finish(idx):
        n = idx // chunks_per_seq
        rows = chunk_rows(idx)
        sums_ref[...] = jnp.dot(prod_ref[...].reshape(heads * c * c, HG_DK), ones,
                                preferred_element_type=F32)
        inter = []
        for h in range(heads):
            st = st_in_ref[n, h]
            vb = vb_ref[h]
            if transposed:
                inter.append(lax.dot_general(qc_ref[h], st.astype(BF16), (((1,), (1,)), ((), ())),
                                             preferred_element_type=F32))
                upd = lax.dot_general(vb, ke_ref[h], (((0,), (0,)), ((), ())),
                                      preferred_element_type=F32)
                st_out_ref[n, h] = st * ce_ref[h] + upd
            else:
                inter.append(jnp.dot(qc_ref[h], st.astype(BF16), preferred_element_type=F32))
                upd = lax.dot_general(ke_ref[h], vb, (((0,), (0,)), ((), ())),
                                      preferred_element_type=F32)
                decay_col = jnp.sum(jnp.where(eye, ce_ref[h], 0.0), axis=1, keepdims=True)
                st_out_ref[n, h] = st * decay_col + upd
        outs = []
        for h in range(heads):
            scores = jnp.zeros((c, HG_DK), F32)
            for d in range(c):
                scores = jnp.where(lag == d, sums_ref[pl.ds((h * c + d) * c, c), :], scores)
            outs.append(inter[h] + jnp.dot(scores[:, :c].astype(BF16), vb_ref[h],
                                           preferred_element_type=F32))
        for hs, h in zip(head_cols, range(heads)):
            of_ref[rows, hs] = outs[h] * _rms_scale(outs[h]) * gs_ref[h]

    def step(idx, carry):
        finish(idx - 1)
        prepare(idx)
        return carry

    prepare(0)
    lax.fori_loop(1, n_steps, step, 0, unroll=True if n_steps <= SUBLANES else 2)
    finish(n_steps - 1)
    o_ref[...] = of_ref[...].astype(BF16)


def _hgrn_body(*refs, heads, n_prev, long_cfg, short_cfg):
    long_in, short_in = refs[0:4], refs[4:8]
    lb_ref, nrm_ref, s0_ref = refs[8:11]
    ol_ref, os_ref, sl_ref, ss_ref, stl_ref, ofl_ref, ofs_ref = refs[11 + n_prev:18 + n_prev]
    stage = refs[18 + n_prev:]
    stage_long, stage_short = stage[:len(stage) // 2], stage[len(stage) // 2:]
    t = pl.program_id(1)

    @pl.when(t == 0)
    def _():
        for h in range(heads):
            stl_ref[0, h] = jnp.zeros(stl_ref.shape[2:], F32)

    _hgrn_run(*long_in, lb_ref, nrm_ref, stl_ref, stl_ref, ofl_ref, ol_ref, stage_long,
              heads=heads, transposed=True, **long_cfg)

    @pl.when(t == pl.num_programs(1) - 1)
    def _():
        for h in range(heads):
            sl_ref[0, h] = stl_ref[0, h].T

    _hgrn_run(*short_in, lb_ref, nrm_ref, s0_ref, ss_ref, ofs_ref, os_ref, stage_short,
              heads=heads, transposed=False, **short_cfg)


def _layer_state_spec(shape, nb, layer):
    tail = (0,) * (len(shape) - 2)
    return pl.BlockSpec((None, nb) + tuple(shape[2:]), lambda b, t: (layer, b) + tail)


def _hgrn(cols, lb, nrm, short_init, prevs, *, layer, heads, batch, seq, tl, c,
          short_nseq, short_seq, short_row0):
    depth = lb.shape[0]
    dk = HG_DK
    mix_w = heads * dk
    dv = mix_w // heads
    nt = seq // tl
    nb_s = short_nseq // (batch * nt)
    blk_s = nb_s * short_seq
    rb0_s = short_row0 // blk_s
    step = lambda b, t: b * nt + t
    long_shape = (depth, batch, heads, dk, dv)
    short_shape = (depth, short_nseq, heads, dk, dv)

    def colspecs(blk, rb0):
        return [pl.BlockSpec((blk, mix_w), lambda b, t, g=g: (rb0 + step(b, t), g)) for g in range(4)]

    vec = pl.BlockSpec((None, 1, mix_w), lambda b, t: (layer, 0, 0))
    long_spec = _layer_state_spec(long_shape, 1, layer)
    short_spec = pl.BlockSpec((None, nb_s, heads, dk, dv), lambda b, t: (layer, step(b, t), 0, 0, 0))
    args = [cols] * 8 + [lb.reshape(depth, 1, mix_w), nrm.reshape(depth, 1, mix_w)]
    st_specs, st_args, aliases = _state_args([short_spec], (short_init,), (None, None) + tuple(prevs),
                                             len(args))
    out = pl.pallas_call(
        functools.partial(_hgrn_body, heads=heads, n_prev=len(aliases),
                          long_cfg=dict(nb=1, tl=tl, c=c),
                          short_cfg=dict(nb=nb_s, tl=short_seq, c=short_seq)),
        grid=(batch, nt),
        in_specs=colspecs(tl, 0) + colspecs(blk_s, rb0_s) + [vec, vec] + st_specs,
        out_specs=(pl.BlockSpec((tl, mix_w), lambda b, t: (step(b, t), 0)),
                   pl.BlockSpec((blk_s, mix_w), lambda b, t: (step(b, t), 0)),
                   long_spec, short_spec),
        out_shape=(jax.ShapeDtypeStruct((cols.shape[0], mix_w), BF16),
                   jax.ShapeDtypeStruct((short_nseq * short_seq, mix_w), BF16),
                   jax.ShapeDtypeStruct(long_shape, F32), jax.ShapeDtypeStruct(short_shape, F32)),
        scratch_shapes=[pltpu.VMEM((1, heads, dv, dk), F32), pltpu.VMEM((tl, mix_w), F32),
                        pltpu.VMEM((blk_s, mix_w), F32)]
                       + _hgrn_stage_shapes(heads, c) + _hgrn_stage_shapes(heads, short_seq),
        input_output_aliases=aliases,
        compiler_params=_cparams(("parallel", "arbitrary")),
        name="hgrn",
    )(*args, *st_args)
    return out


def _gelu_tanh(x):
    return 0.5 * x * (1.0 + jnp.tanh(math.sqrt(2.0 / math.pi) * (x + 0.044715 * (x * x * x))))


def _state_args(specs, inits, prevs, n_args):
    in_specs, args, aliases = [], [], {}
    if inits is not None:
        in_specs += list(specs)
        args += list(inits)
    for out_idx, prev in enumerate(prevs):
        if prev is not None:
            aliases[n_args + len(args)] = out_idx
            in_specs.append(pl.BlockSpec(memory_space=pl.ANY))
            args.append(prev)
    return in_specs, args, aliases


def _s5_body(*refs, nb, tl, has_init, n_prev):
    u_ref, wb_ref, wc_ref, abr_ref, abi_ref, d_ref = refs[:6]
    y_ref, xr_ref, xi_ref, sr_ref, si_ref = refs[6 + 2 * has_init + n_prev:]
    rows = nb * tl
    pitch = rows + SUBLANES
    nsb = wb_ref.shape[0]
    cw = wb_ref.shape[1]
    half = wb_ref.shape[2] // 2
    per = half // LANES
    slabs = nsb * per
    t = pl.program_id(1)

    @pl.when(t == 0)
    def _():
        if has_init:
            xr_ref[...] = refs[6][...]
            xi_ref[...] = refs[7][...]
        else:
            xr_ref[...] = jnp.zeros_like(xr_ref)
            xi_ref[...] = jnp.zeros_like(xi_ref)

    for sb in range(nsb):
        ub = u_ref[:, sb * cw:(sb + 1) * cw].astype(BF16)
        res = jnp.dot(ub, wb_ref[sb], preferred_element_type=F32)
        for j in range(per):
            base = (sb * per + j) * pitch
            sr_ref[pl.ds(base, rows), :] = res[:, j * LANES:(j + 1) * LANES]
            si_ref[pl.ds(base, rows), :] = res[:, half + j * LANES:half + (j + 1) * LANES]

    ar = abr_ref[...]
    ai = abi_ref[...]

    def advance(n, tt, xr, xi):
        at = pl.ds(n * tl + tt, slabs, stride=pitch)
        nxr = ar * xr - ai * xi + sr_ref[at, :]
        nxi = ar * xi + ai * xr + si_ref[at, :]
        sr_ref[at, :] = nxr
        si_ref[at, :] = nxi
        return nxr, nxi

    par = 2 if nb % 2 == 0 else 1

    def group_scan(p, carry):
        seqs = [par * p + i for i in range(par)]

        def step(tt, x):
            out = ()
            for i, n in enumerate(seqs):
                out += advance(n, tt, x[2 * i], x[2 * i + 1])
            return out

        x0 = ()
        for n in seqs:
            x0 += (xr_ref[n], xi_ref[n])
        x = lax.fori_loop(0, tl, step, x0, unroll=SUBLANES)
        for i, n in enumerate(seqs):
            xr_ref[n] = x[2 * i]
            xi_ref[n] = x[2 * i + 1]
        return carry

    lax.fori_loop(0, nb // par, group_scan, 0)

    for sb in range(nsb):
        parts = [sr_ref[pl.ds((sb * per + j) * pitch, rows), :] for j in range(per)]
        parts += [si_ref[pl.ds((sb * per + j) * pitch, rows), :] for j in range(per)]
        xs = jnp.concatenate(parts, axis=1).astype(BF16)
        cs = slice(sb * cw, (sb + 1) * cw)
        y = jnp.dot(xs, wc_ref[sb], preferred_element_type=F32) + d_ref[:, cs] * u_ref[:, cs]
        y_ref[:, cs] = _gelu_tanh(y).astype(BF16)


def _s5(cols, wb, wc, ab_re, ab_im, d, inits, prevs, *, layer, depth, nseq, row0, col0, seq, nb, tl):
    slabs = ab_re.shape[0]
    nt = seq // tl
    blk = nb * tl
    rb0 = row0 // blk
    mix_w = d.shape[-1]
    cb = col0 // mix_w
    const3 = lambda a: pl.BlockSpec(a.shape, lambda b, t: (0, 0, 0))
    const2 = lambda a: pl.BlockSpec(a.shape, lambda b, t: (0, 0))
    sshape = (depth, nseq, slabs, LANES)
    xspec = _layer_state_spec(sshape, nb, layer)
    pitch = blk + SUBLANES
    args = [cols, wb, wc, ab_re, ab_im, d]
    st_specs, st_args, aliases = _state_args([xspec, xspec], inits, prevs, len(args))
    return pl.pallas_call(
        functools.partial(_s5_body, nb=nb, tl=tl, has_init=inits is not None, n_prev=len(aliases)),
        grid=(nseq // nb, nt),
        in_specs=[pl.BlockSpec((blk, mix_w), lambda b, t: (rb0 + b * nt + t, cb)),
                  const3(wb), const3(wc), const2(ab_re), const2(ab_im), const2(d)] + st_specs,
        out_specs=(pl.BlockSpec((blk, mix_w), lambda b, t: (rb0 + b * nt + t, 0)), xspec, xspec),
        out_shape=(jax.ShapeDtypeStruct((cols.shape[0], mix_w), BF16),
                   jax.ShapeDtypeStruct(sshape, F32),
                   jax.ShapeDtypeStruct(sshape, F32)),
        scratch_shapes=[pltpu.VMEM((slabs * pitch, LANES), F32),
                        pltpu.VMEM((slabs * pitch, LANES), F32)],
        input_output_aliases=aliases,
        compiler_params=_cparams(("parallel", "arbitrary")),
        name="s5",
    )(*args, *st_args)


def _s5_weights(bb_re, bb_im, c_re, c_im):
    ch, gp = bb_re.shape
    groups = c_re.shape[0]
    state = gp // groups
    nsb = groups // S5_SUPER
    eye = jnp.eye(S5_SUPER, dtype=F32)

    def in_w(bb):
        b4 = bb.reshape(ch, nsb, S5_SUPER, state)
        return jnp.einsum('csgp,gh->sgchp', b4, eye).reshape(nsb, S5_SUPER * ch, S5_SUPER * state)

    def out_w(c):
        c4 = c.reshape(nsb, S5_SUPER, ch, state)
        return jnp.einsum('sgcp,gh->sgphc', c4, eye).reshape(nsb, S5_SUPER * state, S5_SUPER * ch)

    wb = jnp.concatenate([in_w(bb_re), in_w(bb_im)], axis=2).astype(BF16)
    wc = jnp.concatenate([out_w(c_re), -out_w(c_im)], axis=1).astype(BF16)
    return wb, wc


def _mlproj_body(*refs, nb, tl, has_init, n_prev):
    x_ref, cw_ref, cb_ref, wq_ref, wk_ref, wv_ref, wg_ref, bg_ref = refs[:8]
    q_ref, k_ref, v_ref, g_ref, cn_ref, xx_ref = refs[8 + has_init + n_prev:]
    t = pl.program_id(1)
    width = x_ref.shape[1]
    halo = CONV_W - 1
    lo = SUBLANES - halo

    @pl.when(t == 0)
    def _():
        if has_init:
            xx_ref[:, lo:SUBLANES, :] = refs[8][...]
        else:
            xx_ref[:, lo:SUBLANES, :] = jnp.zeros((nb, halo, width), F32)

    @pl.when(t > 0)
    def _():
        xx_ref[:, lo:SUBLANES, :] = xx_ref[:, tl + lo:tl + SUBLANES, :]

    x = x_ref[...]
    xx_ref[:, SUBLANES:, :] = x.reshape(nb, tl, width)
    xc = jnp.zeros((nb, tl, width), F32) + cb_ref[...][None]
    for j in range(CONV_W):
        xc = xc + xx_ref[:, lo + j:lo + j + tl, :] * cw_ref[j:j + 1, :][None]
    cn_ref[...] = xx_ref[:, tl + lo:tl + SUBLANES, :]
    xc = _silu(xc).reshape(nb * tl, width).astype(BF16)
    xb = x.astype(BF16)

    dh = width // ML_HEADS
    gates = jnp.zeros((nb * tl, 2 * ML_HEADS), F32) + bg_ref[...]
    for h in range(ML_HEADS):
        hs = slice(h * dh, (h + 1) * dh)
        q = jnp.dot(xc[:, hs], wq_ref[h], preferred_element_type=F32)
        k = jnp.dot(xc[:, hs], wk_ref[h], preferred_element_type=F32)
        v = jnp.dot(xb[:, hs], wv_ref[h], preferred_element_type=F32)
        q_ref[:, hs] = q
        k_ref[:, hs] = k
        v_ref[:, hs] = v
        for part, val in enumerate((q, k, v)):
            w = wg_ref[part * width + h * dh:part * width + (h + 1) * dh, :]
            gates = gates + jnp.dot(val.astype(BF16), w, preferred_element_type=F32)
    g_ref[...] = gates


def _mlproj(cols, conv_w, conv_b, wq, wk, wv, wg, bg, init, prev, *, layer, depth, nseq, row0, col0,
            seq, nb, tl):
    width = conv_w.shape[1]
    halo = CONV_W - 1
    nt = seq // tl
    blk = nb * tl
    rb0 = row0 // blk
    cb = col0 // width
    ng = wg.shape[1]
    full = lambda a: pl.BlockSpec(a.shape, lambda b, t: (0,) * a.ndim)
    rowspec = lambda w: pl.BlockSpec((blk, w), lambda b, t: (b * nt + t, 0))
    sshape = (depth, nseq, halo, width)
    cspec = _layer_state_spec(sshape, nb, layer)
    args = [cols, conv_w, conv_b.reshape(1, width), wq, wk, wv, wg, bg.reshape(1, ng)]
    st_specs, st_args, aliases = _state_args(
        [cspec], None if init is None else (init,), (None, None, None, None, prev), len(args))
    return pl.pallas_call(
        functools.partial(_mlproj_body, nb=nb, tl=tl, has_init=init is not None,
                          n_prev=len(aliases)),
        grid=(nseq // nb, nt),
        in_specs=[pl.BlockSpec((blk, width), lambda b, t: (rb0 + b * nt + t, cb))]
                 + [full(a) for a in args[1:]] + st_specs,
        out_specs=(rowspec(width), rowspec(width), rowspec(width), rowspec(ng), cspec),
        out_shape=(jax.ShapeDtypeStruct((nseq * seq, width), F32),) * 3
                  + (jax.ShapeDtypeStruct((nseq * seq, ng), F32),
                     jax.ShapeDtypeStruct(sshape, F32)),
        scratch_shapes=[pltpu.VMEM((nb, tl + SUBLANES, width), F32)],
        input_output_aliases=aliases,
        compiler_params=_cparams(("parallel", "arbitrary")),
        name="mlproj",
    )(*args, *st_args)


def _mlrec_run(q_ref, k_ref, v_ref, g_ref, o_ref, nrm_ref, c_ref, n_ref, m_ref, hf_ref, h_ref,
               *, nb, tl, c, heads, unroll):
    dh = q_ref.shape[1] // heads
    glane = lax.broadcasted_iota(jnp.int32, (c, 2 * heads), 1)
    r = lax.broadcasted_iota(jnp.int32, (c, c), 0)
    s = lax.broadcasted_iota(jnp.int32, (c, c), 1)
    causal = r >= s
    chunks_per_seq = tl // c
    k_scale = dh ** -0.5

    def chunk(idx, carry):
        n = idx // chunks_per_seq
        rows = pl.ds(pl.multiple_of(idx * c, c), c)
        gt = g_ref[rows, :]
        for h in range(heads):
            hs = slice(h * dh, (h + 1) * dh)
            q = q_ref[rows, hs]
            k = k_ref[rows, hs] * k_scale
            v = v_ref[rows, hs]
            ig_col = jnp.sum(jnp.where(glane == h, gt, 0.0), axis=1, keepdims=True)
            fp_col = jnp.sum(jnp.where(glane == h + heads, gt, 0.0), axis=1, keepdims=True)
            lf_col = jnp.minimum(fp_col, 0.0) - jnp.log1p(jnp.exp(-jnp.abs(fp_col)))
            lf_row = jnp.sum(jnp.where(r == s, lf_col, 0.0), axis=0, keepdims=True)
            ig_row = jnp.sum(jnp.where(r == s, ig_col, 0.0), axis=0, keepdims=True)
            b_col = jnp.sum(jnp.where(causal, lf_row, 0.0), axis=1, keepdims=True)
            b_row = jnp.sum(jnp.where(r <= s, lf_col, 0.0), axis=0, keepdims=True)

            m_prev = m_ref[n, h][:, 0:1]
            n_prev = n_ref[n, h]
            c_prev = c_ref[n, h]

            logw = jnp.where(causal, b_col - b_row + ig_row, NEG_BIG)
            m_t = jnp.maximum(b_col + m_prev, jnp.max(logw, axis=1, keepdims=True))
            w_prev = jnp.exp(b_col + m_prev - m_t)
            w = jnp.exp(logw - m_t)
            qb = q.astype(BF16)
            vb = v.astype(BF16)
            sc = lax.dot_general(qb, k.astype(BF16), (((1,), (1,)), ((), ())),
                                 preferred_element_type=F32) * w
            num = (w_prev * jnp.dot(qb, c_prev.astype(BF16), preferred_element_type=F32)
                   + jnp.dot(sc.astype(BF16), vb, preferred_element_type=F32))
            den = (w_prev * jnp.sum(q * n_prev, axis=1, keepdims=True)
                   + jnp.sum(sc, axis=1, keepdims=True))
            hval = num / jnp.maximum(jnp.abs(den), jnp.exp(-m_t))

            m_end = m_t[c - 1:c, :]
            b_end = b_col[c - 1:c, :]
            g_prev = jnp.exp(b_end + m_prev - m_end)
            w_in = jnp.exp(b_end - b_col + ig_col - m_end)
            kw = k * w_in
            c_ref[n, h] = g_prev * c_prev + lax.dot_general(
                kw.astype(BF16), vb, (((0,), (0,)), ((), ())), preferred_element_type=F32)
            n_ref[n, h] = g_prev * n_prev + jnp.sum(kw, axis=0, keepdims=True)
            m_ref[n, h] = jnp.broadcast_to(m_end, (1, LANES))
            hf_ref[rows, hs] = (hval * _rms_scale(hval) * nrm_ref[:, hs]
                                * jax.nn.sigmoid(o_ref[rows, hs]))
        return carry

    lax.fori_loop(0, nb * chunks_per_seq, chunk, 0, unroll=unroll)
    h_ref[...] = hf_ref[...].astype(BF16)


def _mlrec_body(*refs, heads, n_prev, long_cfg, short_cfg):
    long_in, short_in = refs[0:5], refs[5:10]
    nrm_ref = refs[10]
    c0_ref, n0_ref, m0_ref = refs[11:14]
    (hl_ref, hs_ref, cl_ref, nl_ref, ml_ref, cs_ref, ns_ref, ms_ref, hfl_ref, hfs_ref
     ) = refs[14 + n_prev:]

    @pl.when(pl.program_id(1) == 0)
    def _():
        cl_ref[...] = jnp.zeros_like(cl_ref)
        nl_ref[...] = jnp.zeros_like(nl_ref)
        ml_ref[...] = jnp.zeros_like(ml_ref)

    _mlrec_run(*long_in, nrm_ref, cl_ref, nl_ref, ml_ref, hfl_ref, hl_ref, heads=heads, **long_cfg)
    cs_ref[...] = c0_ref[...]
    ns_ref[...] = n0_ref[...]
    ms_ref[...] = m0_ref[...]
    _mlrec_run(*short_in, nrm_ref, cs_ref, ns_ref, ms_ref, hfs_ref, hs_ref, heads=heads, **short_cfg)


def _mlrec(long_qkvg, short_qkvg, cols, nrm, short_init, prevs, *, layer, depth, heads, col0,
           batch, seq, tl, c, short_nseq, short_seq, short_row0, unroll):
    width = long_qkvg[0].shape[1]
    dh = width // heads
    nt = seq // tl
    steps = batch * nt
    nb_s = short_nseq // steps
    blk_s = nb_s * short_seq
    cb = col0 // width
    ng = long_qkvg[3].shape[1]
    rb0_s = short_row0 // blk_s
    step = lambda b, t: b * nt + t

    def group_specs(blk, rb0):
        rows = pl.BlockSpec((blk, width), lambda b, t: (step(b, t), 0))
        return [rows, rows, rows,
                pl.BlockSpec((blk, ng), lambda b, t: (step(b, t), 0)),
                pl.BlockSpec((blk, width), lambda b, t: (rb0 + step(b, t), cb))]

    def state_shapes(nseq):
        return ((depth, nseq, heads, dh, dh), (depth, nseq, heads, 1, dh), (depth, nseq, heads, 1, LANES))

    long_specs = [_layer_state_spec(sh, 1, layer) for sh in state_shapes(batch)]
    short_specs = [pl.BlockSpec((None, nb_s) + sh[2:],
                                lambda b, t, nd=len(sh): (layer, step(b, t)) + (0,) * (nd - 2))
                   for sh in state_shapes(short_nseq)]
    args = [*long_qkvg, cols, *short_qkvg, cols, nrm.reshape(1, width)]
    st_specs, st_args, aliases = _state_args(short_specs, short_init, (None, None) + tuple(prevs),
                                             len(args))
    out = pl.pallas_call(
        functools.partial(_mlrec_body, heads=heads, n_prev=len(aliases),
                          long_cfg=dict(nb=1, tl=tl, c=c, unroll=unroll),
                          short_cfg=dict(nb=nb_s, tl=short_seq, c=short_seq, unroll=True)),
        grid=(batch, nt),
        in_specs=group_specs(tl, 0) + group_specs(blk_s, rb0_s)
                 + [pl.BlockSpec((1, width), lambda b, t: (0, 0))] + st_specs,
        out_specs=(pl.BlockSpec((tl, width), lambda b, t: (step(b, t), 0)),
                   pl.BlockSpec((blk_s, width), lambda b, t: (step(b, t), 0)),
                   *long_specs, *short_specs),
        out_shape=(jax.ShapeDtypeStruct((cols.shape[0], width), BF16),
                   jax.ShapeDtypeStruct((short_nseq * short_seq, width), BF16))
                  + tuple(jax.ShapeDtypeStruct(sh, F32) for sh in state_shapes(batch))
                  + tuple(jax.ShapeDtypeStruct(sh, F32) for sh in state_shapes(short_nseq)),
        scratch_shapes=[pltpu.VMEM((tl, width), F32), pltpu.VMEM((blk_s, width), F32)],
        input_output_aliases=aliases,
        compiler_params=_cparams(("parallel", "arbitrary")),
        name="mlrec",
    )(*args, *st_args)
    return out[0], out[1], out[2:5], out[5:8]


def _merge_body(x_ref, oa_ref, ob_ref, oc_ref, ga_ref, gb_ref, gc_ref, wa_ref, wba_ref, wbb_ref,
                wc_ref, wo_ref, g_ref, o_ref, acc_ref):
    j = pl.program_id(1)

    @pl.when(j == 0)
    def _():
        acc_ref[...] = jnp.zeros_like(acc_ref)

    yb = ob_ref[...]
    br_a = jnp.dot(oa_ref[...], wa_ref[...], preferred_element_type=F32)
    br_b = (jnp.dot(yb, wba_ref[...], preferred_element_type=F32)
            * jax.nn.sigmoid(jnp.dot(yb, wbb_ref[...], preferred_element_type=F32)))
    br_c = jnp.dot(oc_ref[...], wc_ref[...], preferred_element_type=F32)
    merged = (jax.nn.sigmoid(ga_ref[...]) * br_a + jax.nn.sigmoid(gb_ref[...]) * br_b
              + jax.nn.sigmoid(gc_ref[...]) * br_c)
    acc_ref[...] += jnp.dot(merged.astype(BF16), wo_ref[...], preferred_element_type=F32)

    @pl.when(j == pl.num_programs(1) - 1)
    def _():
        _postnorm_residual_to(o_ref, x_ref, acc_ref, g_ref, 1.0)


def _merge(x, oa, ob, oc, cols, w_a, w_ba, w_bb, w_c, w_out, gains, k_gain, *, layer, col0,
           tm=512, tn=512):
    m, d = x.shape
    mix_w = oa.shape[1]
    nj = d // tn
    gb0 = col0 // tn
    act = pl.BlockSpec((tm, mix_w), lambda i, j: (i, 0))
    wspec = pl.BlockSpec((None, mix_w, tn), lambda i, j: (layer, 0, j))

    def gate(branch):
        return pl.BlockSpec((tm, tn), lambda i, j: (i, gb0 + branch * nj + j))

    return pl.pallas_call(
        _merge_body,
        grid=(m // tm, nj),
        in_specs=[pl.BlockSpec((tm, d), lambda i, j: (i, 0)), act, act, act,
                  gate(0), gate(1), gate(2), wspec, wspec, wspec, wspec,
                  pl.BlockSpec((None, tn, d), lambda i, j: (layer, j, 0)),
                  _gain_spec(gains, layer, k_gain)],
        out_specs=pl.BlockSpec((tm, d), lambda i, j: (i, 0)),
        out_shape=jax.ShapeDtypeStruct((m, d), F32),
        scratch_shapes=[pltpu.VMEM((tm, d), F32)],
        compiler_params=_cparams(("parallel", "arbitrary")),
        name="merge",
    )(x, oa, ob, oc, cols, cols, cols, w_a, w_ba, w_bb, w_c, w_out, gains)


def kernel(x_prompt, x_sample, state_hgrn, state_s5_re, state_s5_im, state_mlstm_c, state_mlstm_n, state_mlstm_m, state_mlstm_conv, norm_gains, w_ffn1_up, w_ffn1_down, w_in, hgrn_lower_bounds, hgrn_norm, w_hgrn_out, s5_a_re, s5_a_im, s5_log_dt, s5_b_re, s5_b_im, s5_c_re, s5_c_im, s5_d, w_s5_glu_a, w_s5_glu_b, mlstm_conv_w, mlstm_conv_b, mlstm_wq, mlstm_wk, mlstm_wv, mlstm_w_gates, mlstm_b_gates, mlstm_norm, w_mlstm_out, w_out, w_ffn2_up, w_ffn2_down):
    batch, seq, d_model = x_prompt.shape
    dec_batch, dec_seq, _ = x_sample.shape
    depth = w_in.shape[0]
    mix_w = w_hgrn_out.shape[1]
    n_prompt = batch * seq
    heads_hg = state_hgrn.shape[2]
    groups, state = s5_a_re.shape[1:]
    slabs = groups * state // LANES

    x = (x_prompt.reshape(n_prompt, d_model), x_sample.reshape(dec_batch * dec_seq, d_model))

    lb_all, ab_re, ab_im, bb_re, bb_im = _prep(hgrn_lower_bounds, s5_a_re, s5_a_im, s5_log_dt,
                                               s5_b_re, s5_b_im)

    col_su, col_mx, col_mo, col_gz = 4 * mix_w, 5 * mix_w, 6 * mix_w, 7 * mix_w
    dh = mix_w // ML_HEADS

    sample_init = dict(
        hg=state_hgrn,
        s5=(state_s5_re.reshape(depth, dec_batch, slabs, LANES),
            state_s5_im.reshape(depth, dec_batch, slabs, LANES)),
        conv=state_mlstm_conv,
        ml=(state_mlstm_c, state_mlstm_n.reshape(depth, dec_batch, ML_HEADS, 1, dh),
            jnp.broadcast_to(state_mlstm_m[..., None, None], (depth, dec_batch, ML_HEADS, 1, LANES))))
    groups_cfg = (
        dict(row0=0, nseq=batch, seq=seq, init=dict(s5=None, conv=None),
             s5=dict(nb=1, tl=256), mp=dict(nb=1, tl=512)),
        dict(row0=n_prompt, nseq=dec_batch, seq=dec_seq, init=sample_init,
             s5=dict(nb=16, tl=dec_seq), mp=dict(nb=16, tl=dec_seq)),
    )
    hgrn_cfg = dict(tl=512, c=16)
    mlrec_cfg = dict(tl=256, c=128, unroll=2)
    new = [dict(hg=None, s5=(None, None), conv=None, ml=(None, None, None)) for _ in groups_cfg]

    gains = norm_gains.reshape(depth * N_GAINS, 1, d_model)
    bf = lambda w: w.astype(BF16)
    w_ffn1_up, w_ffn1_down, w_ffn2_up, w_ffn2_down = map(bf, (w_ffn1_up, w_ffn1_down, w_ffn2_up, w_ffn2_down))
    w_hgrn_out, w_s5_glu_a, w_s5_glu_b, w_mlstm_out, w_out = map(
        bf, (w_hgrn_out, w_s5_glu_a, w_s5_glu_b, w_mlstm_out, w_out))

    for l in range(depth):
        x = _ffn(x, gains, 0, 1, w_ffn1_up, w_ffn1_down, layer=l, rows0=n_prompt)
        cols = _inproj(x, gains, 2, w_in, layer=l)

        wb, wc = _s5_weights(bb_re[l], bb_im[l], s5_c_re[l], s5_c_im[l])
        abr = ab_re[l].reshape(slabs, LANES)
        abi = ab_im[l].reshape(slabs, LANES)
        s5d = s5_d[l].reshape(1, mix_w)
        wq, wk, wv = (w[l].astype(BF16) for w in (mlstm_wq, mlstm_wk, mlstm_wv))
        wg = mlstm_w_gates[l].astype(BF16)

        both = dict(layer=l, batch=batch, seq=seq, short_nseq=dec_batch, short_seq=dec_seq,
                    short_row0=n_prompt)
        oa, oa_sample, new[0]['hg'], new[1]['hg'] = _hgrn(
            cols, lb_all, hgrn_norm, sample_init['hg'], (new[0]['hg'], new[1]['hg']),
            heads=heads_hg, **both, **hgrn_cfg)
        oa = lax.dynamic_update_slice(oa, oa_sample, (n_prompt, 0))

        ob = None
        qkvg = []
        for cfg, st in zip(groups_cfg, new):
            common = dict(layer=l, nseq=cfg['nseq'], row0=cfg['row0'])
            init = cfg['init']
            ob, *st['s5'] = _s5(cols, wb, wc, abr, abi, s5d, init['s5'], (ob, *st['s5']),
                                depth=depth, col0=col_su, seq=cfg['seq'], **common, **cfg['s5'])
            *group_qkvg, st['conv'] = _mlproj(
                cols, mlstm_conv_w[l], mlstm_conv_b[l], wq, wk, wv, wg, mlstm_b_gates[l],
                init['conv'], st['conv'], depth=depth, col0=col_mx, seq=cfg['seq'], **common,
                **cfg['mp'])
            qkvg.append(group_qkvg)
        oc, oc_sample, new[0]['ml'], new[1]['ml'] = _mlrec(
            qkvg[0], qkvg[1], cols, mlstm_norm[l], sample_init['ml'],
            tuple(new[0]['ml']) + tuple(new[1]['ml']), depth=depth, heads=ML_HEADS,
            col0=col_mo, **both, **mlrec_cfg)
        oc = lax.dynamic_update_slice(oc, oc_sample, (n_prompt, 0))

        x = _merge(x, oa, ob, oc, cols, w_hgrn_out, w_s5_glu_a, w_s5_glu_b, w_mlstm_out, w_out,
                   gains, 3, layer=l, col0=col_gz)
        x = _ffn(x, gains, 4, 5, w_ffn2_up, w_ffn2_down, layer=l, rows0=n_prompt,
                 split_out=l == depth - 1)

    y_prompt = x[0].reshape(batch, seq, d_model)
    y_sample = x[1].reshape(dec_batch, dec_seq, d_model)
    states = ()
    for cfg, st in zip(groups_cfg, new):
        nseq = cfg['nseq']
        c_new, n_new, m_new = st['ml']
        states += (st['hg'],
                   st['s5'][0].reshape(depth, nseq, groups, state),
                   st['s5'][1].reshape(depth, nseq, groups, state),
                   c_new, n_new.reshape(depth, nseq, ML_HEADS, dh), m_new[:, :, :, 0, 0], st['conv'])
    return (y_prompt, y_sample) + states
```

```python
import functools
import math

import jax
import jax.numpy as jnp
from jax import lax
from jax.experimental import pallas as pl
from jax.experimental.pallas import tpu as pltpu

F32 = jnp.float32
BF16 = jnp.bfloat16

EPS = 1e-6
NEG_BIG = -1e30
S5_DT_MIN_CLAMP = -1e-4

HG_DK = 128
S5_SUPER = 8
ML_HEADS = 4
CONV_W = 4
N_GAINS = 6

LANES = 128
SUBLANES = 8
VMEM_LIMIT = 56 * 1024 * 1024


def _cparams(sem):
    return pltpu.CompilerParams(dimension_semantics=sem, vmem_limit_bytes=VMEM_LIMIT)


def _rms_scale(y):
    return lax.rsqrt(jnp.mean(y * y, axis=-1, keepdims=True) + EPS)


def _silu(x):
    return x * jax.nn.sigmoid(x)


NORM_ROWS = 16


def _for_row_chunks(n_rows, body):
    def step(r, carry):
        body(pl.ds(pl.multiple_of(r * NORM_ROWS, NORM_ROWS), NORM_ROWS))
        return carry

    lax.fori_loop(0, n_rows // NORM_ROWS, step, 0, unroll=8)


def _prenorm_to(h_ref, x_ref, g_ref):
    g = g_ref[...]

    def body(rows):
        x = x_ref[rows, :]
        h_ref[rows, :] = (x * _rms_scale(x) * g).astype(BF16)

    _for_row_chunks(x_ref.shape[0], body)


def _postnorm_residual_to(o_ref, x_ref, y_ref, g_ref, weight, next_ref=None, gnext_ref=None):
    g = g_ref[...] if weight == 1.0 else g_ref[...] * weight
    gnext = None if next_ref is None else gnext_ref[...]

    def body(rows):
        y = y_ref[rows, :]
        o = x_ref[rows, :] + y * _rms_scale(y) * g
        o_ref[rows, :] = o
        if next_ref is not None:
            next_ref[rows, :] = (o * _rms_scale(o) * gnext).astype(BF16)

    _for_row_chunks(x_ref.shape[0], body)


def _ffn_body(*refs, n_x, n_out, n_next, tiles0):
    x_refs = refs[:n_x]
    gpre_ref, gpost_ref = refs[n_x:n_x + 2]
    gnext_ref = refs[n_x + 2] if n_next else None
    wa_ref, wb_ref, wd_ref = refs[n_x + 2 + n_next:n_x + 5 + n_next]
    o_refs = refs[n_x + 5 + n_next:n_x + 5 + n_next + n_out]
    next_ref = refs[n_x + 5 + n_next + n_out] if n_next else None
    h_ref, acc_ref = refs[n_x + 5 + 2 * n_next + n_out:]
    i = pl.program_id(0)
    j = pl.program_id(1)
    in_part = (i < tiles0, i >= tiles0)

    for part in range(n_x):
        @pl.when((j == 0) & in_part[part] if n_x == 2 else j == 0)
        def _():
            _prenorm_to(h_ref, x_refs[part], gpre_ref)

    @pl.when(j == 0)
    def _():
        acc_ref[...] = jnp.zeros_like(acc_ref)

    h = h_ref[...]
    a = jnp.dot(h, wa_ref[...], preferred_element_type=F32)
    b = jnp.dot(h, wb_ref[...], preferred_element_type=F32)
    act = (_silu(a) * b).astype(BF16)
    acc_ref[...] += jnp.dot(act, wd_ref[...], preferred_element_type=F32)

    last = j == pl.num_programs(1) - 1
    split = max(n_x, n_out) == 2
    for part in range(2 if split else 1):
        @pl.when(last & in_part[part] if split else last)
        def _():
            _postnorm_residual_to(o_refs[part if n_out == 2 else 0], x_refs[part if n_x == 2 else 0],
                                  acc_ref, gpost_ref, 0.5, next_ref, gnext_ref)


def _gain_spec(gains, layer, k):
    return pl.BlockSpec((None, 1, gains.shape[-1]), lambda i, j: (layer * N_GAINS + k, 0, 0))


def _ffn(x, gains, k_pre, k_post, w_up, w_down, *, layer, rows0, k_next=None, split_out=False,
         tm=512, tf=512):
    xs = x if isinstance(x, tuple) else (x,)
    d = xs[0].shape[1]
    m = sum(a.shape[0] for a in xs)
    d_ff = w_down.shape[1]
    nf = d_ff // tf
    tiles0 = rows0 // tm
    n_next = 0 if k_next is None else 1
    assert not (split_out and n_next)
    whole = pl.BlockSpec((tm, d), lambda i, j: (i, 0))
    parts = [pl.BlockSpec((tm, d), lambda i, j: (jnp.minimum(i, tiles0 - 1), 0)),
             pl.BlockSpec((tm, d), lambda i, j: (jnp.maximum(i - tiles0, 0), 0))]
    part_shapes = [jax.ShapeDtypeStruct((rows0, d), F32), jax.ShapeDtypeStruct((m - rows0, d), F32)]
    if split_out:
        out_specs, out_shape = parts, part_shapes
    elif n_next:
        out_specs = [whole, whole]
        out_shape = [jax.ShapeDtypeStruct((m, d), F32), jax.ShapeDtypeStruct((m, d), BF16)]
    else:
        out_specs, out_shape = whole, jax.ShapeDtypeStruct((m, d), F32)
    gain_specs = [_gain_spec(gains, layer, k) for k in (k_pre, k_post) + ((k_next,) if n_next else ())]
    out = pl.pallas_call(
        functools.partial(_ffn_body, n_x=len(xs), n_out=2 if split_out else 1, n_next=n_next,
                          tiles0=tiles0),
        grid=(m // tm, nf),
        in_specs=(parts if len(xs) == 2 else [whole]) + gain_specs + [
            pl.BlockSpec((None, d, tf), lambda i, j: (layer, 0, j)),
            pl.BlockSpec((None, d, tf), lambda i, j: (layer, 0, j + nf)),
            pl.BlockSpec((None, tf, d), lambda i, j: (layer, j, 0)),
        ],
        out_specs=out_specs,
        out_shape=out_shape,
        scratch_shapes=[pltpu.VMEM((tm, d), BF16), pltpu.VMEM((tm, d), F32)],
        compiler_params=_cparams(("parallel", "arbitrary")),
        name="ffn",
    )(*xs, *([gains] * len(gain_specs)), w_up, w_up, w_down)
    return tuple(out) if (split_out or n_next) else out


def _inproj_body(h_ref, w_ref, o_ref, wb_ref):
    @pl.when(pl.program_id(1) == 0)
    def _():
        wb_ref[...] = w_ref[...].astype(BF16)

    o_ref[...] = jnp.dot(h_ref[...], wb_ref[...], preferred_element_type=F32)


def _inproj(h, w, *, layer, tm=1024, tn=1024):
    m, d = h.shape
    n = w.shape[2]
    return pl.pallas_call(
        _inproj_body,
        grid=(n // tn, m // tm),
        in_specs=[
            pl.BlockSpec((tm, d), lambda j, i: (i, 0)),
            pl.BlockSpec((None, d, tn), lambda j, i: (layer, 0, j)),
        ],
        out_specs=pl.BlockSpec((tm, tn), lambda j, i: (i, j)),
        out_shape=jax.ShapeDtypeStruct((m, n), F32),
        scratch_shapes=[pltpu.VMEM((d, tn), BF16)],
        compiler_params=_cparams(("parallel", "arbitrary")),
        name="inproj",
    )(h, w)


def _prep_body(lbraw_ref, are_ref, aim_ref, ldt_ref, bre_ref, bim_ref,
               lb_ref, abr_ref, abi_ref, bbr_ref, bbi_ref):
    depth = lbraw_ref.shape[0]
    raw = lbraw_ref[...]
    e = jnp.exp(raw - jnp.max(raw, axis=0, keepdims=True))
    lbs = e / jnp.sum(e, axis=0, keepdims=True)
    run = jnp.zeros_like(lbs[0:1])
    for l in range(depth):
        run = run + lbs[l:l + 1]
        lb_ref[l:l + 1, :] = run - lbs[0:1]

    for l in range(depth):
        dt = jnp.exp(ldt_ref[l])
        lam_re = jnp.minimum(are_ref[l], S5_DT_MIN_CLAMP)
        lam_im = aim_ref[l]
        mag = jnp.exp(lam_re * dt)
        ab_re = mag * jnp.cos(lam_im * dt)
        ab_im = mag * jnp.sin(lam_im * dt)
        inv = 1.0 / (lam_re * lam_re + lam_im * lam_im)
        f_re = ((ab_re - 1.0) * lam_re + ab_im * lam_im) * inv
        f_im = (ab_im * lam_re - (ab_re - 1.0) * lam_im) * inv
        abr_ref[l] = ab_re
        abi_ref[l] = ab_im
        b_re = bre_ref[l]
        b_im = bim_ref[l]
        bbr_ref[l] = f_re * b_re - f_im * b_im
        bbi_ref[l] = f_re * b_im + f_im * b_re


def _prep(lb_raw, a_re, a_im, log_dt, b_re, b_im):
    depth, groups, state = a_re.shape
    gp = groups * state
    ch = b_re.shape[-1]
    row = lambda a: a.reshape(depth, 1, gp)
    ldt = jnp.broadcast_to(log_dt[:, :, None], (depth, groups, state))
    chan_major = lambda b: b.transpose(0, 3, 1, 2).reshape(depth, ch, gp)
    out_shape = (
        jax.ShapeDtypeStruct(lb_raw.shape, F32),
        jax.ShapeDtypeStruct((depth, 1, gp), F32),
        jax.ShapeDtypeStruct((depth, 1, gp), F32),
        jax.ShapeDtypeStruct((depth, ch, gp), F32),
        jax.ShapeDtypeStruct((depth, ch, gp), F32),
    )
    return pl.pallas_call(_prep_body, out_shape=out_shape, name="prep")(
        lb_raw, row(a_re), row(a_im), row(ldt), chan_major(b_re), chan_major(b_im))


def _hgrn_stage_shapes(heads, c):
    return [pltpu.VMEM((heads, c * c, HG_DK), BF16), pltpu.VMEM((heads * c * c, LANES), F32),
            pltpu.VMEM((heads, c, HG_DK), BF16), pltpu.VMEM((heads, c, HG_DK), BF16),
            pltpu.VMEM((heads, c, HG_DK), BF16), pltpu.VMEM((heads, 1, HG_DK), F32),
            pltpu.VMEM((heads, c, HG_DK), F32)]


def _hgrn_run(q_ref, f_ref, i_ref, g_ref, lb_ref, nrm_ref, st_in_ref, st_out_ref, of_ref, o_ref,
              stage, *, nb, tl, c, heads, transposed):
    assert st_in_ref is st_out_ref or tl == c
    prod_ref, sums_ref, qc_ref, ke_ref, vb_ref, ce_ref, gs_ref = stage
    eye = (lax.broadcasted_iota(jnp.int32, (HG_DK, HG_DK), 0)
           == lax.broadcasted_iota(jnp.int32, (HG_DK, HG_DK), 1))
    row = lax.broadcasted_iota(jnp.int32, (c, HG_DK), 0)
    lag = row - lax.broadcasted_iota(jnp.int32, (c, HG_DK), 1)
    ones = jnp.ones((HG_DK, LANES), BF16)
    chunks_per_seq = tl // c
    n_steps = nb * chunks_per_seq
    head_cols = [slice(h * HG_DK, (h + 1) * HG_DK) for h in range(heads)]

    def chunk_rows(idx):
        return pl.ds(pl.multiple_of(idx * c, c), c)

    def prepare(idx):
        rows = chunk_rows(idx)
        for h, hs in enumerate(head_cols):
            lb = lb_ref[:, hs]
            fpre = f_ref[rows, hs]
            q = _silu(q_ref[rows, hs])
            forget = lb + (1.0 - lb) * jax.nn.sigmoid(fpre)
            kin = (1.0 - lb) * jax.nn.sigmoid(-fpre)

            cp = forget
            sh = 1
            while sh < c:
                cp = cp * jnp.where(row >= sh, pltpu.roll(cp, sh, 0), 1.0)
                sh *= 2
            sp = jnp.where(row < c - 1, pltpu.roll(forget, c - 1, 0), 1.0)
            sh = 1
            while sh < c:
                sp = sp * jnp.where(row < c - sh, pltpu.roll(sp, c - sh, 0), 1.0)
                sh *= 2

            kd = kin
            prods = [q * kd]
            for d in range(1, c):
                kd = pltpu.roll(kd, 1, 0) * forget
                prods.append(q * kd)
            prod_ref[h] = jnp.concatenate(prods, axis=0).astype(BF16)
            qc_ref[h] = (q * cp).astype(BF16)
            ke_ref[h] = (kin * sp).astype(BF16)
            vb_ref[h] = i_ref[rows, hs].astype(BF16)
            ce_ref[h] = cp[c - 1:c, :]
            gs_ref[h] = nrm_ref[:, hs] * _silu(g_ref[rows, hs])

    def finish(idx):
        n = idx // chunks_per_seq
        rows = chunk_rows(idx)
        sums_ref[...] = jnp.dot(prod_ref[...].reshape(heads * c * c, HG_DK), ones,
                                preferred_element_type=F32)
        inter = []
        for h in range(heads):
            st = st_in_ref[n, h]
            vb = vb_ref[h]
            if transposed:
                inter.append(lax.dot_general(qc_ref[h], st.astype(BF16), (((1,), (1,)), ((), ())),
                                             preferred_element_type=F32))
                upd = lax.dot_general(vb, ke_ref[h], (((0,), (0,)), ((), ())),
                                      preferred_element_type=F32)
                st_out_ref[n, h] = st * ce_ref[h] + upd
            else:
                inter.append(jnp.dot(qc_ref[h], st.astype(BF16), preferred_element_type=F32))
                upd = lax.dot_general(ke_ref[h], vb, (((0,), (0,)), ((), ())),
                                      preferred_element_type=F32)
                decay_col = jnp.sum(jnp.where(eye, ce_ref[h], 0.0), axis=1, keepdims=True)
                st_out_ref[n, h] = st * decay_col + upd
        outs = []
        for h in range(heads):
            scores = jnp.zeros((c, HG_DK), F32)
            for d in range(c):
                scores = jnp.where(lag == d, sums_ref[pl.ds((h * c + d) * c, c), :], scores)
            outs.append(inter[h] + jnp.dot(scores[:, :c].astype(BF16), vb_ref[h],
                                           preferred_element_type=F32))
        for hs, h in zip(head_cols, range(heads)):
            of_ref[rows, hs] = outs[h] * _rms_scale(outs[h]) * gs_ref[h]

    def step(idx, carry):
        finish(idx - 1)
        prepare(idx)
        return carry

    prepare(0)
    lax.fori_loop(1, n_steps, step, 0, unroll=True if n_steps <= SUBLANES else 2)
    finish(n_steps - 1)
    o_ref[...] = of_ref[...].astype(BF16)


def _hgrn_body(*refs, heads, n_prev, long_cfg, short_cfg):
    long_in, short_in = refs[0:4], refs[4:8]
    lb_ref, nrm_ref, s0_ref = refs[8:11]
    ol_ref, os_ref, sl_ref, ss_ref, stl_ref, ofl_ref, ofs_ref = refs[11 + n_prev:18 + n_prev]
    stage = refs[18 + n_prev:]
    stage_long, stage_short = stage[:len(stage) // 2], stage[len(stage) // 2:]
    t = pl.program_id(1)

    @pl.when(t == 0)
    def _():
        for h in range(heads):
            stl_ref[0, h] = jnp.zeros(stl_ref.shape[2:], F32)

    _hgrn_run(*long_in, lb_ref, nrm_ref, stl_ref, stl_ref, ofl_ref, ol_ref, stage_long,
              heads=heads, transposed=True, **long_cfg)

    @pl.when(t == pl.num_programs(1) - 1)
    def _():
        for h in range(heads):
            sl_ref[0, h] = stl_ref[0, h].T

    _hgrn_run(*short_in, lb_ref, nrm_ref, s0_ref, ss_ref, ofs_ref, os_ref, stage_short,
              heads=heads, transposed=False, **short_cfg)


def _layer_state_spec(shape, nb, layer):
    tail = (0,) * (len(shape) - 2)
    return pl.BlockSpec((None, nb) + tuple(shape[2:]), lambda b, t: (layer, b) + tail)


def _hgrn(cols, lb, nrm, short_init, prevs, *, layer, heads, batch, seq, tl, c,
          short_nseq, short_seq, short_row0):
    depth = lb.shape[0]
    dk = HG_DK
    mix_w = heads * dk
    dv = mix_w // heads
    nt = seq // tl
    nb_s = short_nseq // (batch * nt)
    blk_s = nb_s * short_seq
    rb0_s = short_row0 // blk_s
    step = lambda b, t: b * nt + t
    long_shape = (depth, batch, heads, dk, dv)
    short_shape = (depth, short_nseq, heads, dk, dv)

    def colspecs(blk, rb0):
        return [pl.BlockSpec((blk, mix_w), lambda b, t, g=g: (rb0 + step(b, t), g)) for g in range(4)]

    vec = pl.BlockSpec((None, 1, mix_w), lambda b, t: (layer, 0, 0))
    long_spec = _layer_state_spec(long_shape, 1, layer)
    short_spec = pl.BlockSpec((None, nb_s, heads, dk, dv), lambda b, t: (layer, step(b, t), 0, 0, 0))
    args = [cols] * 8 + [lb.reshape(depth, 1, mix_w), nrm.reshape(depth, 1, mix_w)]
    st_specs, st_args, aliases = _state_args([short_spec], (short_init,), (None, None) + tuple(prevs),
                                             len(args))
    out = pl.pallas_call(
        functools.partial(_hgrn_body, heads=heads, n_prev=len(aliases),
                          long_cfg=dict(nb=1, tl=tl, c=c),
                          short_cfg=dict(nb=nb_s, tl=short_seq, c=short_seq)),
        grid=(batch, nt),
        in_specs=colspecs(tl, 0) + colspecs(blk_s, rb0_s) + [vec, vec] + st_specs,
        out_specs=(pl.BlockSpec((tl, mix_w), lambda b, t: (step(b, t), 0)),
                   pl.BlockSpec((blk_s, mix_w), lambda b, t: (step(b, t), 0)),
                   long_spec, short_spec),
        out_shape=(jax.ShapeDtypeStruct((cols.shape[0], mix_w), BF16),
                   jax.ShapeDtypeStruct((short_nseq * short_seq, mix_w), BF16),
                   jax.ShapeDtypeStruct(long_shape, F32), jax.ShapeDtypeStruct(short_shape, F32)),
        scratch_shapes=[pltpu.VMEM((1, heads, dv, dk), F32), pltpu.VMEM((tl, mix_w), F32),
                        pltpu.VMEM((blk_s, mix_w), F32)]
                       + _hgrn_stage_shapes(heads, c) + _hgrn_stage_shapes(heads, short_seq),
        input_output_aliases=aliases,
        compiler_params=_cparams(("parallel", "arbitrary")),
        name="hgrn",
    )(*args, *st_args)
    return out


def _gelu_tanh(x):
    return 0.5 * x * (1.0 + jnp.tanh(math.sqrt(2.0 / math.pi) * (x + 0.044715 * (x * x * x))))


def _state_args(specs, inits, prevs, n_args):
    in_specs, args, aliases = [], [], {}
    if inits is not None:
        in_specs += list(specs)
        args += list(inits)
    for out_idx, prev in enumerate(prevs):
        if prev is not None:
            aliases[n_args + len(args)] = out_idx
            in_specs.append(pl.BlockSpec(memory_space=pl.ANY))
            args.append(prev)
    return in_specs, args, aliases


def _s5_body(*refs, nb, tl, has_init, n_prev):
    u_ref, wb_ref, wc_ref, abr_ref, abi_ref, d_ref = refs[:6]
    y_ref, xr_ref, xi_ref, sr_ref, si_ref = refs[6 + 2 * has_init + n_prev:]
    rows = nb * tl
    pitch = rows + SUBLANES
    nsb = wb_ref.shape[0]
    cw = wb_ref.shape[1]
    half = wb_ref.shape[2] // 2
    per = half // LANES
    slabs = nsb * per
    t = pl.program_id(1)

    @pl.when(t == 0)
    def _():
        if has_init:
            xr_ref[...] = refs[6][...]
            xi_ref[...] = refs[7][...]
        else:
            xr_ref[...] = jnp.zeros_like(xr_ref)
            xi_ref[...] = jnp.zeros_like(xi_ref)

    for sb in range(nsb):
        ub = u_ref[:, sb * cw:(sb + 1) * cw].astype(BF16)
        res = jnp.dot(ub, wb_ref[sb], preferred_element_type=F32)
        for j in range(per):
            base = (sb * per + j) * pitch
            sr_ref[pl.ds(base, rows), :] = res[:, j * LANES:(j + 1) * LANES]
            si_ref[pl.ds(base, rows), :] = res[:, half + j * LANES:half + (j + 1) * LANES]

    ar = abr_ref[...]
    ai = abi_ref[...]

    def advance(n, tt, xr, xi):
        at = pl.ds(n * tl + tt, slabs, stride=pitch)
        nxr = ar * xr - ai * xi + sr_ref[at, :]
        nxi = ar * xi + ai * xr + si_ref[at, :]
        sr_ref[at, :] = nxr
        si_ref[at, :] = nxi
        return nxr, nxi

    par = 2 if nb % 2 == 0 else 1

    def group_scan(p, carry):
        seqs = [par * p + i for i in range(par)]

        def step(tt, x):
            out = ()
            for i, n in enumerate(seqs):
                out += advance(n, tt, x[2 * i], x[2 * i + 1])
            return out

        x0 = ()
        for n in seqs:
            x0 += (xr_ref[n], xi_ref[n])
        x = lax.fori_loop(0, tl, step, x0, unroll=SUBLANES)
        for i, n in enumerate(seqs):
            xr_ref[n] = x[2 * i]
            xi_ref[n] = x[2 * i + 1]
        return carry

    lax.fori_loop(0, nb // par, group_scan, 0)

    for sb in range(nsb):
        parts = [sr_ref[pl.ds((sb * per + j) * pitch, rows), :] for j in range(per)]
        parts += [si_ref[pl.ds((sb * per + j) * pitch, rows), :] for j in range(per)]
        xs = jnp.concatenate(parts, axis=1).astype(BF16)
        cs = slice(sb * cw, (sb + 1) * cw)
        y = jnp.dot(xs, wc_ref[sb], preferred_element_type=F32) + d_ref[:, cs] * u_ref[:, cs]
        y_ref[:, cs] = _gelu_tanh(y).astype(BF16)


def _s5(cols, wb, wc, ab_re, ab_im, d, inits, prevs, *, layer, depth, nseq, row0, col0, seq, nb, tl):
    slabs = ab_re.shape[0]
    nt = seq // tl
    blk = nb * tl
    rb0 = row0 // blk
    mix_w = d.shape[-1]
    cb = col0 // mix_w
    const3 = lambda a: pl.BlockSpec(a.shape, lambda b, t: (0, 0, 0))
    const2 = lambda a: pl.BlockSpec(a.shape, lambda b, t: (0, 0))
    sshape = (depth, nseq, slabs, LANES)
    xspec = _layer_state_spec(sshape, nb, layer)
    pitch = blk + SUBLANES
    args = [cols, wb, wc, ab_re, ab_im, d]
    st_specs, st_args, aliases = _state_args([xspec, xspec], inits, prevs, len(args))
    return pl.pallas_call(
        functools.partial(_s5_body, nb=nb, tl=tl, has_init=inits is not None, n_prev=len(aliases)),
        grid=(nseq // nb, nt),
        in_specs=[pl.BlockSpec((blk, mix_w), lambda b, t: (rb0 + b * nt + t, cb)),
                  const3(wb), const3(wc), const2(ab_re), const2(ab_im), const2(d)] + st_specs,
        out_specs=(pl.BlockSpec((blk, mix_w), lambda b, t: (rb0 + b * nt + t, 0)), xspec, xspec),
        out_shape=(jax.ShapeDtypeStruct((cols.shape[0], mix_w), BF16),
                   jax.ShapeDtypeStruct(sshape, F32),
                   jax.ShapeDtypeStruct(sshape, F32)),
        scratch_shapes=[pltpu.VMEM((slabs * pitch, LANES), F32),
                        pltpu.VMEM((slabs * pitch, LANES), F32)],
        input_output_aliases=aliases,
        compiler_params=_cparams(("parallel", "arbitrary")),
        name="s5",
    )(*args, *st_args)


def _s5_weights(bb_re, bb_im, c_re, c_im):
    ch, gp = bb_re.shape
    groups = c_re.shape[0]
    state = gp // groups
    nsb = groups // S5_SUPER
    eye = jnp.eye(S5_SUPER, dtype=F32)

    def in_w(bb):
        b4 = bb.reshape(ch, nsb, S5_SUPER, state)
        return jnp.einsum('csgp,gh->sgchp', b4, eye).reshape(nsb, S5_SUPER * ch, S5_SUPER * state)

    def out_w(c):
        c4 = c.reshape(nsb, S5_SUPER, ch, state)
        return jnp.einsum('sgcp,gh->sgphc', c4, eye).reshape(nsb, S5_SUPER * state, S5_SUPER * ch)

    wb = jnp.concatenate([in_w(bb_re), in_w(bb_im)], axis=2).astype(BF16)
    wc = jnp.concatenate([out_w(c_re), -out_w(c_im)], axis=1).astype(BF16)
    return wb, wc


def _mlproj_body(*refs, nb, tl, has_init, n_prev):
    x_ref, cw_ref, cb_ref, wq_ref, wk_ref, wv_ref, wg_ref, bg_ref = refs[:8]
    q_ref, k_ref, v_ref, g_ref, cn_ref, xx_ref = refs[8 + has_init + n_prev:]
    t = pl.program_id(1)
    width = x_ref.shape[1]
    halo = CONV_W - 1
    lo = SUBLANES - halo

    @pl.when(t == 0)
    def _():
        if has_init:
            xx_ref[:, lo:SUBLANES, :] = refs[8][...]
        else:
            xx_ref[:, lo:SUBLANES, :] = jnp.zeros((nb, halo, width), F32)

    @pl.when(t > 0)
    def _():
        xx_ref[:, lo:SUBLANES, :] = xx_ref[:, tl + lo:tl + SUBLANES, :]

    x = x_ref[...]
    xx_ref[:, SUBLANES:, :] = x.reshape(nb, tl, width)
    xc = jnp.zeros((nb, tl, width), F32) + cb_ref[...][None]
    for j in range(CONV_W):
        xc = xc + xx_ref[:, lo + j:lo + j + tl, :] * cw_ref[j:j + 1, :][None]
    cn_ref[...] = xx_ref[:, tl + lo:tl + SUBLANES, :]
    xc = _silu(xc).reshape(nb * tl, width).astype(BF16)
    xb = x.astype(BF16)

    dh = width // ML_HEADS
    gates = jnp.zeros((nb * tl, 2 * ML_HEADS), F32) + bg_ref[...]
    for h in range(ML_HEADS):
        hs = slice(h * dh, (h + 1) * dh)
        q = jnp.dot(xc[:, hs], wq_ref[h], preferred_element_type=F32)
        k = jnp.dot(xc[:, hs], wk_ref[h], preferred_element_type=F32)
        v = jnp.dot(xb[:, hs], wv_ref[h], preferred_element_type=F32)
        q_ref[:, hs] = q
        k_ref[:, hs] = k
        v_ref[:, hs] = v
        for part, val in enumerate((q, k, v)):
            w = wg_ref[part * width + h * dh:part * width + (h + 1) * dh, :]
            gates = gates + jnp.dot(val.astype(BF16), w, preferred_element_type=F32)
    g_ref[...] = gates


def _mlproj(cols, conv_w, conv_b, wq, wk, wv, wg, bg, init, prev, *, layer, depth, nseq, row0, col0,
            seq, nb, tl):
    width = conv_w.shape[1]
    halo = CONV_W - 1
    nt = seq // tl
    blk = nb * tl
    rb0 = row0 // blk
    cb = col0 // width
    ng = wg.shape[1]
    full = lambda a: pl.BlockSpec(a.shape, lambda b, t: (0,) * a.ndim)
    rowspec = lambda w: pl.BlockSpec((blk, w), lambda b, t: (b * nt + t, 0))
    sshape = (depth, nseq, halo, width)
    cspec = _layer_state_spec(sshape, nb, layer)
    args = [cols, conv_w, conv_b.reshape(1, width), wq, wk, wv, wg, bg.reshape(1, ng)]
    st_specs, st_args, aliases = _state_args(
        [cspec], None if init is None else (init,), (None, None, None, None, prev), len(args))
    return pl.pallas_call(
        functools.partial(_mlproj_body, nb=nb, tl=tl, has_init=init is not None,
                          n_prev=len(aliases)),
        grid=(nseq // nb, nt),
        in_specs=[pl.BlockSpec((blk, width), lambda b, t: (rb0 + b * nt + t, cb))]
                 + [full(a) for a in args[1:]] + st_specs,
        out_specs=(rowspec(width), rowspec(width), rowspec(width), rowspec(ng), cspec),
        out_shape=(jax.ShapeDtypeStruct((nseq * seq, width), F32),) * 3
                  + (jax.ShapeDtypeStruct((nseq * seq, ng), F32),
                     jax.ShapeDtypeStruct(sshape, F32)),
        scratch_shapes=[pltpu.VMEM((nb, tl + SUBLANES, width), F32)],
        input_output_aliases=aliases,
        compiler_params=_cparams(("parallel", "arbitrary")),
        name="mlproj",
    )(*args, *st_args)


def _mlrec_run(q_ref, k_ref, v_ref, g_ref, o_ref, nrm_ref, c_ref, n_ref, m_ref, hf_ref, h_ref,
               *, nb, tl, c, heads, unroll):
    dh = q_ref.shape[1] // heads
    glane = lax.broadcasted_iota(jnp.int32, (c, 2 * heads), 1)
    r = lax.broadcasted_iota(jnp.int32, (c, c), 0)
    s = lax.broadcasted_iota(jnp.int32, (c, c), 1)
    causal = r >= s
    chunks_per_seq = tl // c
    k_scale = dh ** -0.5

    def chunk(idx, carry):
        n = idx // chunks_per_seq
        rows = pl.ds(pl.multiple_of(idx * c, c), c)
        gt = g_ref[rows, :]
        for h in range(heads):
            hs = slice(h * dh, (h + 1) * dh)
            q = q_ref[rows, hs]
            k = k_ref[rows, hs] * k_scale
            v = v_ref[rows, hs]
            ig_col = jnp.sum(jnp.where(glane == h, gt, 0.0), axis=1, keepdims=True)
            fp_col = jnp.sum(jnp.where(glane == h + heads, gt, 0.0), axis=1, keepdims=True)
            lf_col = jnp.minimum(fp_col, 0.0) - jnp.log1p(jnp.exp(-jnp.abs(fp_col)))
            lf_row = jnp.sum(jnp.where(r == s, lf_col, 0.0), axis=0, keepdims=True)
            ig_row = jnp.sum(jnp.where(r == s, ig_col, 0.0), axis=0, keepdims=True)
            b_col = jnp.sum(jnp.where(causal, lf_row, 0.0), axis=1, keepdims=True)
            b_row = jnp.sum(jnp.where(r <= s, lf_col, 0.0), axis=0, keepdims=True)

            m_prev = m_ref[n, h][:, 0:1]
            n_prev = n_ref[n, h]
            c_prev = c_ref[n, h]

            logw = jnp.where(causal, b_col - b_row + ig_row, NEG_BIG)
            m_t = jnp.maximum(b_col + m_prev, jnp.max(logw, axis=1, keepdims=True))
            w_prev = jnp.exp(b_col + m_prev - m_t)
            w = jnp.exp(logw - m_t)
            qb = q.astype(BF16)
            vb = v.astype(BF16)
            sc = lax.dot_general(qb, k.astype(BF16), (((1,), (1,)), ((), ())),
                                 preferred_element_type=F32) * w
            num = (w_prev * jnp.dot(qb, c_prev.astype(BF16), preferred_element_type=F32)
                   + jnp.dot(sc.astype(BF16), vb, preferred_element_type=F32))
            den = (w_prev * jnp.sum(q * n_prev, axis=1, keepdims=True)
                   + jnp.sum(sc, axis=1, keepdims=True))
            hval = num / jnp.maximum(jnp.abs(den), jnp.exp(-m_t))

            m_end = m_t[c - 1:c, :]
            b_end = b_col[c - 1:c, :]
            g_prev = jnp.exp(b_end + m_prev - m_end)
            w_in = jnp.exp(b_end - b_col + ig_col - m_end)
            kw = k * w_in
            c_ref[n, h] = g_prev * c_prev + lax.dot_general(
                kw.astype(BF16), vb, (((0,), (0,)), ((), ())), preferred_element_type=F32)
            n_ref[n, h] = g_prev * n_prev + jnp.sum(kw, axis=0, keepdims=True)
            m_ref[n, h] = jnp.broadcast_to(m_end, (1, LANES))
            hf_ref[rows, hs] = (hval * _rms_scale(hval) * nrm_ref[:, hs]
                                * jax.nn.sigmoid(o_ref[rows, hs]))
        return carry

    lax.fori_loop(0, nb * chunks_per_seq, chunk, 0, unroll=unroll)
    h_ref[...] = hf_ref[...].astype(BF16)


def _mlrec_body(*refs, heads, n_prev, long_cfg, short_cfg):
    long_in, short_in = refs[0:5], refs[5:10]
    nrm_ref = refs[10]
    c0_ref, n0_ref, m0_ref = refs[11:14]
    (hl_ref, hs_ref, cl_ref, nl_ref, ml_ref, cs_ref, ns_ref, ms_ref, hfl_ref, hfs_ref
     ) = refs[14 + n_prev:]

    @pl.when(pl.program_id(1) == 0)
    def _():
        cl_ref[...] = jnp.zeros_like(cl_ref)
        nl_ref[...] = jnp.zeros_like(nl_ref)
        ml_ref[...] = jnp.zeros_like(ml_ref)

    _mlrec_run(*long_in, nrm_ref, cl_ref, nl_ref, ml_ref, hfl_ref, hl_ref, heads=heads, **long_cfg)
    cs_ref[...] = c0_ref[...]
    ns_ref[...] = n0_ref[...]
    ms_ref[...] = m0_ref[...]
    _mlrec_run(*short_in, nrm_ref, cs_ref, ns_ref, ms_ref, hfs_ref, hs_ref, heads=heads, **short_cfg)


def _mlrec(long_qkvg, short_qkvg, cols, nrm, short_init, prevs, *, layer, depth, heads, col0,
           batch, seq, tl, c, short_nseq, short_seq, short_row0, unroll):
    width = long_qkvg[0].shape[1]
    dh = width // heads
    nt = seq // tl
    steps = batch * nt
    nb_s = short_nseq // steps
    blk_s = nb_s * short_seq
    cb = col0 // width
    ng = long_qkvg[3].shape[1]
    rb0_s = short_row0 // blk_s
    step = lambda b, t: b * nt + t

    def group_specs(blk, rb0):
        rows = pl.BlockSpec((blk, width), lambda b, t: (step(b, t), 0))
        return [rows, rows, rows,
                pl.BlockSpec((blk, ng), lambda b, t: (step(b, t), 0)),
                pl.BlockSpec((blk, width), lambda b, t: (rb0 + step(b, t), cb))]

    def state_shapes(nseq):
        return ((depth, nseq, heads, dh, dh), (depth, nseq, heads, 1, dh), (depth, nseq, heads, 1, LANES))

    long_specs = [_layer_state_spec(sh, 1, layer) for sh in state_shapes(batch)]
    short_specs = [pl.BlockSpec((None, nb_s) + sh[2:],
                                lambda b, t, nd=len(sh): (layer, step(b, t)) + (0,) * (nd - 2))
                   for sh in state_shapes(short_nseq)]
    args = [*long_qkvg, cols, *short_qkvg, cols, nrm.reshape(1, width)]
    st_specs, st_args, aliases = _state_args(short_specs, short_init, (None, None) + tuple(prevs),
                                             len(args))
    out = pl.pallas_call(
        functools.partial(_mlrec_body, heads=heads, n_prev=len(aliases),
                          long_cfg=dict(nb=1, tl=tl, c=c, unroll=unroll),
                          short_cfg=dict(nb=nb_s, tl=short_seq, c=short_seq, unroll=True)),
        grid=(batch, nt),
        in_specs=group_specs(tl, 0) + group_specs(blk_s, rb0_s)
                 + [pl.BlockSpec((1, width), lambda b, t: (0, 0))] + st_specs,
        out_specs=(pl.BlockSpec((tl, width), lambda b, t: (step(b, t), 0)),
                   pl.BlockSpec((blk_s, width), lambda b, t: (step(b, t), 0)),
                   *long_specs, *short_specs),
        out_shape=(jax.ShapeDtypeStruct((cols.shape[0], width), BF16),
                   jax.ShapeDtypeStruct((short_nseq * short_seq, width), BF16))
                  + tuple(jax.ShapeDtypeStruct(sh, F32) for sh in state_shapes(batch))
                  + tuple(jax.ShapeDtypeStruct(sh, F32) for sh in state_shapes(short_nseq)),
        scratch_shapes=[pltpu.VMEM((tl, width), F32), pltpu.VMEM((blk_s, width), F32)],
        input_output_aliases=aliases,
        compiler_params=_cparams(("parallel", "arbitrary")),
        name="mlrec",
    )(*args, *st_args)
    return out[0], out[1], out[2:5], out[5:8]


def _merge_body(x_ref, oa_ref, ob_ref, oc_ref, ga_ref, gb_ref, gc_ref, wa_ref, wba_ref, wbb_ref,
                wc_ref, wo_ref, g_ref, o_ref, acc_ref):
    j = pl.program_id(1)

    @pl.when(j == 0)
    def _():
        acc_ref[...] = jnp.zeros_like(acc_ref)

    yb = ob_ref[...]
    br_a = jnp.dot(oa_ref[...], wa_ref[...], preferred_element_type=F32)
    br_b = (jnp.dot(yb, wba_ref[...], preferred_element_type=F32)
            * jax.nn.sigmoid(jnp.dot(yb, wbb_ref[...], preferred_element_type=F32)))
    br_c = jnp.dot(oc_ref[...], wc_ref[...], preferred_element_type=F32)
    merged = (jax.nn.sigmoid(ga_ref[...]) * br_a + jax.nn.sigmoid(gb_ref[...]) * br_b
              + jax.nn.sigmoid(gc_ref[...]) * br_c)
    acc_ref[...] += jnp.dot(merged.astype(BF16), wo_ref[...], preferred_element_type=F32)

    @pl.when(j == pl.num_programs(1) - 1)
    def _():
        _postnorm_residual_to(o_ref, x_ref, acc_ref, g_ref, 1.0)


def _merge(x, oa, ob, oc, cols, w_a, w_ba, w_bb, w_c, w_out, gains, k_gain, *, layer, col0,
           tm=512, tn=512):
    m, d = x.shape
    mix_w = oa.shape[1]
    nj = d // tn
    gb0 = col0 // tn
    act = pl.BlockSpec((tm, mix_w), lambda i, j: (i, 0))
    wspec = pl.BlockSpec((None, mix_w, tn), lambda i, j: (layer, 0, j))

    def gate(branch):
        return pl.BlockSpec((tm, tn), lambda i, j: (i, gb0 + branch * nj + j))

    return pl.pallas_call(
        _merge_body,
        grid=(m // tm, nj),
        in_specs=[pl.BlockSpec((tm, d), lambda i, j: (i, 0)), act, act, act,
                  gate(0), gate(1), gate(2), wspec, wspec, wspec, wspec,
                  pl.BlockSpec((None, tn, d), lambda i, j: (layer, j, 0)),
                  _gain_spec(gains, layer, k_gain)],
        out_specs=pl.BlockSpec((tm, d), lambda i, j: (i, 0)),
        out_shape=jax.ShapeDtypeStruct((m, d), F32),
        scratch_shapes=[pltpu.VMEM((tm, d), F32)],
        compiler_params=_cparams(("parallel", "arbitrary")),
        name="merge",
    )(x, oa, ob, oc, cols, cols, cols, w_a, w_ba, w_bb, w_c, w_out, gains)


def kernel(x_prompt, x_sample, state_hgrn, state_s5_re, state_s5_im, state_mlstm_c, state_mlstm_n, state_mlstm_m, state_mlstm_conv, norm_gains, w_ffn1_up, w_ffn1_down, w_in, hgrn_lower_bounds, hgrn_norm, w_hgrn_out, s5_a_re, s5_a_im, s5_log_dt, s5_b_re, s5_b_im, s5_c_re, s5_c_im, s5_d, w_s5_glu_a, w_s5_glu_b, mlstm_conv_w, mlstm_conv_b, mlstm_wq, mlstm_wk, mlstm_wv, mlstm_w_gates, mlstm_b_gates, mlstm_norm, w_mlstm_out, w_out, w_ffn2_up, w_ffn2_down):
    batch, seq, d_model = x_prompt.shape
    dec_batch, dec_seq, _ = x_sample.shape
    depth = w_in.shape[0]
    mix_w = w_hgrn_out.shape[1]
    n_prompt = batch * seq
    heads_hg = state_hgrn.shape[2]
    groups, state = s5_a_re.shape[1:]
    slabs = groups * state // LANES

    x = (x_prompt.reshape(n_prompt, d_model), x_sample.reshape(dec_batch * dec_seq, d_model))

    lb_all, ab_re, ab_im, bb_re, bb_im = _prep(hgrn_lower_bounds, s5_a_re, s5_a_im, s5_log_dt,
                                               s5_b_re, s5_b_im)

    col_su, col_mx, col_mo, col_gz = 4 * mix_w, 5 * mix_w, 6 * mix_w, 7 * mix_w
    dh = mix_w // ML_HEADS

    sample_init = dict(
        hg=state_hgrn,
        s5=(state_s5_re.reshape(depth, dec_batch, slabs, LANES),
            state_s5_im.reshape(depth, dec_batch, slabs, LANES)),
        conv=state_mlstm_conv,
        ml=(state_mlstm_c, state_mlstm_n.reshape(depth, dec_batch, ML_HEADS, 1, dh),
            jnp.broadcast_to(state_mlstm_m[..., None, None], (depth, dec_batch, ML_HEADS, 1, LANES))))
    groups_cfg = (
        dict(row0=0, nseq=batch, seq=seq, init=dict(s5=None, conv=None),
             s5=dict(nb=1, tl=256), mp=dict(nb=1, tl=512)),
        dict(row0=n_prompt, nseq=dec_batch, seq=dec_seq, init=sample_init,
             s5=dict(nb=16, tl=dec_seq), mp=dict(nb=16, tl=dec_seq)),
    )
    hgrn_cfg = dict(tl=512, c=16)
    mlrec_cfg = dict(tl=256, c=128, unroll=2)
    new = [dict(hg=None, s5=(None, None), conv=None, ml=(None, None, None)) for _ in groups_cfg]

    gains = norm_gains.reshape(depth * N_GAINS, 1, d_model)
    bf = lambda w: w.astype(BF16)
    w_ffn1_up, w_ffn1_down, w_ffn2_up, w_ffn2_down = map(bf, (w_ffn1_up, w_ffn1_down, w_ffn2_up, w_ffn2_down))
    w_hgrn_out, w_s5_glu_a, w_s5_glu_b, w_mlstm_out, w_out = map(
        bf, (w_hgrn_out, w_s5_glu_a, w_s5_glu_b, w_mlstm_out, w_out))

    for l in range(depth):
        x, h_mix = _ffn(x, gains, 0, 1, w_ffn1_up, w_ffn1_down, layer=l, rows0=n_prompt, k_next=2)
        cols = _inproj(h_mix, w_in, layer=l)

        wb, wc = _s5_weights(bb_re[l], bb_im[l], s5_c_re[l], s5_c_im[l])
        abr = ab_re[l].reshape(slabs, LANES)
        abi = ab_im[l].reshape(slabs, LANES)
        s5d = s5_d[l].reshape(1, mix_w)
        wq, wk, wv = (w[l].astype(BF16) for w in (mlstm_wq, mlstm_wk, mlstm_wv))
        wg = mlstm_w_gates[l].astype(BF16)

        both = dict(layer=l, batch=batch, seq=seq, short_nseq=dec_batch, short_seq=dec_seq,
                    short_row0=n_prompt)
        oa, oa_sample, new[0]['hg'], new[1]['hg'] = _hgrn(
            cols, lb_all, hgrn_norm, sample_init['hg'], (new[0]['hg'], new[1]['hg']),
            heads=heads_hg, **both, **hgrn_cfg)
        oa = lax.dynamic_update_slice(oa, oa_sample, (n_prompt, 0))

        ob = None
        qkvg = []
        for cfg, st in zip(groups_cfg, new):
            common = dict(layer=l, nseq=cfg['nseq'], row0=cfg['row0'])
            init = cfg['init']
            ob, *st['s5'] = _s5(cols, wb, wc, abr, abi, s5d, init['s5'], (ob, *st['s5']),
                                depth=depth, col0=col_su, seq=cfg['seq'], **common, **cfg['s5'])
            *group_qkvg, st['conv'] = _mlproj(
                cols, mlstm_conv_w[l], mlstm_conv_b[l], wq, wk, wv, wg, mlstm_b_gates[l],
                init['conv'], st['conv'], depth=depth, col0=col_mx, seq=cfg['seq'], **common,
                **cfg['mp'])
            qkvg.append(group_qkvg)
        oc, oc_sample, new[0]['ml'], new[1]['ml'] = _mlrec(
            qkvg[0], qkvg[1], cols, mlstm_norm[l], sample_init['ml'],
            tuple(new[0]['ml']) + tuple(new[1]['ml']), depth=depth, heads=ML_HEADS,
            col0=col_mo, **both, **mlrec_cfg)
        oc = lax.dynamic_update_slice(oc, oc_sample, (n_prompt, 0))

        x = _merge(x, oa, ob, oc, cols, w_hgrn_out, w_s5_glu_a, w_s5_glu_b, w_mlstm_out, w_out,
                   gains, 3, layer=l, col0=col_gz)
        x = _ffn(x, gains, 4, 5, w_ffn2_up, w_ffn2_down, layer=l, rows0=n_prompt,
                 split_out=l == depth - 1)

    y_prompt = x[0].reshape(batch, seq, d_model)
    y_sample = x[1].reshape(dec_batch, dec_seq, d_model)
    states = ()
    for cfg, st in zip(groups_cfg, new):
        nseq = cfg['nseq']
        c_new, n_new, m_new = st['ml']
        states += (st['hg'],
                   st['s5'][0].reshape(depth, nseq, groups, state),
                   st['s5'][1].reshape(depth, nseq, groups, state),
                   c_new, n_new.reshape(depth, nseq, ML_HEADS, dh), m_new[:, :, :, 0, 0], st['conv'])
    return (y_prompt, y_sample) + states
```
